```python
import jax, jax.numpy as jnp
from jax import lax
import numpy as np

D_MODEL = 1024
BATCH = 2
SEQ = 8192
DEPTH = 2

GRID_W = 64
CTX_LEN = 256
N_MODS = 9
FFN_HIDDEN = 2816
EPS = 1e-6

HEAD_DIM = 64
ATT_W = D_MODEL // 2
ATT_HEADS = ATT_W // HEAD_DIM
ATT_KV_HEADS = 2
ATT_GROUP = ATT_HEADS // ATT_KV_HEADS
WINDOW = 128
ATT_BLOCK = 128
ROPE_BASE = 10000.0

HG_W = D_MODEL // 4
HG_DV = 64
HG_HEADS = HG_W // HG_DV
HG_DK = 64
HG_KEY = HG_HEADS * HG_DK
HG_CHUNK = 64

RG_W = D_MODEL - ATT_W - HG_W
RG_BLOCKS = 4
RG_BW = RG_W // RG_BLOCKS
RG_CONV = 4
RG_CONV_LEFT = 2
RG_C = 8.0

SPLIT_SIZES = (ATT_W, ATT_KV_HEADS * HEAD_DIM, ATT_KV_HEADS * HEAD_DIM,
               HG_KEY, HG_KEY, HG_KEY, HG_W, HG_W,
               RG_W, RG_W)
IN_W = sum(SPLIT_SIZES)
SPLIT_POINTS = tuple(np.cumsum(SPLIT_SIZES)[:-1].tolist())
MIX_W = ATT_W + HG_W + RG_W

kernel_name = "hybrid_dit_hgrn2_rglru_swa_prefix"

F32 = jnp.float32


def _flip(t):
    return t[:, ::-1]


def _identity(t):
    return t


def rms_norm(x, g):
    xf = x.astype(F32)
    y = xf * lax.rsqrt(jnp.mean(xf * xf, axis=-1, keepdims=True) + EPS)
    return (y * g).astype(x.dtype)


def modulate(h, shift, scale):
    return h * (1 + scale) + shift


def adaln_mods(cond, w, b):
    m = jax.nn.silu(cond.reshape(-1, D_MODEL)) @ w + b
    m = m.reshape(-1, 1, N_MODS, D_MODEL)
    return [m[:, :, k] for k in range(N_MODS)]


def swiglu(h, w_in, w_out):
    a, b = jnp.split(h @ w_in, 2, axis=-1)
    return (jax.nn.silu(a) * b) @ w_out


def ffn_half(x, mods, g, w_in, w_out):
    shift, scale, gate = mods
    h = modulate(rms_norm(x, g), shift, scale)
    return x + 0.5 * gate * swiglu(h, w_in, w_out)


def axial_rope(rows):
    n_freq = HEAD_DIM // 4
    row = jnp.repeat(jnp.arange(rows), GRID_W)
    col = jnp.tile(jnp.arange(GRID_W), rows)
    inv_freq = ROPE_BASE ** (-jnp.arange(n_freq, dtype=F32) / n_freq)
    ang = jnp.stack([row[:, None] * inv_freq, col[:, None] * inv_freq], axis=1)
    return jnp.cos(ang), jnp.sin(ang)


def apply_rope(t, cos, sin):
    b, l, h, _ = t.shape
    tr = t.reshape(b, l, h, 2, 2, HEAD_DIM // 4)
    t1, t2 = tr[..., 0, :], tr[..., 1, :]
    c = cos[None, :, None].astype(t.dtype)
    s = sin[None, :, None].astype(t.dtype)
    out = jnp.stack([t1 * c - t2 * s, t2 * c + t1 * s], axis=-2)
    return out.reshape(t.shape)


def window_attention(qx, kx, vx, qc, kc, vc, sink, cos, sin, with_ctx_out):
    bsz, length, _ = qx.shape
    nb = length // ATT_BLOCK
    scale = HEAD_DIM ** -0.5
    qx = apply_rope(qx.reshape(bsz, length, ATT_HEADS, HEAD_DIM), cos, sin) * scale
    kx = apply_rope(kx.reshape(bsz, length, ATT_KV_HEADS, HEAD_DIM), cos, sin)
    vx = vx.reshape(bsz, length, ATT_KV_HEADS, HEAD_DIM)
    kc = kc.reshape(bsz, -1, ATT_KV_HEADS, HEAD_DIM)
    vc = vc.reshape(bsz, -1, ATT_KV_HEADS, HEAD_DIM)
    sink_f = sink.astype(F32).reshape(ATT_KV_HEADS, ATT_GROUP)

    def band(t):
        tp = jnp.pad(t, ((0, 0), (ATT_BLOCK, ATT_BLOCK), (0, 0), (0, 0)))
        tp = tp.reshape(bsz, nb + 2, ATT_BLOCK, ATT_KV_HEADS, HEAD_DIM)
        return jnp.concatenate([tp[:, :-2], tp[:, 1:-1], tp[:, 2:]], axis=2)

    qb = qx.reshape(bsz, nb, ATT_BLOCK, ATT_KV_HEADS, ATT_GROUP, HEAD_DIM)
    kb, vb = band(kx), band(vx)
    q_pos = jnp.arange(nb)[:, None] * ATT_BLOCK + jnp.arange(ATT_BLOCK)
    k_pos = jnp.arange(nb)[:, None] * ATT_BLOCK - ATT_BLOCK + jnp.arange(3 * ATT_BLOCK)
    kp = k_pos[:, None, :]
    valid = (jnp.abs(kp - q_pos[:, :, None]) <= WINDOW) & (kp >= 0) & (kp < length)

    s_loc = jnp.einsum('bnqhgd,bnkhd->bnhgqk', qb, kb, preferred_element_type=F32)
    s_loc = jnp.where(valid[None, :, None, None], s_loc, -jnp.inf)
    s_ctx = jnp.einsum('bnqhgd,bkhd->bnhgqk', qb, kc, preferred_element_type=F32)
    sink_b = sink_f[None, None, :, :, None, None]
    m = jnp.maximum(jnp.maximum(s_loc.max(-1, keepdims=True), s_ctx.max(-1, keepdims=True)), sink_b)
    p_loc = jnp.exp(s_loc - m)
    p_ctx = jnp.exp(s_ctx - m)
    denom = p_loc.sum(-1, keepdims=True) + p_ctx.sum(-1, keepdims=True) + jnp.exp(sink_b - m)
    o = (jnp.einsum('bnhgqk,bnkhd->bnqhgd', (p_loc / denom).astype(vx.dtype), vb)
         + jnp.einsum('bnhgqk,bkhd->bnqhgd', (p_ctx / denom).astype(vx.dtype), vc))
    yx = o.reshape(bsz, length, ATT_W)
    if not with_ctx_out:
        return yx, None
    qcg = (qc * scale).reshape(bsz, -1, ATT_KV_HEADS, ATT_GROUP, HEAD_DIM)
    s_cc = jnp.einsum('bqhgd,bkhd->bhgqk', qcg, kc, preferred_element_type=F32)
    sink_c = sink_f[None, :, :, None, None]
    m_c = jnp.maximum(s_cc.max(-1, keepdims=True), sink_c)
    p = jnp.exp(s_cc - m_c)
    p = p / (p.sum(-1, keepdims=True) + jnp.exp(sink_c - m_c))
    yc = jnp.einsum('bhgqk,bkhd->bqhgd', p.astype(vc.dtype), vc).reshape(bsz, -1, ATT_W)
    return yx, yc


def hgrn2_inputs(q, fz, i, lb):
    f = lb + (1 - lb) * jax.nn.sigmoid(fz.astype(F32))
    heads = lambda t: t.reshape(t.shape[0], t.shape[1], HG_HEADS, -1)
    return (heads(jax.nn.silu(q.astype(F32))), heads(1 - f), heads(i.astype(F32)), heads(jnp.log(f)))


def hgrn2_chunk_scan(q, k, v, logf, s0):
    bsz, length, nh, _ = q.shape
    n = length // HG_CHUNK

    def to_chunks(t):
        return t.reshape(bsz, n, HG_CHUNK, nh, t.shape[-1]).transpose(1, 0, 3, 2, 4)

    tri = jnp.tril(jnp.ones((HG_CHUNK, HG_CHUNK), dtype=bool))

    def step(state, blk):
        qc, kc, vc, gc = blk
        b = jnp.cumsum(gc, axis=2)
        o_inter = jnp.einsum('bhtk,bhkv->bhtv', qc * jnp.exp(b), state)
        rel = b[:, :, :, None, :] - b[:, :, None, :, :]
        decay = jnp.exp(jnp.where(tri[:, :, None], rel, -jnp.inf))
        att = jnp.einsum('bhtk,bhsk,bhtsk->bhts', qc, kc, decay)
        o_intra = jnp.einsum('bhts,bhsv->bhtv', att, vc)
        b_end = b[:, :, -1:, :]
        new_state = (jnp.exp(b_end[:, :, 0, :, None]) * state
                     + jnp.einsum('bhsk,bhsv->bhkv', kc * jnp.exp(b_end - b), vc))
        return new_state, o_inter + o_intra

    s_fin, o = lax.scan(step, s0, (to_chunks(q), to_chunks(k), to_chunks(v), to_chunks(logf)))
    o = o.transpose(1, 0, 3, 2, 4).reshape(bsz, length, nh, -1)
    return o, s_fin


def gated_head_norm(o, g, gate):
    y = o * lax.rsqrt(jnp.mean(o * o, axis=-1, keepdims=True) + EPS)
    y = y.reshape(o.shape[0], o.shape[1], -1)
    return (y * g * jax.nn.silu(gate.astype(F32))).astype(gate.dtype)


def hgrn2_mixer(lat, cxt, lb, norm_g, with_ctx_out):
    qx, ffx, fbx, ix, gx = lat
    qc, ffc, fbc, ic, gc = cxt
    s_zero = jnp.zeros((qx.shape[0], HG_HEADS, HG_DK, HG_DV), F32)
    outs_x, outs_c = [], []
    for d, (fzx, fzc) in enumerate(((ffx, ffc), (fbx, fbc))):
        rev = _flip if d == 1 else _identity
        o_c, state_c = hgrn2_chunk_scan(*[rev(t) for t in hgrn2_inputs(qc, fzc, ic, lb[d])], s_zero)
        o_x, _ = hgrn2_chunk_scan(*[rev(t) for t in hgrn2_inputs(qx, fzx, ix, lb[d])], state_c)
        outs_x.append(rev(o_x))
        outs_c.append(rev(o_c))
    yx = gated_head_norm(outs_x[0] + outs_x[1], norm_g, gx)
    yc = gated_head_norm(outs_c[0] + outs_c[1], norm_g, gc) if with_ctx_out else None
    return yx, yc


def centred_dwconv(x, w, b):
    length = x.shape[1]
    xp = jnp.pad(x, ((0, 0), (RG_CONV_LEFT, RG_CONV - 1 - RG_CONV_LEFT), (0, 0)))
    out = b
    for j in range(RG_CONV):
        out = out + xp[:, j:j + length] * w[j]
    return out


def rglru_scan(xs, wa, ba, wx, bx, lam, h0):
    bsz, length, _ = xs.shape
    xg = xs.reshape(bsz, length, RG_BLOCKS, RG_BW)
    r = jax.nn.sigmoid(jnp.einsum('blnd,nde->blne', xg, wa).reshape(bsz, length, RG_W) + ba)
    i = jax.nn.sigmoid(jnp.einsum('blnd,nde->blne', xg, wx).reshape(bsz, length, RG_W) + bx)
    log_a = -RG_C * r * jax.nn.softplus(-lam)
    a = jnp.exp(log_a)
    u = jnp.sqrt(-jnp.expm1(2.0 * log_a)) * (i * xs)
    u = u.at[:, 0].add(a[:, 0] * h0)

    def combine(p, q):
        a1, b1 = p
        a2, b2 = q
        return a1 * a2, a2 * b1 + b2

    _, h = lax.associative_scan(combine, (a, u), axis=1)
    return h


def rglru_mixer(lat, cxt, conv_w, conv_b, ga_w, ga_b, gx_w, gx_b, lam, with_ctx_out):
    xx, gate_x = lat
    xc, gate_c = cxt
    ux = centred_dwconv(xx.astype(F32), conv_w, conv_b)
    uc = centred_dwconv(xc.astype(F32), conv_w, conv_b)
    h0 = jnp.zeros((xx.shape[0], RG_W), F32)
    hx_list, hc_list = [], []
    for d in range(2):
        rev = _flip if d == 1 else _identity
        hc = rglru_scan(rev(uc), ga_w[d], ga_b[d], gx_w[d], gx_b[d], lam[d], h0)
        hx = rglru_scan(rev(ux), ga_w[d], ga_b[d], gx_w[d], gx_b[d], lam[d], hc[:, -1])
        hx_list.append(rev(hx))
        hc_list.append(rev(hc))
    yx = (jax.nn.gelu(gate_x.astype(F32)) * (hx_list[0] + hx_list[1])).astype(xx.dtype)
    if not with_ctx_out:
        return yx, None
    yc = (jax.nn.gelu(gate_c.astype(F32)) * (hc_list[0] + hc_list[1])).astype(xc.dtype)
    return yx, yc


def token_mixing(hx, hc, w_in, w_out, sink, lb, hg_norm, conv_w, conv_b, ga_w, ga_b, gx_w, gx_b, lam,
                 cos, sin, with_ctx_out):
    px = jnp.split(hx @ w_in, SPLIT_POINTS, axis=-1)
    pc = jnp.split(hc @ w_in, SPLIT_POINTS, axis=-1)
    ax, ac = window_attention(px[0], px[1], px[2], pc[0], pc[1], pc[2], sink, cos, sin, with_ctx_out)
    gx, gc = hgrn2_mixer(px[3:8], pc[3:8], lb, hg_norm, with_ctx_out)
    rx, rc = rglru_mixer(px[8:10], pc[8:10], conv_w, conv_b, ga_w, ga_b, gx_w, gx_b, lam, with_ctx_out)
    yx = jnp.concatenate([ax, gx, rx], axis=-1) @ w_out
    yc = jnp.concatenate([ac, gc, rc], axis=-1) @ w_out if with_ctx_out else None
    return yx, yc


def setup_inputs(seed: int = 0) -> dict:
    key = jax.random.key(seed)
    ks = jax.random.split(key, 32)
    D, F = D_MODEL, FFN_HIDDEN

    def nrm(k, shape, s):
        return jax.random.normal(k, shape, F32) * s

    def gain(k, shape):
        return 1.0 + 0.05 * jax.random.normal(k, shape, F32)

    u = jax.random.uniform(ks[21], (DEPTH, 2, RG_W), F32, minval=0.9, maxval=0.999)
    a = u ** (1.0 / RG_C)
    return {
        "x": nrm(ks[0], (BATCH, SEQ, D), 1.0),
        "c": nrm(ks[1], (BATCH, D), 1.0),
        "ctx": nrm(ks[2], (BATCH, CTX_LEN, D), 1.0),
        "c_ctx": nrm(ks[3], (D,), 1.0),
        "ada_w": nrm(ks[4], (DEPTH, D, N_MODS * D), 0.5 * D ** -0.5),
        "ada_b": nrm(ks[5], (DEPTH, N_MODS * D), 0.02),
        "norm_ffn1": gain(ks[6], (DEPTH, D)),
        "ffn1_w_in": nrm(ks[7], (DEPTH, D, 2 * F), D ** -0.5),
        "ffn1_w_out": nrm(ks[8], (DEPTH, F, D), F ** -0.5),
        "norm_mix": gain(ks[9], (DEPTH, D)),
        "w_in": nrm(ks[10], (DEPTH, D, IN_W), D ** -0.5),
        "w_out": nrm(ks[11], (DEPTH, MIX_W, D), MIX_W ** -0.5),
        "attn_sink": nrm(ks[12], (DEPTH, ATT_HEADS), 0.5),
        "hg_lb_logits": nrm(ks[13], (DEPTH, 2, HG_KEY), 0.5),
        "hg_norm": gain(ks[14], (DEPTH, HG_W)),
        "rg_conv_w": nrm(ks[15], (DEPTH, RG_CONV, RG_W), RG_CONV ** -0.5),
        "rg_conv_b": nrm(ks[16], (DEPTH, RG_W), 0.02),
        "rg_gate_a_w": nrm(ks[17], (DEPTH, 2, RG_BLOCKS, RG_BW, RG_BW), RG_BW ** -0.5),
        "rg_gate_a_b": nrm(ks[18], (DEPTH, 2, RG_W), 0.02),
        "rg_gate_x_w": nrm(ks[19], (DEPTH, 2, RG_BLOCKS, RG_BW, RG_BW), RG_BW ** -0.5),
        "rg_gate_x_b": nrm(ks[20], (DEPTH, 2, RG_W), 0.02),
        "rg_lambda": jnp.log(a) - jnp.log1p(-a),
        "norm_ffn2": gain(ks[22], (DEPTH, D)),
        "ffn2_w_in": nrm(ks[23], (DEPTH, D, 2 * F), D ** -0.5),
        "ffn2_w_out": nrm(ks[24], (DEPTH, F, D), F ** -0.5),
        "final_norm": gain(ks[25], (D,)),
    }


def reference(x, c, ctx, c_ctx, ada_w, ada_b, norm_ffn1, ffn1_w_in, ffn1_w_out, norm_mix, w_in, w_out,
              attn_sink, hg_lb_logits, hg_norm, rg_conv_w, rg_conv_b, rg_gate_a_w, rg_gate_a_b,
              rg_gate_x_w, rg_gate_x_b, rg_lambda, norm_ffn2, ffn2_w_in, ffn2_w_out, final_norm):
    rows = x.shape[1] // GRID_W
    cos, sin = axial_rope(rows)
    lb_p = jax.nn.softmax(hg_lb_logits.astype(F32), axis=0)
    hg_lb = jnp.cumsum(lb_p, axis=0) - lb_p[0]
    for l in range(DEPTH):
        with_ctx_out = l < DEPTH - 1
        mx = adaln_mods(c, ada_w[l], ada_b[l])
        mc = adaln_mods(c_ctx, ada_w[l], ada_b[l])
        x = ffn_half(x, mx[0:3], norm_ffn1[l], ffn1_w_in[l], ffn1_w_out[l])
        ctx = ffn_half(ctx, mc[0:3], norm_ffn1[l], ffn1_w_in[l], ffn1_w_out[l])
        hx = modulate(rms_norm(x, norm_mix[l]), mx[3], mx[4])
        hc = modulate(rms_norm(ctx, norm_mix[l]), mc[3], mc[4])
        yx, yc = token_mixing(hx, hc, w_in[l], w_out[l], attn_sink[l], hg_lb[l], hg_norm[l],
                              rg_conv_w[l], rg_conv_b[l], rg_gate_a_w[l], rg_gate_a_b[l],
                              rg_gate_x_w[l], rg_gate_x_b[l], rg_lambda[l], cos, sin, with_ctx_out)
        x = x + mx[5] * yx
        x = ffn_half(x, mx[6:9], norm_ffn2[l], ffn2_w_in[l], ffn2_w_out[l])
        if with_ctx_out:
            ctx = ctx + mc[5] * yc
            ctx = ffn_half(ctx, mc[6:9], norm_ffn2[l], ffn2_w_in[l], ffn2_w_out[l])
    return rms_norm(x, final_norm)
```

```python
import functools

import jax
import jax.numpy as jnp
from jax import lax
from jax.experimental import pallas as pl
from jax.experimental.pallas import tpu as pltpu

F32 = jnp.float32
BF16 = jnp.bfloat16

GRID_W = 64
CTX_LEN = 256
N_MODS = 9
EPS = 1e-6
HEAD_DIM = 64
ATT_KV_HEADS = 2
ATT_GROUP = 4
WINDOW = 128
ATT_BLOCK = 128
ROPE_BASE = 10000.0
HG_HEADS = 4
HG_DK = 64
HG_CHUNK = 16
RG_BLOCKS = 4
RG_CONV = 4
RG_C = 8.0

TILE = 256
HALO = 8
VMEM_LIMIT = 52 * 1024 * 1024


def _cparams(sem):
    return pltpu.CompilerParams(dimension_semantics=sem, vmem_limit_bytes=VMEM_LIMIT)


def _const_spec(shape):
    nd = len(shape)
    return pl.BlockSpec(shape, lambda *_: (0,) * nd, pipeline_mode=pl.Buffered(1))


def _rms(xf, g):
    return xf * lax.rsqrt(jnp.mean(xf * xf, axis=-1, keepdims=True) + EPS) * g


def _dot(a, b):
    return jnp.dot(a, b, preferred_element_type=F32)


def _dot_nt(a, b):
    return lax.dot_general(a, b, (((1,), (1,)), ((), ())), preferred_element_type=F32)


def _dot_tn(a, b):
    return lax.dot_general(a, b, (((0,), (0,)), ((), ())), preferred_element_type=F32)


def _adaln_kernel(c_ref, w_ref, b_ref, o_ref):
    s = jax.nn.silu(c_ref[...]).astype(BF16)
    o_ref[0] = _dot(s, w_ref[0].astype(BF16)) + b_ref[0]


def _adaln(cond, ada_w, ada_b):
    depth, d, n = ada_w.shape
    tn = n // 8
    return pl.pallas_call(
        _adaln_kernel,
        grid=(depth, n // tn),
        in_specs=[
            pl.BlockSpec(cond.shape, lambda l, j: (0, 0)),
            pl.BlockSpec((1, d, tn), lambda l, j: (l, 0, j)),
            pl.BlockSpec((1, 1, tn), lambda l, j: (l, 0, j)),
        ],
        out_specs=pl.BlockSpec((1, cond.shape[0], tn), lambda l, j: (l, 0, j)),
        out_shape=jax.ShapeDtypeStruct((depth, cond.shape[0], n), F32),
        compiler_params=_cparams(("arbitrary", "arbitrary")),
        name="adaln",
    )(cond, ada_w, ada_b.reshape(depth, 1, n))


def _ffn_kernel(*refs, mod0, has_mix, has_final):
    it = iter(refs)
    x_ref, mods_ref, g_ref, win_ref, wout_ref = (next(it) for _ in range(5))
    if has_mix:
        ya_ref, yg_ref, yr_ref, wmix_ref = (next(it) for _ in range(4))
    if has_final:
        fin_ref = next(it)
    o_ref = next(it)

    mods = mods_ref[0]
    x = x_ref[...]
    if has_mix:
        y = jnp.concatenate([ya_ref[...], yg_ref[...], yr_ref[...]], axis=-1)
        x = x + mods[5:6] * _dot(y, wmix_ref[...])
    shift, scale, gate = mods[mod0:mod0 + 1], mods[mod0 + 1:mod0 + 2], mods[mod0 + 2:mod0 + 3]
    h = (_rms(x, g_ref[...]) * (1.0 + scale) + shift).astype(BF16)
    f = wout_ref.shape[0]
    a = _dot(h, win_ref[:, :f])
    b = _dot(h, win_ref[:, f:])
    u = (jax.nn.silu(a) * b).astype(BF16)
    out = x + (0.5 * gate) * _dot(u, wout_ref[...])
    if has_final:
        out = _rms(out, fin_ref[...])
    o_ref[...] = out


def _ffn(xs, mods, g, w_in, w_out, *, mod0, n_ctx_tiles, tiles_per_batch, mix=None, final_g=None,
         latent_only=False):
    n, d = xs.shape
    skip = n_ctx_tiles if latent_only else 0
    nt = n // TILE - skip
    ctx_row = mods.shape[0] - 1

    def row(i):
        return (i + skip, 0)

    def mod_row(i):
        j = i + skip
        return (jnp.where(j < n_ctx_tiles, ctx_row, (j - n_ctx_tiles) // tiles_per_batch), 0, 0)

    in_specs = [
        pl.BlockSpec((TILE, d), row),
        pl.BlockSpec((1, N_MODS, d), mod_row),
        _const_spec((1, d)),
        _const_spec(w_in.shape),
        _const_spec(w_out.shape),
    ]
    args = [xs, mods, g.reshape(1, d), w_in, w_out]
    if mix is not None:
        ya, yg, yr, w_mix = mix
        in_specs += [pl.BlockSpec((TILE, ya.shape[1]), row), pl.BlockSpec((TILE, yg.shape[1]), row),
                     pl.BlockSpec((TILE, yr.shape[1]), row), _const_spec(w_mix.shape)]
        args += [ya, yg, yr, w_mix]
    if final_g is not None:
        in_specs.append(_const_spec((1, d)))
        args.append(final_g.reshape(1, d))
    return pl.pallas_call(
        functools.partial(_ffn_kernel, mod0=mod0, has_mix=mix is not None, has_final=final_g is not None),
        grid=(nt,),
        in_specs=in_specs,
        out_specs=pl.BlockSpec((TILE, d), lambda i: (i, 0)),
        out_shape=jax.ShapeDtypeStruct((nt * TILE, d), F32),
        compiler_params=_cparams(("arbitrary",)),
        name="ffn_mix" if mix is not None else "ffn",
    )(*args)


def _mix_in_kernel(x_ref, mods_ref, g_ref, w_ref, cos_ref, sa_ref, sb_ref,
                   q_ref, k_ref, v_ref, ph_ref, pr_ref):
    mods = mods_ref[0]
    h = (_rms(x_ref[...], g_ref[...]) * (1.0 + mods[4:5]) + mods[3:4]).astype(BF16)
    cos, sa, sb = cos_ref[...], sa_ref[...], sb_ref[...]

    def rope(t):
        w = t.shape[1]
        rep = w // cos.shape[1]
        c, a, b = (jnp.concatenate([m] * rep, axis=1) if rep > 1 else m for m in (cos, sa, sb))
        half = HEAD_DIM // 4
        return t * c + pltpu.roll(t, w - half, 1) * a + pltpu.roll(t, half, 1) * b

    nq = q_ref.shape[1]
    nk = k_ref.shape[1]
    o = 0
    q_ref[...] = (rope(_dot(h, w_ref[:, o:o + nq])) * (HEAD_DIM ** -0.5)).astype(BF16)
    o += nq
    k_ref[...] = rope(_dot(h, w_ref[:, o:o + nk])).astype(BF16)
    o += nk
    v_ref[...] = _dot(h, w_ref[:, o:o + nk]).astype(BF16)
    o += nk
    nh = ph_ref.shape[1]
    ph_ref[...] = _dot(h, w_ref[:, o:o + nh])
    o += nh
    pr_ref[...] = _dot(h, w_ref[:, o:])


def _mix_in(xs, mods, g, w_in, tabs, *, n_ctx_tiles, tiles_per_batch, widths):
    n, d = xs.shape
    nq, nk, nh, nr = widths
    ctx_row = mods.shape[0] - 1
    row = lambda i: (i, 0)

    def mod_row(i):
        return (jnp.where(i < n_ctx_tiles, ctx_row, (i - n_ctx_tiles) // tiles_per_batch), 0, 0)

    def tab_row(i):
        return (jnp.where(i < n_ctx_tiles, 0, 1 + (i - n_ctx_tiles) % tiles_per_batch), 0)

    tw = tabs[0].shape[1]
    return pl.pallas_call(
        _mix_in_kernel,
        grid=(n // TILE,),
        in_specs=[pl.BlockSpec((TILE, d), row), pl.BlockSpec((1, N_MODS, d), mod_row), _const_spec((1, d)),
                  _const_spec(w_in.shape)] + [pl.BlockSpec((TILE, tw), tab_row)] * 3,
        out_specs=[pl.BlockSpec((TILE, w), row) for w in (nq, nk, nk, nh, nr)],
        out_shape=[jax.ShapeDtypeStruct((n, nq), BF16), jax.ShapeDtypeStruct((n, nk), BF16),
                   jax.ShapeDtypeStruct((n, nk), BF16), jax.ShapeDtypeStruct((n, nh), F32),
                   jax.ShapeDtypeStruct((n, nr), F32)],
        compiler_params=_cparams(("arbitrary",)),
        name="mix_in",
    )(xs, mods, g.reshape(1, d), w_in, *tabs)


def _attn_kernel(sink_ref, q_ref, kc_ref, vc_ref, kp_ref, k0_ref, kn_ref, vp_ref, v0_ref, vn_ref, o_ref,
                 *, t0, n_ctx_blocks, length):
    t = pl.program_id(1) + t0
    n = t - n_ctx_blocks
    q_pos = n * ATT_BLOCK + lax.broadcasted_iota(jnp.int32, (ATT_BLOCK, 3 * ATT_BLOCK), 0)
    k_pos = (n - 1) * ATT_BLOCK + lax.broadcasted_iota(jnp.int32, (ATT_BLOCK, 3 * ATT_BLOCK), 1)
    valid = (jnp.abs(k_pos - q_pos) <= WINDOW) & (k_pos >= 0) & (k_pos < length) & (n >= 0)
    k_loc = jnp.concatenate([kp_ref[...], k0_ref[...], kn_ref[...]], axis=0)
    v_loc = jnp.concatenate([vp_ref[...], v0_ref[...], vn_ref[...]], axis=0)
    k_ctx, v_ctx = kc_ref[...], vc_ref[...]
    q = q_ref[...]
    for hq in range(ATT_KV_HEADS * ATT_GROUP):
        kv = hq // ATT_GROUP
        ks = slice(kv * HEAD_DIM, (kv + 1) * HEAD_DIM)
        qh = q[:, hq * HEAD_DIM:(hq + 1) * HEAD_DIM]
        s_loc = jnp.where(valid, _dot_nt(qh, k_loc[:, ks]), -jnp.inf)
        s_ctx = _dot_nt(qh, k_ctx[:, ks])
        sink = sink_ref[hq]
        m = jnp.maximum(jnp.maximum(s_loc.max(-1, keepdims=True), s_ctx.max(-1, keepdims=True)), sink)
        p_loc = jnp.exp(s_loc - m)
        p_ctx = jnp.exp(s_ctx - m)
        denom = p_loc.sum(-1, keepdims=True) + p_ctx.sum(-1, keepdims=True) + jnp.exp(sink - m)
        o = _dot(p_loc.astype(BF16), v_loc[:, ks]) + _dot(p_ctx.astype(BF16), v_ctx[:, ks])
        o_ref[:, hq * HEAD_DIM:(hq + 1) * HEAD_DIM] = (o / denom).astype(BF16)


def _attention(q, k, v, sink, *, bsz, length, with_ctx_out):
    n, nq = q.shape
    nk = k.shape[1]
    cb = CTX_LEN // ATT_BLOCK
    lb = length // ATT_BLOCK
    t0 = 0 if with_ctx_out else cb

    def q_row(b, i):
        t = i + t0
        return (jnp.where(t < cb, cb * b + t, cb * bsz + lb * b + (t - cb)), 0)

    def loc_row(off):
        def f(b, i):
            nn = jnp.clip(i + t0 - cb + off, 0, lb - 1)
            return (cb * bsz + lb * b + nn, 0)
        return f

    ctx_row = lambda b, i: (b, 0)
    kspec = [pl.BlockSpec((ATT_BLOCK, nk), loc_row(o)) for o in (-1, 0, 1)]
    return pl.pallas_call(
        functools.partial(_attn_kernel, t0=t0, n_ctx_blocks=cb, length=length),
        grid=(bsz, cb + lb - t0),
        in_specs=[pl.BlockSpec(memory_space=pltpu.SMEM), pl.BlockSpec((ATT_BLOCK, nq), q_row),
                  pl.BlockSpec((CTX_LEN, nk), ctx_row), pl.BlockSpec((CTX_LEN, nk), ctx_row)] + kspec + kspec,
        out_specs=pl.BlockSpec((ATT_BLOCK, nq), q_row),
        out_shape=jax.ShapeDtypeStruct((n, nq), BF16),
        compiler_params=_cparams(("arbitrary", "arbitrary")),
        name="attention",
    )(sink, q, k, v, k, k, k, v, v, v)


def _seq_tile(d, i, tiles_per_batch):
    return jnp.where(d == 0, i, jnp.where(i == 0, 0, tiles_per_batch + 1 - i))


def _seq_row(b, tile, bsz, tiles_per_batch):
    return jnp.where(tile == 0, b, bsz + tiles_per_batch * b + tile - 1)


def _shift_rows(x, s, rev):
    if s == 0:
        return x
    return pltpu.roll(x, (x.shape[0] - s) if rev else s, 0)


def _hgrn_kernel(lbl_ref, hq_ref, fz_ref, hi_ref, hg_ref, ng_ref, jmat_ref, bd_ref, o_ref,
                 st_ref, ofwd_ref, *, layer, tiles_per_batch):
    d = pl.program_id(1)
    i = pl.program_id(2)
    tile = _seq_tile(d, i, tiles_per_batch)
    tt, w = hq_ref.shape
    c = HG_CHUNK

    @pl.when(i == 0)
    def _():
        st_ref[...] = jnp.zeros_like(st_ref)

    logits = lbl_ref[:, 0, 0, :]
    e = jnp.exp(logits - logits.max(0, keepdims=True))
    lbp = e / e.sum(0, keepdims=True)
    lb = jnp.zeros((1, w), F32)
    for j in range(1, layer + 1):
        lb = lb + lbp[j:j + 1]

    q = jax.nn.silu(hq_ref[...])
    f = lb + (1.0 - lb) * jax.nn.sigmoid(fz_ref[...])
    kk = 1.0 - f
    g = jnp.log(f)
    v = hi_ref[...]
    jmat = jmat_ref[...]
    pos = lax.broadcasted_iota(jnp.int32, (tt, w), 0) % c

    def run(rev):
        b = g
        s = 1
        while s < c:
            ok = (pos <= c - 1 - s) if rev else (pos >= s)
            b = b + jnp.where(ok, _shift_rows(b, s, rev), 0.0)
            s *= 2
        o = jnp.zeros((tt, w), F32)
        for delta in range(c):
            ok = (pos <= c - 1 - delta) if rev else (pos >= delta)
            dec = jnp.exp(jnp.where(ok, b - _shift_rows(b, delta, rev), -jnp.inf))
            wgt = (q * _shift_rows(kk, delta, rev) * dec).astype(BF16)
            o = o + _dot(wgt, jmat) * _shift_rows(v, delta, rev)
        b3 = b.reshape(tt // c, c, w)
        b_end = b3[:, 0:1, :] if rev else b3[:, c - 1:c, :]
        q_in = (q * jnp.exp(b)).astype(BF16)
        k_out = (kk.reshape(tt // c, c, w) * jnp.exp(b_end - b3)).reshape(tt, w).astype(BF16)
        dec_end = jnp.exp(b_end)
        v16 = v.astype(BF16)
        bd = bd_ref[...]
        st = st_ref[...]
        parts = [None] * (tt // c)
        order = range(tt // c - 1, -1, -1) if rev else range(tt // c)
        for ci in order:
            rows = slice(ci * c, (ci + 1) * c)
            parts[ci] = _dot_nt(q_in[rows], st.astype(BF16))
            st = st * dec_end[ci] + bd * _dot_tn(v16[rows], k_out[rows])
        st_ref[...] = st
        return o + jnp.concatenate(parts, axis=0)

    @pl.when(d == 0)
    def _():
        ofwd_ref[tile] = run(False)

    @pl.when(d == 1)
    def _():
        tot = run(True) + ofwd_ref[tile]
        sq = tot * tot
        hi = sq.astype(BF16)
        lo = (sq - hi.astype(F32)).astype(BF16)
        ms = (_dot(hi, jmat) + _dot(lo, jmat)) * (1.0 / HG_DK)
        y = tot * lax.rsqrt(ms + EPS)
        o_ref[...] = (y * ng_ref[...] * jax.nn.silu(hg_ref[...])).astype(BF16)


def _hgrn(ph, lb_logits, norm_g, *, layer, bsz, tiles_per_batch):
    n = ph.shape[0]
    depth = lb_logits.shape[0]
    w = norm_g.shape[0]
    head = jnp.arange(w) // HG_DK
    same = head[:, None] == head[None, :]

    def col(cfn):
        def f(b, d, i):
            return (_seq_row(b, _seq_tile(d, i, tiles_per_batch), bsz, tiles_per_batch), cfn(d))
        return f

    def out_row(b, d, i):
        tile = jnp.where(d == 0, 0, _seq_tile(d, i, tiles_per_batch))
        return (_seq_row(b, tile, bsz, tiles_per_batch), 0)

    return pl.pallas_call(
        functools.partial(_hgrn_kernel, layer=layer, tiles_per_batch=tiles_per_batch),
        grid=(bsz, 2, tiles_per_batch + 1),
        in_specs=[
            pl.BlockSpec((depth, 1, 1, w), lambda b, d, i: (0, d, 0, 0)),
            pl.BlockSpec((TILE, w), col(lambda d: 0)),
            pl.BlockSpec((TILE, w), col(lambda d: 1 + d)),
            pl.BlockSpec((TILE, w), col(lambda d: 3)),
            pl.BlockSpec((TILE, w), col(lambda d: 4)),
            _const_spec((1, w)), _const_spec((w, w)), _const_spec((w, w)),
        ],
        out_specs=pl.BlockSpec((TILE, w), out_row),
        out_shape=jax.ShapeDtypeStruct((n, w), BF16),
        scratch_shapes=[pltpu.VMEM((w, w), F32), pltpu.VMEM((tiles_per_batch + 1, TILE, w), F32)],
        compiler_params=_cparams(("arbitrary", "arbitrary", "arbitrary")),
        name="hgrn2",
    )(lb_logits.reshape(depth, 2, 1, w), ph, ph, ph, ph, norm_g.reshape(1, w),
      same.astype(BF16), same.astype(F32))


def _rglru_kernel(x_ref, gate_ref, prev_ref, next_ref, cw_ref, cb_ref, wa_ref, ba_ref, wx_ref, bx_ref, lam_ref,
                  o_ref, h_ref, hfwd_ref, *, tiles_per_batch):
    d = pl.program_id(1)
    i = pl.program_id(2)
    tile = _seq_tile(d, i, tiles_per_batch)
    tt, w = x_ref.shape

    @pl.when(i == 0)
    def _():
        h_ref[...] = jnp.zeros_like(h_ref)

    x = x_ref[...]
    has_prev = (tile >= 2).astype(F32)
    has_next = jnp.logical_and(tile >= 1, tile < tiles_per_batch).astype(F32)
    prev = prev_ref[...] * has_prev
    nxt = next_ref[...] * has_next
    row = lax.broadcasted_iota(jnp.int32, (tt, w), 0)
    xm1 = jnp.where(row == 0, prev[HALO - 1:HALO], _shift_rows(x, 1, False))
    xm2 = jnp.where(row == 0, prev[HALO - 2:HALO - 1],
                    jnp.where(row == 1, prev[HALO - 1:HALO], _shift_rows(x, 2, False)))
    xp1 = jnp.where(row == tt - 1, nxt[0:1], _shift_rows(x, 1, True))
    cw = cw_ref[...]
    u = cb_ref[...] + xm2 * cw[0:1] + xm1 * cw[1:2] + x * cw[2:3] + xp1 * cw[3:4]

    u16 = u.astype(BF16)
    r = jax.nn.sigmoid(_dot(u16, wa_ref[0]) + ba_ref[0])
    ig = jax.nn.sigmoid(_dot(u16, wx_ref[0]) + bx_ref[0])
    log_a = -RG_C * r * jax.nn.softplus(-lam_ref[0])
    a = jnp.exp(log_a)
    z = jnp.sqrt(jnp.tanh(-log_a) * (1.0 + a * a)) * (ig * u)

    def scan(rev):
        aa, zz = a, z
        s = 1
        while s < tt:
            ok = (row <= tt - 1 - s) if rev else (row >= s)
            zz = zz + jnp.where(ok, aa * _shift_rows(zz, s, rev), 0.0)
            aa = aa * jnp.where(ok, _shift_rows(aa, s, rev), 1.0)
            s *= 2
        h = aa * h_ref[0:1, :] + zz
        last = h[0:1] if rev else h[tt - 1:tt]
        h_ref[...] = jnp.broadcast_to(last, h_ref.shape)
        return h

    @pl.when(d == 0)
    def _():
        hfwd_ref[tile] = scan(False)

    @pl.when(d == 1)
    def _():
        h = scan(True) + hfwd_ref[tile]
        o_ref[...] = (jax.nn.gelu(gate_ref[...]) * h).astype(BF16)


def _rglru(pr, conv_w, conv_b, wa, ba, wx, bx, lam, *, bsz, tiles_per_batch):
    n = pr.shape[0]
    w = conv_b.shape[0]
    hpt = TILE // HALO

    def seq_row(b, d, i):
        return _seq_row(b, _seq_tile(d, i, tiles_per_batch), bsz, tiles_per_batch)

    def out_row(b, d, i):
        tile = jnp.where(d == 0, 0, _seq_tile(d, i, tiles_per_batch))
        return (_seq_row(b, tile, bsz, tiles_per_batch), 0)

    per_dir = lambda b, d, i: (d, 0, 0)
    return pl.pallas_call(
        functools.partial(_rglru_kernel, tiles_per_batch=tiles_per_batch),
        grid=(bsz, 2, tiles_per_batch + 1),
        in_specs=[
            pl.BlockSpec((TILE, w), lambda b, d, i: (seq_row(b, d, i), 0)),
            pl.BlockSpec((TILE, w), lambda b, d, i: (seq_row(b, d, i), 1)),
            pl.BlockSpec((HALO, w), lambda b, d, i: (jnp.maximum(seq_row(b, d, i) * hpt - 1, 0), 0)),
            pl.BlockSpec((HALO, w), lambda b, d, i: (jnp.minimum((seq_row(b, d, i) + 1) * hpt, n // HALO - 1), 0)),
            _const_spec((RG_CONV, w)), _const_spec((1, w)),
            pl.BlockSpec((1, w, w), per_dir), pl.BlockSpec((1, 1, w), per_dir),
            pl.BlockSpec((1, w, w), per_dir), pl.BlockSpec((1, 1, w), per_dir),
            pl.BlockSpec((1, 1, w), per_dir),
        ],
        out_specs=pl.BlockSpec((TILE, w), out_row),
        out_shape=jax.ShapeDtypeStruct((n, w), BF16),
        scratch_shapes=[pltpu.VMEM((8, w), F32), pltpu.VMEM((tiles_per_batch + 1, TILE, w), F32)],
        compiler_params=_cparams(("arbitrary", "arbitrary", "arbitrary")),
        name="rglru",
    )(pr, pr, pr, pr, conv_w, conv_b.reshape(1, w), wa, ba.reshape(2, 1, w), wx, bx.reshape(2, 1, w),
      lam.reshape(2, 1, w))


def _rope_tables(length):
    n_freq = HEAD_DIM // 4
    rows = length // GRID_W
    row = jnp.repeat(jnp.arange(rows), GRID_W)
    col = jnp.tile(jnp.arange(GRID_W), rows)
    inv_freq = ROPE_BASE ** (-jnp.arange(n_freq, dtype=F32) / n_freq)
    ang = jnp.stack([row[:, None] * inv_freq, col[:, None] * inv_freq], axis=1)
    cos, sin = jnp.cos(ang), jnp.sin(ang)
    zero = jnp.zeros_like(sin[:, 0])
    cos_h = jnp.concatenate([cos[:, 0], cos[:, 0], cos[:, 1], cos[:, 1]], axis=-1)
    sa_h = jnp.concatenate([-sin[:, 0], zero, -sin[:, 1], zero], axis=-1)
    sb_h = jnp.concatenate([zero, sin[:, 0], zero, sin[:, 1]], axis=-1)
    ctx = (jnp.ones((CTX_LEN, HEAD_DIM), F32), jnp.zeros((CTX_LEN, HEAD_DIM), F32), jnp.zeros((CTX_LEN, HEAD_DIM), F32))
    return tuple(jnp.tile(jnp.concatenate([c, t], axis=0), (1, 2)) for c, t in zip(ctx, (cos_h, sa_h, sb_h)))


def _block_diag(wb):
    nb, bw = wb.shape[-3], wb.shape[-1]
    out = jnp.zeros(wb.shape[:-3] + (nb * bw, nb * bw), wb.dtype)
    for j in range(nb):
        out = out.at[..., j * bw:(j + 1) * bw, j * bw:(j + 1) * bw].set(wb[..., j, :, :])
    return out


def kernel(x, c, ctx, c_ctx, ada_w, ada_b, norm_ffn1, ffn1_w_in, ffn1_w_out, norm_mix, w_in, w_out, attn_sink, hg_lb_logits, hg_norm, rg_conv_w, rg_conv_b, rg_gate_a_w, rg_gate_a_b, rg_gate_x_w, rg_gate_x_b, rg_lambda, norm_ffn2, ffn2_w_in, ffn2_w_out, final_norm):
    bsz, length, d = x.shape
    depth = ada_w.shape[0]
    assert ctx.shape[1] == CTX_LEN == TILE and length % TILE == 0
    tpb = length // TILE
    n_ctx_tiles = bsz
    geo = dict(n_ctx_tiles=n_ctx_tiles, tiles_per_batch=tpb)

    xs = jnp.concatenate([ctx.reshape(bsz * CTX_LEN, d), x.reshape(bsz * length, d)], axis=0)
    cond = jnp.zeros((8, d), F32).at[:bsz].set(c).at[bsz].set(c_ctx)
    mods_all = _adaln(cond, ada_w, ada_b).reshape(depth, 8, N_MODS, d)[:, :bsz + 1]
    tabs = _rope_tables(length)

    att_w = w_out.shape[1] // 2
    kv_w = ATT_KV_HEADS * HEAD_DIM
    hg_w = hg_norm.shape[1]
    rg_w = rg_conv_b.shape[1]
    widths = (att_w, kv_w, 5 * hg_w, 2 * rg_w)
    assert sum(widths) + kv_w == w_in.shape[2]

    to16 = lambda t: t.astype(BF16)
    f1_in, f1_out, f2_in, f2_out, wi16, wo16 = map(to16, (ffn1_w_in, ffn1_w_out, ffn2_w_in, ffn2_w_out, w_in, w_out))
    wa_bd, wx_bd = to16(_block_diag(rg_gate_a_w)), to16(_block_diag(rg_gate_x_w))

    for l in range(depth):
        last = l == depth - 1
        mods = mods_all[l]
        xs = _ffn(xs, mods, norm_ffn1[l], f1_in[l], f1_out[l], mod0=0, **geo)
        q, k, v, ph, pr = _mix_in(xs, mods, norm_mix[l], wi16[l], tabs, widths=widths, **geo)
        ya = _attention(q, k, v, attn_sink[l], bsz=bsz, length=length, with_ctx_out=not last)
        yg = _hgrn(ph, hg_lb_logits, hg_norm[l], layer=l, bsz=bsz, tiles_per_batch=tpb)
        yr = _rglru(pr, rg_conv_w[l], rg_conv_b[l], wa_bd[l], rg_gate_a_b[l], wx_bd[l], rg_gate_x_b[l],
                    rg_lambda[l], bsz=bsz, tiles_per_batch=tpb)
        xs = _ffn(xs, mods, norm_ffn2[l], f2_in[l], f2_out[l], mod0=6, mix=(ya, yg, yr, wo16[l]),
                  final_g=final_norm if last else None, latent_only=last, **geo)
    return xs.reshape(bsz, length, d)
```

```python
import functools

import jax
import jax.numpy as jnp
import numpy as np
from jax import lax
from jax.experimental import pallas as pl
from jax.experimental.pallas import tpu as pltpu

F32 = jnp.float32
BF16 = jnp.bfloat16

GRID_W = 64
CTX_LEN = 256
N_MODS = 9
EPS = 1e-6
HEAD_DIM = 64
ATT_KV_HEADS = 2
ATT_GROUP = 4
WINDOW = 128
ATT_BLOCK = 128
ROPE_BASE = 10000.0
HG_HEADS = 4
HG_DK = 64
HG_CHUNK = 16
RG_BLOCKS = 4
RG_CONV = 4
RG_C = 8.0

TILE = 256
HALO = 8
VMEM_LIMIT = 52 * 1024 * 1024


def _cparams(sem):
    return pltpu.CompilerParams(dimension_semantics=sem, vmem_limit_bytes=VMEM_LIMIT)


def _const_spec(shape):
    nd = len(shape)
    return pl.BlockSpec(shape, lambda *_: (0,) * nd, pipeline_mode=pl.Buffered(1))


def _rms(xf, g):
    return xf * lax.rsqrt(jnp.mean(xf * xf, axis=-1, keepdims=True) + EPS) * g


def _dot(a, b):
    return jnp.dot(a, b, preferred_element_type=F32)


def _dot_nt(a, b):
    return lax.dot_general(a, b, (((1,), (1,)), ((), ())), preferred_element_type=F32)


def _dot_tn(a, b):
    return lax.dot_general(a, b, (((0,), (0,)), ((), ())), preferred_element_type=F32)


def _adaln_kernel(c_ref, w_ref, b_ref, o_ref):
    s = jax.nn.silu(c_ref[...]).astype(BF16)
    o_ref[0] = _dot(s, w_ref[0].astype(BF16)) + b_ref[0]


def _adaln(cond, ada_w, ada_b):
    depth, d, n = ada_w.shape
    tn = n // 8
    return pl.pallas_call(
        _adaln_kernel,
        grid=(depth, n // tn),
        in_specs=[
            pl.BlockSpec(cond.shape, lambda l, j: (0, 0)),
            pl.BlockSpec((1, d, tn), lambda l, j: (l, 0, j)),
            pl.BlockSpec((1, 1, tn), lambda l, j: (l, 0, j)),
        ],
        out_specs=pl.BlockSpec((1, cond.shape[0], tn), lambda l, j: (l, 0, j)),
        out_shape=jax.ShapeDtypeStruct((depth, cond.shape[0], n), F32),
        compiler_params=_cparams(("arbitrary", "arbitrary")),
        name="adaln",
    )(cond, ada_w, ada_b.reshape(depth, 1, n))


def _ffn_kernel(*refs, mod0, has_mix, has_final):
    it = iter(refs)
    x_ref, mods_ref, g_ref, win_ref, wout_ref = (next(it) for _ in range(5))
    if has_mix:
        ya_ref, yg_ref, yr_ref, wmix_ref = (next(it) for _ in range(4))
    if has_final:
        fin_ref = next(it)
    o_ref = next(it)

    mods = mods_ref[0]
    x = x_ref[...]
    if has_mix:
        y = jnp.concatenate([ya_ref[...], yg_ref[...], yr_ref[...]], axis=-1)
        x = x + mods[5:6] * _dot(y, wmix_ref[...])
    shift, scale, gate = mods[mod0:mod0 + 1], mods[mod0 + 1:mod0 + 2], mods[mod0 + 2:mod0 + 3]
    h = (_rms(x, g_ref[...]) * (1.0 + scale) + shift).astype(BF16)
    f = wout_ref.shape[0]
    a = _dot(h, win_ref[:, :f])
    b = _dot(h, win_ref[:, f:])
    u = (jax.nn.silu(a) * b).astype(BF16)
    out = x + (0.5 * gate) * _dot(u, wout_ref[...])
    if has_final:
        out = _rms(out, fin_ref[...])
    o_ref[...] = out


def _ffn(xs, mods, g, w_in, w_out, *, mod0, n_ctx_tiles, tiles_per_batch, mix=None, final_g=None,
         latent_only=False):
    n, d = xs.shape
    skip = n_ctx_tiles if latent_only else 0
    nt = n // TILE - skip
    ctx_row = mods.shape[0] - 1

    def row(i):
        return (i + skip, 0)

    def mod_row(i):
        j = i + skip
        return (jnp.where(j < n_ctx_tiles, ctx_row, (j - n_ctx_tiles) // tiles_per_batch), 0, 0)

    in_specs = [
        pl.BlockSpec((TILE, d), row),
        pl.BlockSpec((1, N_MODS, d), mod_row),
        _const_spec((1, d)),
        _const_spec(w_in.shape),
        _const_spec(w_out.shape),
    ]
    args = [xs, mods, g.reshape(1, d), w_in, w_out]
    if mix is not None:
        ya, yg, yr, w_mix = mix
        in_specs += [pl.BlockSpec((TILE, ya.shape[1]), row), pl.BlockSpec((TILE, yg.shape[1]), row),
                     pl.BlockSpec((TILE, yr.shape[1]), row), _const_spec(w_mix.shape)]
        args += [ya, yg, yr, w_mix]
    if final_g is not None:
        in_specs.append(_const_spec((1, d)))
        args.append(final_g.reshape(1, d))
    return pl.pallas_call(
        functools.partial(_ffn_kernel, mod0=mod0, has_mix=mix is not None, has_final=final_g is not None),
        grid=(nt,),
        in_specs=in_specs,
        out_specs=pl.BlockSpec((TILE, d), lambda i: (i, 0)),
        out_shape=jax.ShapeDtypeStruct((nt * TILE, d), F32),
        compiler_params=_cparams(("arbitrary",)),
        name="ffn_mix" if mix is not None else "ffn",
    )(*args)


def _mix_in_kernel(x_ref, mods_ref, g_ref, w_ref, cos_ref, sa_ref, sb_ref,
                   q_ref, k_ref, v_ref, ph_ref, pr_ref):
    mods = mods_ref[0]
    h = (_rms(x_ref[...], g_ref[...]) * (1.0 + mods[4:5]) + mods[3:4]).astype(BF16)
    cos, sa, sb = cos_ref[...], sa_ref[...], sb_ref[...]

    def rope(t):
        w = t.shape[1]
        rep = w // cos.shape[1]
        c, a, b = (jnp.concatenate([m] * rep, axis=1) if rep > 1 else m for m in (cos, sa, sb))
        half = HEAD_DIM // 4
        return t * c + pltpu.roll(t, w - half, 1) * a + pltpu.roll(t, half, 1) * b

    nq = q_ref.shape[1]
    nk = k_ref.shape[1]
    o = 0
    q_ref[...] = (rope(_dot(h, w_ref[:, o:o + nq])) * (HEAD_DIM ** -0.5)).astype(BF16)
    o += nq
    k_ref[...] = rope(_dot(h, w_ref[:, o:o + nk])).astype(BF16)
    o += nk
    v_ref[...] = _dot(h, w_ref[:, o:o + nk]).astype(BF16)
    o += nk
    nh = ph_ref.shape[1]
    ph_ref[...] = _dot(h, w_ref[:, o:o + nh])
    o += nh
    pr_ref[...] = _dot(h, w_ref[:, o:])


def _mix_in(xs, mods, g, w_in, tabs, *, n_ctx_tiles, tiles_per_batch, widths):
    n, d = xs.shape
    nq, nk, nh, nr = widths
    ctx_row = mods.shape[0] - 1
    row = lambda i: (i, 0)

    def mod_row(i):
        return (jnp.where(i < n_ctx_tiles, ctx_row, (i - n_ctx_tiles) // tiles_per_batch), 0, 0)

    def tab_row(i):
        return (jnp.where(i < n_ctx_tiles, 0, 1 + (i - n_ctx_tiles) % tiles_per_batch), 0)

    tw = tabs[0].shape[1]
    return pl.pallas_call(
        _mix_in_kernel,
        grid=(n // TILE,),
        in_specs=[pl.BlockSpec((TILE, d), row), pl.BlockSpec((1, N_MODS, d), mod_row), _const_spec((1, d)),
                  _const_spec(w_in.shape)] + [pl.BlockSpec((TILE, tw), tab_row)] * 3,
        out_specs=[pl.BlockSpec((TILE, w), row) for w in (nq, nk, nk, nh, nr)],
        out_shape=[jax.ShapeDtypeStruct((n, nq), BF16), jax.ShapeDtypeStruct((n, nk), BF16),
                   jax.ShapeDtypeStruct((n, nk), BF16), jax.ShapeDtypeStruct((n, nh), F32),
                   jax.ShapeDtypeStruct((n, nr), F32)],
        compiler_params=_cparams(("arbitrary",)),
        name="mix_in",
    )(xs, mods, g.reshape(1, d), w_in, *tabs)


def _attn_kernel(sink_ref, q_ref, kc_ref, vc_ref, kp_ref, k0_ref, kn_ref, vp_ref, v0_ref, vn_ref, o_ref,
                 *, t0, n_ctx_blocks, length):
    t = pl.program_id(1) + t0
    n = t - n_ctx_blocks
    k_loc = jnp.concatenate([kp_ref[...], k0_ref[...], kn_ref[...]], axis=0)
    v_loc = jnp.concatenate([vp_ref[...], v0_ref[...], vn_ref[...]], axis=0)
    k_ctx, v_ctx = kc_ref[...], vc_ref[...]
    sw = k_loc.shape[1]
    nslab = q_ref.shape[1] // sw
    q = jnp.concatenate([q_ref[:, g * sw:(g + 1) * sw] for g in range(nslab)], axis=0)
    k_pos = (n - 1) * ATT_BLOCK + lax.broadcasted_iota(jnp.int32, (3 * ATT_BLOCK, ATT_BLOCK), 0)
    q_pos = n * ATT_BLOCK + lax.broadcasted_iota(jnp.int32, (3 * ATT_BLOCK, ATT_BLOCK), 1)
    valid = (jnp.abs(k_pos - q_pos) <= WINDOW) & (k_pos >= 0) & (k_pos < length) & (n >= 0)
    valid = jnp.concatenate([valid] * nslab, axis=1)
    acc = jnp.zeros((sw, q.shape[0]), F32)
    for kv in range(ATT_KV_HEADS):
        def only(t):
            lane = lax.broadcasted_iota(jnp.int32, t.shape, 1)
            return jnp.where((lane >= kv * HEAD_DIM) & (lane < (kv + 1) * HEAD_DIM), t, jnp.zeros_like(t))
        sink = jnp.concatenate([jnp.full((1, ATT_BLOCK), sink_ref[kv * ATT_GROUP + g], F32) for g in range(nslab)],
                               axis=1)
        s_loc = jnp.where(valid, _dot_nt(only(k_loc), q), -jnp.inf)
        s_ctx = _dot_nt(only(k_ctx), q)
        m = jnp.maximum(jnp.maximum(s_loc.max(0, keepdims=True), s_ctx.max(0, keepdims=True)), sink)
        p_loc = jnp.exp(s_loc - m)
        p_ctx = jnp.exp(s_ctx - m)
        denom = p_loc.sum(0, keepdims=True) + p_ctx.sum(0, keepdims=True) + jnp.exp(sink - m)
        o = _dot_tn(only(v_loc), p_loc.astype(BF16)) + _dot_tn(only(v_ctx), p_ctx.astype(BF16))
        acc = acc + o / denom
    out = acc.T
    for g in range(nslab):
        o_ref[:, g * sw:(g + 1) * sw] = out[g * ATT_BLOCK:(g + 1) * ATT_BLOCK].astype(BF16)


def _attention(q, k, v, sink, *, bsz, length, with_ctx_out):
    n, nq = q.shape
    nk = k.shape[1]
    cb = CTX_LEN // ATT_BLOCK
    lb = length // ATT_BLOCK
    t0 = 0 if with_ctx_out else cb

    def q_row(b, i):
        t = i + t0
        return (jnp.where(t < cb, cb * b + t, cb * bsz + lb * b + (t - cb)), 0)

    def loc_row(off):
        def f(b, i):
            nn = jnp.clip(i + t0 - cb + off, 0, lb - 1)
            return (cb * bsz + lb * b + nn, 0)
        return f

    ctx_row = lambda b, i: (b, 0)
    kspec = [pl.BlockSpec((ATT_BLOCK, nk), loc_row(o)) for o in (-1, 0, 1)]
    return pl.pallas_call(
        functools.partial(_attn_kernel, t0=t0, n_ctx_blocks=cb, length=length),
        grid=(bsz, cb + lb - t0),
        in_specs=[pl.BlockSpec(memory_space=pltpu.SMEM), pl.BlockSpec((ATT_BLOCK, nq), q_row),
                  pl.BlockSpec((CTX_LEN, nk), ctx_row), pl.BlockSpec((CTX_LEN, nk), ctx_row)] + kspec + kspec,
        out_specs=pl.BlockSpec((ATT_BLOCK, nq), q_row),
        out_shape=jax.ShapeDtypeStruct((n, nq), BF16),
        compiler_params=_cparams(("arbitrary", "arbitrary")),
        name="attention",
    )(sink, q, k, v, k, k, k, v, v, v)


def _seq_tile(d, i, tiles_per_batch):
    return jnp.where(d == 0, i, jnp.where(i == 0, 0, tiles_per_batch + 1 - i))


def _seq_row(b, tile, bsz, tiles_per_batch):
    return jnp.where(tile == 0, b, bsz + tiles_per_batch * b + tile - 1)


def _shift_rows(x, s, rev):
    if s == 0:
        return x
    return pltpu.roll(x, (x.shape[0] - s) if rev else s, 0)


def _hgrn_kernel(lbl_ref, hq_ref, fz_ref, hi_ref, hg_ref, ng_ref, jmat_ref, bd_ref, o_ref,
                 st_ref, ofwd_ref, upd_ref, sin_ref, *, layer, tiles_per_batch):
    d = pl.program_id(1)
    i = pl.program_id(2)
    tile = _seq_tile(d, i, tiles_per_batch)
    tt, w = hq_ref.shape
    c = HG_CHUNK

    @pl.when(i == 0)
    def _():
        st_ref[...] = jnp.zeros_like(st_ref)

    logits = lbl_ref[:, 0, 0, :]
    e = jnp.exp(logits - logits.max(0, keepdims=True))
    lbp = e / e.sum(0, keepdims=True)
    lb = jnp.zeros((1, w), F32)
    for j in range(1, layer + 1):
        lb = lb + lbp[j:j + 1]

    q = jax.nn.silu(hq_ref[...])
    f = lb + (1.0 - lb) * jax.nn.sigmoid(fz_ref[...])
    kk = 1.0 - f
    v = hi_ref[...]
    jmat = jmat_ref[...]
    pos = lax.broadcasted_iota(jnp.int32, (tt, w), 0) % c

    def run(rev):
        def in_chunk(s, r):
            return (pos <= c - 1 - s) if r else (pos >= s)

        def cumprod(y, r):
            s = 1
            while s < c:
                y = y * jnp.where(in_chunk(s, r), _shift_rows(y, s, r), 1.0)
                s *= 2
            return y

        p_in = cumprod(f, rev)
        p_out = cumprod(jnp.where(in_chunk(1, not rev), _shift_rows(f, 1, not rev), 1.0), not rev)
        o = jnp.zeros((tt, w), F32)
        qd = q
        for delta in range(c):
            fs = jnp.where(in_chunk(delta, rev), _shift_rows(f, delta, rev), 1.0)
            nxt = qd * fs
            o = o + _dot((qd - nxt).astype(BF16), jmat) * _shift_rows(v, delta, rev)
            qd = nxt
        p3 = p_in.reshape(tt // c, c, w)
        dec_end = p3[:, 0:1, :] if rev else p3[:, c - 1:c, :]
        q_in = (q * p_in).astype(BF16)
        k_out = (kk * p_out).astype(BF16)
        v16 = v.astype(BF16)
        bd = bd_ref[...]
        nchunk = tt // c
        rows = [slice(ci * c, (ci + 1) * c) for ci in range(nchunk)]
        for ci in range(nchunk):
            upd_ref[ci] = _dot_tn(v16[rows[ci]], k_out[rows[ci]])
        st = st_ref[...]
        for ci in (range(nchunk - 1, -1, -1) if rev else range(nchunk)):
            sin_ref[ci] = (st * bd).astype(BF16)
            st = st * dec_end[ci] + upd_ref[ci]
        st_ref[...] = st
        parts = [_dot_nt(q_in[rows[ci]], sin_ref[ci]) for ci in range(nchunk)]
        return o + jnp.concatenate(parts, axis=0)

    @pl.when(d == 0)
    def _():
        ofwd_ref[tile] = run(False)

    @pl.when(d == 1)
    def _():
        tot = run(True) + ofwd_ref[tile]
        sq = tot * tot
        hi = sq.astype(BF16)
        lo = (sq - hi.astype(F32)).astype(BF16)
        ms = (_dot(hi, jmat) + _dot(lo, jmat)) * (1.0 / HG_DK)
        y = tot * lax.rsqrt(ms + EPS)
        o_ref[...] = (y * ng_ref[...] * jax.nn.silu(hg_ref[...])).astype(BF16)


def _hgrn(ph, lb_logits, norm_g, *, layer, bsz, tiles_per_batch):
    n = ph.shape[0]
    depth = lb_logits.shape[0]
    w = norm_g.shape[0]
    head = jnp.arange(w) // HG_DK
    same = head[:, None] == head[None, :]

    def col(cfn):
        def f(b, d, i):
            return (_seq_row(b, _seq_tile(d, i, tiles_per_batch), bsz, tiles_per_batch), cfn(d))
        return f

    def out_row(b, d, i):
        tile = jnp.where(d == 0, 0, _seq_tile(d, i, tiles_per_batch))
        return (_seq_row(b, tile, bsz, tiles_per_batch), 0)

    return pl.pallas_call(
        functools.partial(_hgrn_kernel, layer=layer, tiles_per_batch=tiles_per_batch),
        grid=(bsz, 2, tiles_per_batch + 1),
        in_specs=[
            pl.BlockSpec((depth, 1, 1, w), lambda b, d, i: (0, d, 0, 0)),
            pl.BlockSpec((TILE, w), col(lambda d: 0)),
            pl.BlockSpec((TILE, w), col(lambda d: 1 + d)),
            pl.BlockSpec((TILE, w), col(lambda d: 3)),
            pl.BlockSpec((TILE, w), col(lambda d: 4)),
            _const_spec((1, w)), _const_spec((w, w)), _const_spec((w, w)),
        ],
        out_specs=pl.BlockSpec((TILE, w), out_row),
        out_shape=jax.ShapeDtypeStruct((n, w), BF16),
        scratch_shapes=[pltpu.VMEM((w, w), F32), pltpu.VMEM((tiles_per_batch + 1, TILE, w), F32),
                        pltpu.VMEM((TILE // HG_CHUNK, w, w), F32), pltpu.VMEM((TILE // HG_CHUNK, w, w), BF16)],
        compiler_params=_cparams(("arbitrary", "arbitrary", "arbitrary")),
        name="hgrn2",
    )(lb_logits.reshape(depth, 2, 1, w), ph, ph, ph, ph, norm_g.reshape(1, w),
      same.astype(BF16), same.astype(F32))


def _rglru_kernel(x_ref, gate_ref, prev_ref, next_ref, cw_ref, cb_ref, wa_ref, ba_ref, wx_ref, bx_ref, lam_ref,
                  o_ref, h_ref, hfwd_ref, *, tiles_per_batch):
    d = pl.program_id(1)
    i = pl.program_id(2)
    tile = _seq_tile(d, i, tiles_per_batch)
    tt, w = x_ref.shape

    @pl.when(i == 0)
    def _():
        h_ref[...] = jnp.zeros_like(h_ref)

    x = x_ref[...]
    has_prev = (tile >= 2).astype(F32)
    has_next = jnp.logical_and(tile >= 1, tile < tiles_per_batch).astype(F32)
    prev = prev_ref[...] * has_prev
    nxt = next_ref[...] * has_next
    row = lax.broadcasted_iota(jnp.int32, (tt, w), 0)
    xm1 = jnp.where(row == 0, prev[HALO - 1:HALO], _shift_rows(x, 1, False))
    xm2 = jnp.where(row == 0, prev[HALO - 2:HALO - 1],
                    jnp.where(row == 1, prev[HALO - 1:HALO], _shift_rows(x, 2, False)))
    xp1 = jnp.where(row == tt - 1, nxt[0:1], _shift_rows(x, 1, True))
    cw = cw_ref[...]
    u = cb_ref[...] + xm2 * cw[0:1] + xm1 * cw[1:2] + x * cw[2:3] + xp1 * cw[3:4]

    u16 = u.astype(BF16)
    r = jax.nn.sigmoid(_dot(u16, wa_ref[0]) + ba_ref[0])
    ig = jax.nn.sigmoid(_dot(u16, wx_ref[0]) + bx_ref[0])
    log_a = -RG_C * r * jax.nn.softplus(-lam_ref[0])
    a = jnp.exp(log_a)
    z = jnp.sqrt(jnp.tanh(-log_a) * (1.0 + a * a)) * (ig * u)

    def scan(rev):
        aa, zz = a, z
        s = 1
        while s < tt:
            ok = (row <= tt - 1 - s) if rev else (row >= s)
            zz = zz + jnp.where(ok, aa * _shift_rows(zz, s, rev), 0.0)
            aa = aa * jnp.where(ok, _shift_rows(aa, s, rev), 1.0)
            s *= 2
        h = aa * h_ref[0:1, :] + zz
        last = h[0:1] if rev else h[tt - 1:tt]
        h_ref[...] = jnp.broadcast_to(last, h_ref.shape)
        return h

    @pl.when(d == 0)
    def _():
        hfwd_ref[tile] = scan(False)

    @pl.when(d == 1)
    def _():
        h = scan(True) + hfwd_ref[tile]
        o_ref[...] = (jax.nn.gelu(gate_ref[...]) * h).astype(BF16)


def _rglru(pr, conv_w, conv_b, wa, ba, wx, bx, lam, *, bsz, tiles_per_batch):
    n = pr.shape[0]
    w = conv_b.shape[0]
    hpt = TILE // HALO

    def seq_row(b, d, i):
        return _seq_row(b, _seq_tile(d, i, tiles_per_batch), bsz, tiles_per_batch)

    def out_row(b, d, i):
        tile = jnp.where(d == 0, 0, _seq_tile(d, i, tiles_per_batch))
        return (_seq_row(b, tile, bsz, tiles_per_batch), 0)

    per_dir = lambda b, d, i: (d, 0, 0)
    return pl.pallas_call(
        functools.partial(_rglru_kernel, tiles_per_batch=tiles_per_batch),
        grid=(bsz, 2, tiles_per_batch + 1),
        in_specs=[
            pl.BlockSpec((TILE, w), lambda b, d, i: (seq_row(b, d, i), 0)),
            pl.BlockSpec((TILE, w), lambda b, d, i: (seq_row(b, d, i), 1)),
            pl.BlockSpec((HALO, w), lambda b, d, i: (jnp.maximum(seq_row(b, d, i) * hpt - 1, 0), 0)),
            pl.BlockSpec((HALO, w), lambda b, d, i: (jnp.minimum((seq_row(b, d, i) + 1) * hpt, n // HALO - 1), 0)),
            _const_spec((RG_CONV, w)), _const_spec((1, w)),
            pl.BlockSpec((1, w, w), per_dir), pl.BlockSpec((1, 1, w), per_dir),
            pl.BlockSpec((1, w, w), per_dir), pl.BlockSpec((1, 1, w), per_dir),
            pl.BlockSpec((1, 1, w), per_dir),
        ],
        out_specs=pl.BlockSpec((TILE, w), out_row),
        out_shape=jax.ShapeDtypeStruct((n, w), BF16),
        scratch_shapes=[pltpu.VMEM((8, w), F32), pltpu.VMEM((tiles_per_batch + 1, TILE, w), F32)],
        compiler_params=_cparams(("arbitrary", "arbitrary", "arbitrary")),
        name="rglru",
    )(pr, pr, pr, pr, conv_w, conv_b.reshape(1, w), wa, ba.reshape(2, 1, w), wx, bx.reshape(2, 1, w),
      lam.reshape(2, 1, w))


def _rope_tables(length):
    n_freq = HEAD_DIM // 4
    rows = length // GRID_W
    row = jnp.repeat(jnp.arange(rows), GRID_W)
    col = jnp.tile(jnp.arange(GRID_W), rows)
    inv_freq = ROPE_BASE ** (-jnp.arange(n_freq, dtype=F32) / n_freq)
    ang = jnp.stack([row[:, None] * inv_freq, col[:, None] * inv_freq], axis=1)
    cos, sin = jnp.cos(ang), jnp.sin(ang)
    zero = jnp.zeros_like(sin[:, 0])
    cos_h = jnp.concatenate([cos[:, 0], cos[:, 0], cos[:, 1], cos[:, 1]], axis=-1)
    sa_h = jnp.concatenate([-sin[:, 0], zero, -sin[:, 1], zero], axis=-1)
    sb_h = jnp.concatenate([zero, sin[:, 0], zero, sin[:, 1]], axis=-1)
    ctx = (jnp.ones((CTX_LEN, HEAD_DIM), F32), jnp.zeros((CTX_LEN, HEAD_DIM), F32), jnp.zeros((CTX_LEN, HEAD_DIM), F32))
    return tuple(jnp.tile(jnp.concatenate([c, t], axis=0), (1, 2)) for c, t in zip(ctx, (cos_h, sa_h, sb_h)))


def _block_diag(wb):
    nb, bw = wb.shape[-3], wb.shape[-1]
    out = jnp.zeros(wb.shape[:-3] + (nb * bw, nb * bw), wb.dtype)
    for j in range(nb):
        out = out.at[..., j * bw:(j + 1) * bw, j * bw:(j + 1) * bw].set(wb[..., j, :, :])
    return out


def kernel(x, c, ctx, c_ctx, ada_w, ada_b, norm_ffn1, ffn1_w_in, ffn1_w_out, norm_mix, w_in, w_out, attn_sink, hg_lb_logits, hg_norm, rg_conv_w, rg_conv_b, rg_gate_a_w, rg_gate_a_b, rg_gate_x_w, rg_gate_x_b, rg_lambda, norm_ffn2, ffn2_w_in, ffn2_w_out, final_norm):
    bsz, length, d = x.shape
    depth = ada_w.shape[0]
    assert ctx.shape[1] == CTX_LEN == TILE and length % TILE == 0
    tpb = length // TILE
    n_ctx_tiles = bsz
    geo = dict(n_ctx_tiles=n_ctx_tiles, tiles_per_batch=tpb)

    xs = jnp.concatenate([ctx.reshape(bsz * CTX_LEN, d), x.reshape(bsz * length, d)], axis=0)
    cond = jnp.zeros((8, d), F32).at[:bsz].set(c).at[bsz].set(c_ctx)
    mods_all = _adaln(cond, ada_w, ada_b).reshape(depth, 8, N_MODS, d)[:, :bsz + 1]
    tabs = _rope_tables(length)

    att_w = w_out.shape[1] // 2
    kv_w = ATT_KV_HEADS * HEAD_DIM
    hg_w = hg_norm.shape[1]
    rg_w = rg_conv_b.shape[1]
    widths = (att_w, kv_w, 5 * hg_w, 2 * rg_w)
    assert sum(widths) + kv_w == w_in.shape[2]

    new = np.arange(att_w)
    grp, kvh, dim = new // kv_w, (new % kv_w) // HEAD_DIM, new % HEAD_DIM
    old = (kvh * ATT_GROUP + grp) * HEAD_DIM + dim
    w_in = w_in[:, :, np.concatenate([old, np.arange(att_w, w_in.shape[2])])]
    w_out = w_out[:, np.concatenate([old, np.arange(att_w, w_out.shape[1])]), :]

    to16 = lambda t: t.astype(BF16)
    f1_in, f1_out, f2_in, f2_out, wi16, wo16 = map(to16, (ffn1_w_in, ffn1_w_out, ffn2_w_in, ffn2_w_out, w_in, w_out))
    wa_bd, wx_bd = to16(_block_diag(rg_gate_a_w)), to16(_block_diag(rg_gate_x_w))

    for l in range(depth):
        last = l == depth - 1
        mods = mods_all[l]
        xs = _ffn(xs, mods, norm_ffn1[l], f1_in[l], f1_out[l], mod0=0, **geo)
        q, k, v, ph, pr = _mix_in(xs, mods, norm_mix[l], wi16[l], tabs, widths=widths, **geo)
        ya = _attention(q, k, v, attn_sink[l], bsz=bsz, length=length, with_ctx_out=not last)
        yg = _hgrn(ph, hg_lb_logits, hg_norm[l], layer=l, bsz=bsz, tiles_per_batch=tpb)
        yr = _rglru(pr, rg_conv_w[l], rg_conv_b[l], wa_bd[l], rg_gate_a_b[l], wx_bd[l], rg_gate_x_b[l],
                    rg_lambda[l], bsz=bsz, tiles_per_batch=tpb)
        xs = _ffn(xs, mods, norm_ffn2[l], f2_in[l], f2_out[l], mod0=6, mix=(ya, yg, yr, wo16[l]),
                  final_g=final_norm if last else None, latent_only=last, **geo)
    return xs.reshape(bsz, length, d)
```

```python
import functools

import jax
import jax.numpy as jnp
from jax import lax
from jax.experimental import pallas as pl
from jax.experimental.pallas import tpu as pltpu

F32 = jnp.float32
BF16 = jnp.bfloat16

GRID_W = 64
CTX_LEN = 256
N_MODS = 9
EPS = 1e-6
HEAD_DIM = 64
ATT_KV_HEADS = 2
ATT_GROUP = 4
WINDOW = 128
ATT_BLOCK = 128
ROPE_BASE = 10000.0
HG_HEADS = 4
HG_DK = 64
HG_CHUNK = 16
RG_BLOCKS = 4
RG_CONV = 4
RG_C = 8.0

TILE = 256
SUBLANES = 8
HALO = SUBLANES
VMEM_LIMIT = 52 * 1024 * 1024


def _cparams(sem):
    return pltpu.CompilerParams(dimension_semantics=sem, vmem_limit_bytes=VMEM_LIMIT)


def _const_spec(shape):
    nd = len(shape)
    return pl.BlockSpec(shape, lambda *_: (0,) * nd, pipeline_mode=pl.Buffered(1))


def _layer_spec(arr, l):
    nd = arr.ndim
    return pl.BlockSpec((None,) + arr.shape[1:], lambda *_: (l,) + (0,) * (nd - 1), pipeline_mode=pl.Buffered(1))


def _rms(xf, g):
    return xf * lax.rsqrt(jnp.mean(xf * xf, axis=-1, keepdims=True) + EPS) * g


def _dot(a, b):
    return jnp.dot(a, b, preferred_element_type=F32)


def _dot_nt(a, b):
    return lax.dot_general(a, b, (((1,), (1,)), ((), ())), preferred_element_type=F32)


def _dot_tn(a, b):
    return lax.dot_general(a, b, (((0,), (0,)), ((), ())), preferred_element_type=F32)


def _adaln_kernel(c_ref, w_ref, b_ref, o_ref):
    s = jax.nn.silu(c_ref[...]).astype(BF16)
    o_ref[0] = _dot(s, w_ref[0].astype(BF16)) + b_ref[0]


def _adaln(cond, ada_w, ada_b):
    depth, d, n = ada_w.shape
    tn = n // 8
    return pl.pallas_call(
        _adaln_kernel,
        grid=(depth, n // tn),
        in_specs=[
            pl.BlockSpec(cond.shape, lambda l, j: (0, 0)),
            pl.BlockSpec((1, d, tn), lambda l, j: (l, 0, j)),
            pl.BlockSpec((1, 1, tn), lambda l, j: (l, 0, j)),
        ],
        out_specs=pl.BlockSpec((1, cond.shape[0], tn), lambda l, j: (l, 0, j)),
        out_shape=jax.ShapeDtypeStruct((depth, cond.shape[0], n), F32),
        compiler_params=_cparams(("arbitrary", "arbitrary")),
        name="adaln",
    )(cond, ada_w, ada_b.reshape(depth, 1, n))


def _rope(t, cos, sa, sb):
    w = t.shape[1]
    rep = w // cos.shape[1]
    c, a, b = (jnp.concatenate([m] * rep, axis=1) if rep > 1 else m for m in (cos, sa, sb))
    half = HEAD_DIM // 4
    return t * c + pltpu.roll(t, w - half, 1) * a + pltpu.roll(t, half, 1) * b


def _token_kernel(*refs, mod0, n_ctx_tiles, two_src, has_mix, has_proj, has_final):
    it = iter(refs)
    if two_src:
        c_ref = next(it)
    x_ref, mods_ref, g_ref, win_ref, wout_ref = (next(it) for _ in range(5))
    if has_mix:
        ya_ref, yg_ref, yr_ref, wmix_ref = (next(it) for _ in range(4))
    if has_proj:
        gp_ref, wp_ref, cos_ref, sa_ref, sb_ref = (next(it) for _ in range(5))
    if has_final:
        fin_ref = next(it)
    o_ref = next(it)
    if has_proj:
        q_ref, k_ref, v_ref, ph_ref, pr_ref = (next(it) for _ in range(5))

    mods = mods_ref[0]
    x = x_ref[...]
    if two_src:
        x = jnp.where(pl.program_id(0) < n_ctx_tiles, c_ref[...], x)
    if has_mix:
        y = jnp.concatenate([ya_ref[...], yg_ref[...], yr_ref[...]], axis=-1)
        x = x + mods[5:6] * _dot(y, wmix_ref[...])
    shift, scale, gate = mods[mod0:mod0 + 1], mods[mod0 + 1:mod0 + 2], mods[mod0 + 2:mod0 + 3]
    h = (_rms(x, g_ref[...]) * (1.0 + scale) + shift).astype(BF16)
    f = wout_ref.shape[0]
    a = _dot(h, win_ref[:, :f])
    b = _dot(h, win_ref[:, f:])
    u = (jax.nn.silu(a) * b).astype(BF16)
    out = x + (0.5 * gate) * _dot(u, wout_ref[...])
    o_ref[...] = _rms(out, fin_ref[...]) if has_final else out
    if has_proj:
        h = (_rms(out, gp_ref[...]) * (1.0 + mods[4:5]) + mods[3:4]).astype(BF16)
        cos, sa, sb = cos_ref[...], sa_ref[...], sb_ref[...]
        o = 0
        for ref, rotate, scl in ((q_ref, True, HEAD_DIM ** -0.5), (k_ref, True, None), (v_ref, False, None),
                                 (ph_ref, False, None), (pr_ref, False, None)):
            p = _dot(h, wp_ref[:, o:o + ref.shape[1]])
            o += ref.shape[1]
            if rotate:
                p = _rope(p, cos, sa, sb)
            if scl is not None:
                p = p * scl
            ref[...] = p.astype(ref.dtype)


def _token_block(src, mods_all, g_all, w_in_all, w_out_all, *, layer, mod0, n_ctx_tiles, tiles_per_batch,
                 mix=None, proj=None, final_g=None, latent_only=False):
    two_src = isinstance(src, tuple)
    if two_src:
        ctx2, x2 = src
        n, d = ctx2.shape[0] + x2.shape[0], x2.shape[1]
    else:
        n, d = src.shape
    skip = n_ctx_tiles if latent_only else 0
    nt = n // TILE - skip
    ctx_row = mods_all.shape[1] - 1
    row = lambda i: (i + skip, 0)

    def mod_row(i):
        j = i + skip
        return (layer, jnp.where(j < n_ctx_tiles, ctx_row, (j - n_ctx_tiles) // tiles_per_batch), 0, 0)

    if two_src:
        in_specs = [pl.BlockSpec((TILE, d), lambda i: (jnp.minimum(i, n_ctx_tiles - 1), 0)),
                    pl.BlockSpec((TILE, d), lambda i: (jnp.maximum(i - n_ctx_tiles, 0), 0))]
        args = [ctx2, x2]
    else:
        in_specs, args = [pl.BlockSpec((TILE, d), row)], [src]
    in_specs += [pl.BlockSpec((None, 1, N_MODS, d), mod_row), _layer_spec(g_all, layer),
                 _layer_spec(w_in_all, layer), _layer_spec(w_out_all, layer)]
    args += [mods_all, g_all, w_in_all, w_out_all]
    if mix is not None:
        ya, yg, yr, w_mix_all = mix
        ya_row = (lambda i: (i, 0)) if ya.shape[0] == nt * TILE else row
        in_specs += [pl.BlockSpec((TILE, ya.shape[1]), ya_row), pl.BlockSpec((TILE, yg.shape[1]), row),
                     pl.BlockSpec((TILE, yr.shape[1]), row), _layer_spec(w_mix_all, layer)]
        args += [ya, yg, yr, w_mix_all]
    out_specs = [pl.BlockSpec((TILE, d), lambda i: (i, 0))]
    out_shape = [jax.ShapeDtypeStruct((nt * TILE, d), F32)]
    if proj is not None:
        gp_all, wp_all, tabs, widths = proj

        def tab_row(i):
            return (jnp.where(i < n_ctx_tiles, 0, 1 + (i - n_ctx_tiles) % tiles_per_batch), 0)

        in_specs += [_layer_spec(gp_all, layer), _layer_spec(wp_all, layer)]
        in_specs += [pl.BlockSpec((TILE, tabs[0].shape[1]), tab_row)] * 3
        args += [gp_all, wp_all, *tabs]
        nq, nk, nh, nr = widths
        out_specs += [pl.BlockSpec((TILE, wd), lambda i: (i, 0)) for wd in (nq, nk, nk, nh, nr)]
        out_shape += [jax.ShapeDtypeStruct((n, wd), dt) for wd, dt in
                      ((nq, BF16), (nk, BF16), (nk, BF16), (nh, F32), (nr, F32))]
    if final_g is not None:
        in_specs.append(_const_spec((1, d)))
        args.append(final_g.reshape(1, d))
    return pl.pallas_call(
        functools.partial(_token_kernel, mod0=mod0, n_ctx_tiles=n_ctx_tiles, two_src=two_src,
                          has_mix=mix is not None, has_proj=proj is not None, has_final=final_g is not None),
        grid=(nt,),
        in_specs=in_specs,
        out_specs=out_specs,
        out_shape=out_shape,
        compiler_params=_cparams(("arbitrary",)),
        name="ffn_out" if mix is not None else "ffn_in",
    )(*args)


def _attn_kernel(sink_ref, q_ref, kc_ref, vc_ref, kp_ref, k0_ref, kn_ref, vp_ref, v0_ref, vn_ref, o_ref,
                 *, layer, t0, n_ctx_blocks, length):
    t = pl.program_id(1) + t0
    n = t - n_ctx_blocks
    k_loc = jnp.concatenate([kp_ref[...], k0_ref[...], kn_ref[...]], axis=0)
    v_loc = jnp.concatenate([vp_ref[...], v0_ref[...], vn_ref[...]], axis=0)
    k_ctx, v_ctx = kc_ref[...], vc_ref[...]
    sw = k_loc.shape[1]
    nslab = q_ref.shape[1] // sw
    q = jnp.concatenate([q_ref[:, g * sw:(g + 1) * sw] for g in range(nslab)], axis=0)
    k_pos = (n - 1) * ATT_BLOCK + lax.broadcasted_iota(jnp.int32, (3 * ATT_BLOCK, ATT_BLOCK), 0)
    q_pos = n * ATT_BLOCK + lax.broadcasted_iota(jnp.int32, (3 * ATT_BLOCK, ATT_BLOCK), 1)
    valid = (jnp.abs(k_pos - q_pos) <= WINDOW) & (k_pos >= 0) & (k_pos < length) & (n >= 0)
    valid = jnp.concatenate([valid] * nslab, axis=1)
    acc = jnp.zeros((sw, q.shape[0]), F32)
    for kv in range(ATT_KV_HEADS):
        def only(t):
            lane = lax.broadcasted_iota(jnp.int32, t.shape, 1)
            return jnp.where((lane >= kv * HEAD_DIM) & (lane < (kv + 1) * HEAD_DIM), t, jnp.zeros_like(t))
        sink = jnp.concatenate([jnp.full((1, ATT_BLOCK), sink_ref[layer, kv * ATT_GROUP + g], F32)
                                for g in range(nslab)], axis=1)
        s_loc = jnp.where(valid, _dot_nt(only(k_loc), q), -jnp.inf)
        s_ctx = _dot_nt(only(k_ctx), q)
        m = jnp.maximum(jnp.maximum(s_loc.max(0, keepdims=True), s_ctx.max(0, keepdims=True)), sink)
        p_loc = jnp.exp(s_loc - m)
        p_ctx = jnp.exp(s_ctx - m)
        denom = p_loc.sum(0, keepdims=True) + p_ctx.sum(0, keepdims=True) + jnp.exp(sink - m)
        o = _dot_tn(only(v_loc), p_loc.astype(BF16)) + _dot_tn(only(v_ctx), p_ctx.astype(BF16))
        acc = acc + o / denom
    out = acc.T
    for g in range(nslab):
        o_ref[:, g * sw:(g + 1) * sw] = out[g * ATT_BLOCK:(g + 1) * ATT_BLOCK].astype(BF16)


def _attention(q, k, v, sink, *, layer, bsz, length, with_ctx_out):
    n, nq = q.shape
    nk = k.shape[1]
    cb = CTX_LEN // ATT_BLOCK
    lb = length // ATT_BLOCK
    t0 = 0 if with_ctx_out else cb

    def q_row(b, i):
        t = i + t0
        return (jnp.where(t < cb, cb * b + t, cb * bsz + lb * b + (t - cb)), 0)

    def loc_row(off):
        def f(b, i):
            nn = jnp.clip(i + t0 - cb + off, 0, lb - 1)
            return (cb * bsz + lb * b + nn, 0)
        return f

    def o_row(b, i):
        r, c = q_row(b, i)
        return (r if with_ctx_out else r - cb * bsz, c)

    ctx_row = lambda b, i: (b, 0)
    kspec = [pl.BlockSpec((ATT_BLOCK, nk), loc_row(o)) for o in (-1, 0, 1)]
    return pl.pallas_call(
        functools.partial(_attn_kernel, layer=layer, t0=t0, n_ctx_blocks=cb, length=length),
        grid=(bsz, cb + lb - t0),
        in_specs=[pl.BlockSpec(memory_space=pltpu.SMEM), pl.BlockSpec((ATT_BLOCK, nq), q_row),
                  pl.BlockSpec((CTX_LEN, nk), ctx_row), pl.BlockSpec((CTX_LEN, nk), ctx_row)] + kspec + kspec,
        out_specs=pl.BlockSpec((ATT_BLOCK, nq), o_row),
        out_shape=jax.ShapeDtypeStruct((n - t0 * bsz * ATT_BLOCK, nq), BF16),
        compiler_params=_cparams(("arbitrary", "arbitrary")),
        name="attention",
    )(sink, q, k, v, k, k, k, v, v, v)


def _seq_tile(d, i, tiles_per_batch):
    return jnp.where(d == 0, i, jnp.where(i == 0, 0, tiles_per_batch + 1 - i))


def _seq_row(b, tile, bsz, tiles_per_batch):
    return jnp.where(tile == 0, b, bsz + tiles_per_batch * b + tile - 1)


def _shift_rows(x, s, rev):
    if s == 0:
        return x
    return pltpu.roll(x, (x.shape[0] - s) if rev else s, 0)


def _hgrn_kernel(lbl_ref, hq_ref, fz_ref, hi_ref, hg_ref, ng_ref, jmat_ref, o_ref,
                 st_ref, ofwd_ref, upd_ref, sin_ref, *, layer, tiles_per_batch):
    d = pl.program_id(1)
    i = pl.program_id(2)
    tile = _seq_tile(d, i, tiles_per_batch)
    tt, w = hq_ref.shape
    c = HG_CHUNK

    @pl.when(i == 0)
    def _():
        st_ref[...] = jnp.zeros_like(st_ref)

    logits = lbl_ref[:, 0, 0, :]
    e = jnp.exp(logits - logits.max(0, keepdims=True))
    lbp = e / e.sum(0, keepdims=True)
    lb = jnp.zeros((1, w), F32)
    for j in range(1, layer + 1):
        lb = lb + lbp[j:j + 1]

    q = jax.nn.silu(hq_ref[...])
    f = lb + (1.0 - lb) * jax.nn.sigmoid(fz_ref[...])
    kk = 1.0 - f
    v = hi_ref[...]
    jmat = jmat_ref[...]
    pos = lax.broadcasted_iota(jnp.int32, (tt, w), 0) % c

    def run(rev):
        def in_chunk(s, r):
            return (pos <= c - 1 - s) if r else (pos >= s)

        def cumprod(y, r):
            s = 1
            while s < c:
                y = y * jnp.where(in_chunk(s, r), _shift_rows(y, s, r), 1.0)
                s *= 2
            return y

        p_in = cumprod(f, rev)
        p_out = cumprod(jnp.where(in_chunk(1, not rev), _shift_rows(f, 1, not rev), 1.0), not rev)
        half = c // 2
        nch = tt // c

        def halves(x):
            x4 = x.reshape(nch, 2, half, w)
            return (x4[:, 1], x4[:, 0]) if rev else (x4[:, 0], x4[:, 1])

        r8 = lax.broadcasted_iota(jnp.int32, (nch, half, w), 1)

        def rot(x, s):
            return x if s == 0 else pltpu.roll(x, (half - s) if rev else s, 1)

        def wrapped(s):
            return (r8 > half - 1 - s) if rev else (r8 < s)

        def pair_sum(wgt, vs):
            return (_dot(wgt.reshape(nch * half, w).astype(BF16), jmat) * vs.reshape(nch * half, w))

        (f_a, f_b), (v_a, v_b), (qd_a, qd_b) = halves(f), halves(v), halves(q)
        o_a = jnp.zeros((nch * half, w), F32)
        o_b = jnp.zeros((nch * half, w), F32)
        for delta in range(c):
            s = delta % half
            fr, vr = rot(f_a, s), rot(v_a, s)
            if delta < half:
                fs_a = jnp.where(wrapped(s), 1.0, fr)
                nxt_a = qd_a * fs_a
                o_a = o_a + pair_sum(qd_a - nxt_a, vr)
                qd_a = nxt_a
                fs_b = jnp.where(wrapped(s), fr, rot(f_b, s))
                vs_b = jnp.where(wrapped(s), vr, rot(v_b, s))
            else:
                fs_b = jnp.where(wrapped(s), 1.0, fr)
                vs_b = vr
            nxt_b = qd_b * fs_b
            o_b = o_b + pair_sum(qd_b - nxt_b, vs_b)
            qd_b = nxt_b
        o_a, o_b = o_a.reshape(nch, half, w), o_b.reshape(nch, half, w)
        o = jnp.stack([o_b, o_a] if rev else [o_a, o_b], axis=1).reshape(tt, w)
        p3 = p_in.reshape(nch, c, w)
        dec_end = p3[:, 0:1, :] if rev else p3[:, c - 1:c, :]
        q_in = (q * p_in).astype(BF16)
        lane3 = lax.broadcasted_iota(jnp.int32, (nch, c, w), 2) % (2 * HG_DK)

        def per_head(x):
            x3 = x.astype(BF16).reshape(nch, c, w)
            zero = jnp.zeros_like(x3)
            return jnp.concatenate([jnp.where(lane3 < HG_DK, x3, zero), jnp.where(lane3 >= HG_DK, x3, zero)], axis=1)

        v_hat, k_hat = per_head(v), per_head(kk * p_out)
        ntile = st_ref.shape[0]
        lt = w // ntile
        for ci in range(nch):
            for j in range(ntile):
                upd_ref[ci, j] = _dot_tn(v_hat[ci, :, j * lt:(j + 1) * lt], k_hat[ci, :, j * lt:(j + 1) * lt])
        st = [st_ref[j] for j in range(ntile)]
        for ci in (range(nch - 1, -1, -1) if rev else range(nch)):
            for j in range(ntile):
                sin_ref[ci, j] = st[j].astype(BF16)
                st[j] = st[j] * dec_end[ci][:, j * lt:(j + 1) * lt] + upd_ref[ci, j]
        for j in range(ntile):
            st_ref[j] = st[j]
        parts = [jnp.concatenate([_dot_nt(q_in[ci * c:(ci + 1) * c, j * lt:(j + 1) * lt], sin_ref[ci, j])
                                  for j in range(ntile)], axis=1) for ci in range(nch)]
        return o + jnp.concatenate(parts, axis=0)

    @pl.when(d == 0)
    def _():
        ofwd_ref[tile] = run(False)

    @pl.when(d == 1)
    def _():
        tot = run(True) + ofwd_ref[tile]
        sq = tot * tot
        hi = sq.astype(BF16)
        lo = (sq - hi.astype(F32)).astype(BF16)
        ms = (_dot(hi, jmat) + _dot(lo, jmat)) * (1.0 / HG_DK)
        y = tot * lax.rsqrt(ms + EPS)
        o_ref[...] = (y * ng_ref[...] * jax.nn.silu(hg_ref[...])).astype(BF16)


def _hgrn(ph, lb_logits, norm_g_all, *, layer, bsz, tiles_per_batch):
    n = ph.shape[0]
    depth = lb_logits.shape[0]
    w = norm_g_all.shape[-1]
    head = jnp.arange(w) // HG_DK
    same = head[:, None] == head[None, :]
    lt = 2 * HG_DK
    ntile = w // lt

    def col(cfn):
        def f(b, d, i):
            return (_seq_row(b, _seq_tile(d, i, tiles_per_batch), bsz, tiles_per_batch), cfn(d))
        return f

    def out_row(b, d, i):
        tile = jnp.where(d == 0, 0, _seq_tile(d, i, tiles_per_batch))
        return (_seq_row(b, tile, bsz, tiles_per_batch), 0)

    return pl.pallas_call(
        functools.partial(_hgrn_kernel, layer=layer, tiles_per_batch=tiles_per_batch),
        grid=(bsz, 2, tiles_per_batch + 1),
        in_specs=[
            pl.BlockSpec((depth, 1, 1, w), lambda b, d, i: (0, d, 0, 0)),
            pl.BlockSpec((TILE, w), col(lambda d: 0)),
            pl.BlockSpec((TILE, w), col(lambda d: 1 + d)),
            pl.BlockSpec((TILE, w), col(lambda d: 3)),
            pl.BlockSpec((TILE, w), col(lambda d: 4)),
            _layer_spec(norm_g_all, layer), _const_spec((w, w)),
        ],
        out_specs=pl.BlockSpec((TILE, w), out_row),
        out_shape=jax.ShapeDtypeStruct((n, w), BF16),
        scratch_shapes=[pltpu.VMEM((ntile, lt, lt), F32), pltpu.VMEM((tiles_per_batch + 1, TILE, w), F32),
                        pltpu.VMEM((TILE // HG_CHUNK, ntile, lt, lt), F32),
                        pltpu.VMEM((TILE // HG_CHUNK, ntile, lt, lt), BF16)],
        compiler_params=_cparams(("arbitrary", "arbitrary", "arbitrary")),
        name="hgrn2",
    )(lb_logits.reshape(depth, 2, 1, w), ph, ph, ph, ph, norm_g_all, same.astype(BF16))


def _rglru_kernel(x_ref, gate_ref, prev_ref, next_ref, cw_ref, cb_ref, wa_ref, ba_ref, wx_ref, bx_ref, lam_ref,
                  o_ref, h_ref, hfwd_ref, *, tiles_per_batch):
    d = pl.program_id(1)
    i = pl.program_id(2)
    tile = _seq_tile(d, i, tiles_per_batch)
    tt, w = x_ref.shape

    @pl.when(i == 0)
    def _():
        h_ref[...] = jnp.zeros_like(h_ref)

    x = x_ref[...]
    has_prev = (tile >= 2).astype(F32)
    has_next = jnp.logical_and(tile >= 1, tile < tiles_per_batch).astype(F32)
    prev = prev_ref[...] * has_prev
    nxt = next_ref[...] * has_next
    row = lax.broadcasted_iota(jnp.int32, (tt, w), 0)
    xm1 = jnp.where(row == 0, prev[HALO - 1:HALO], _shift_rows(x, 1, False))
    xm2 = jnp.where(row == 0, prev[HALO - 2:HALO - 1],
                    jnp.where(row == 1, prev[HALO - 1:HALO], _shift_rows(x, 2, False)))
    xp1 = jnp.where(row == tt - 1, nxt[0:1], _shift_rows(x, 1, True))
    cw = cw_ref[...]
    u = cb_ref[...] + xm2 * cw[0:1] + xm1 * cw[1:2] + x * cw[2:3] + xp1 * cw[3:4]

    u16 = u.astype(BF16)
    r = jax.nn.sigmoid(_dot(u16, wa_ref[0]) + ba_ref[0])
    ig = jax.nn.sigmoid(_dot(u16, wx_ref[0]) + bx_ref[0])
    log_a = -RG_C * r * jax.nn.softplus(-lam_ref[0])
    a = jnp.exp(log_a)
    z = jnp.sqrt(jnp.tanh(-log_a) * (1.0 + a * a)) * (ig * u)

    def scan(rev):
        ng = tt // SUBLANES
        aa, zz = a.reshape(ng, SUBLANES, w), z.reshape(ng, SUBLANES, w)
        r8 = lax.broadcasted_iota(jnp.int32, (ng, SUBLANES, w), 1)
        s = 1
        while s < SUBLANES:
            ok = (r8 <= SUBLANES - 1 - s) if rev else (r8 >= s)
            sh = (SUBLANES - s) if rev else s
            zz = zz + jnp.where(ok, aa * pltpu.roll(zz, sh, 1), 0.0)
            aa = aa * jnp.where(ok, pltpu.roll(aa, sh, 1), 1.0)
            s *= 2
        carry = h_ref[0:1, :]
        groups = [None] * ng
        for gi in (range(ng - 1, -1, -1) if rev else range(ng)):
            groups[gi] = aa[gi] * carry + zz[gi]
            carry = groups[gi][0:1] if rev else groups[gi][SUBLANES - 1:SUBLANES]
        h_ref[...] = jnp.broadcast_to(carry, h_ref.shape)
        return jnp.concatenate(groups, axis=0)

    @pl.when(d == 0)
    def _():
        hfwd_ref[tile] = scan(False)

    @pl.when(d == 1)
    def _():
        h = scan(True) + hfwd_ref[tile]
        o_ref[...] = (jax.nn.gelu(gate_ref[...]) * h).astype(BF16)


def _rglru(pr, conv_w, conv_b, wa, ba, wx, bx, lam, *, layer, bsz, tiles_per_batch):
    n = pr.shape[0]
    w = conv_b.shape[-1]
    hpt = TILE // HALO

    def seq_row(b, d, i):
        return _seq_row(b, _seq_tile(d, i, tiles_per_batch), bsz, tiles_per_batch)

    def out_row(b, d, i):
        tile = jnp.where(d == 0, 0, _seq_tile(d, i, tiles_per_batch))
        return (_seq_row(b, tile, bsz, tiles_per_batch), 0)

    per_dir = lambda b, d, i: (layer, d, 0, 0)
    return pl.pallas_call(
        functools.partial(_rglru_kernel, tiles_per_batch=tiles_per_batch),
        grid=(bsz, 2, tiles_per_batch + 1),
        in_specs=[
            pl.BlockSpec((TILE, w), lambda b, d, i: (seq_row(b, d, i), 0)),
            pl.BlockSpec((TILE, w), lambda b, d, i: (seq_row(b, d, i), 1)),
            pl.BlockSpec((HALO, w), lambda b, d, i: (jnp.maximum(seq_row(b, d, i) * hpt - 1, 0), 0)),
            pl.BlockSpec((HALO, w), lambda b, d, i: (jnp.minimum((seq_row(b, d, i) + 1) * hpt, n // HALO - 1), 0)),
            _layer_spec(conv_w, layer), _layer_spec(conv_b, layer),
            pl.BlockSpec((None, 1, w, w), per_dir), pl.BlockSpec((None, 1, 1, w), per_dir),
            pl.BlockSpec((None, 1, w, w), per_dir), pl.BlockSpec((None, 1, 1, w), per_dir),
            pl.BlockSpec((None, 1, 1, w), per_dir),
        ],
        out_specs=pl.BlockSpec((TILE, w), out_row),
        out_shape=jax.ShapeDtypeStruct((n, w), BF16),
        scratch_shapes=[pltpu.VMEM((SUBLANES, w), F32), pltpu.VMEM((tiles_per_batch + 1, TILE, w), F32)],
        compiler_params=_cparams(("arbitrary", "arbitrary", "arbitrary")),
        name="rglru",
    )(pr, pr, pr, pr, conv_w, conv_b, wa, ba, wx, bx, lam)


def _rope_tables(length):
    n_freq = HEAD_DIM // 4
    rows = length // GRID_W
    row = jnp.repeat(jnp.arange(rows), GRID_W)
    col = jnp.tile(jnp.arange(GRID_W), rows)
    inv_freq = ROPE_BASE ** (-jnp.arange(n_freq, dtype=F32) / n_freq)
    ang = jnp.stack([row[:, None] * inv_freq, col[:, None] * inv_freq], axis=1)
    cos, sin = jnp.cos(ang), jnp.sin(ang)
    zero = jnp.zeros_like(sin[:, 0])
    cos_h = jnp.concatenate([cos[:, 0], cos[:, 0], cos[:, 1], cos[:, 1]], axis=-1)
    sa_h = jnp.concatenate([-sin[:, 0], zero, -sin[:, 1], zero], axis=-1)
    sb_h = jnp.concatenate([zero, sin[:, 0], zero, sin[:, 1]], axis=-1)
    ctx = (jnp.ones((CTX_LEN, HEAD_DIM), F32), jnp.zeros((CTX_LEN, HEAD_DIM), F32), jnp.zeros((CTX_LEN, HEAD_DIM), F32))
    return tuple(jnp.tile(jnp.concatenate([c, t], axis=0), (1, 2)) for c, t in zip(ctx, (cos_h, sa_h, sb_h)))


def _block_diag(wb):
    nb, bw = wb.shape[-3], wb.shape[-1]
    out = jnp.zeros(wb.shape[:-3] + (nb * bw, nb * bw), wb.dtype)
    for j in range(nb):
        out = out.at[..., j * bw:(j + 1) * bw, j * bw:(j + 1) * bw].set(wb[..., j, :, :])
    return out


def _regroup_heads(t, axis):
    shp = t.shape
    t = t.reshape(shp[:axis] + (ATT_KV_HEADS, ATT_GROUP, HEAD_DIM) + shp[axis + 1:])
    return jnp.swapaxes(t, axis, axis + 1).reshape(shp)


def kernel(x, c, ctx, c_ctx, ada_w, ada_b, norm_ffn1, ffn1_w_in, ffn1_w_out, norm_mix, w_in, w_out, attn_sink, hg_lb_logits, hg_norm, rg_conv_w, rg_conv_b, rg_gate_a_w, rg_gate_a_b, rg_gate_x_w, rg_gate_x_b, rg_lambda, norm_ffn2, ffn2_w_in, ffn2_w_out, final_norm):
    bsz, length, d = x.shape
    depth = ada_w.shape[0]
    assert ctx.shape[1] == CTX_LEN == TILE and length % TILE == 0
    tpb = length // TILE
    geo = dict(n_ctx_tiles=bsz, tiles_per_batch=tpb)

    ctx2, x2 = ctx.reshape(bsz * CTX_LEN, d), x.reshape(bsz * length, d)
    cond = jnp.zeros((SUBLANES, d), F32).at[:bsz].set(c).at[bsz].set(c_ctx)
    mods_all = _adaln(cond, ada_w, ada_b).reshape(depth, SUBLANES, N_MODS, d)[:, :bsz + 1]
    tabs = _rope_tables(length)

    att_w = w_out.shape[1] // 2
    kv_w = ATT_KV_HEADS * HEAD_DIM
    hg_w = hg_norm.shape[1]
    rg_w = rg_conv_b.shape[1]
    widths = (att_w, kv_w, 5 * hg_w, 2 * rg_w)
    assert sum(widths) + kv_w == w_in.shape[2]

    to16 = lambda t: t.astype(BF16)
    f1_in, f1_out, f2_in, f2_out = map(to16, (ffn1_w_in, ffn1_w_out, ffn2_w_in, ffn2_w_out))
    wi16 = to16(jnp.concatenate([_regroup_heads(w_in[:, :, :att_w], 2), w_in[:, :, att_w:]], axis=2))
    wo16 = to16(jnp.concatenate([_regroup_heads(w_out[:, :att_w], 1), w_out[:, att_w:]], axis=1))
    wa_bd, wx_bd = to16(_block_diag(rg_gate_a_w)), to16(_block_diag(rg_gate_x_w))
    as_row = lambda t: t.reshape(t.shape[:-1] + (1, t.shape[-1]))
    g1, gm, g2, hgn = map(as_row, (norm_ffn1, norm_mix, norm_ffn2, hg_norm))
    conv_b, ba, bx, lam = map(as_row, (rg_conv_b, rg_gate_a_b, rg_gate_x_b, rg_lambda))

    xs = (ctx2, x2)
    for l in range(depth):
        last = l == depth - 1
        xs, q, k, v, ph, pr = _token_block(xs, mods_all, g1, f1_in, f1_out, layer=l, mod0=0,
                                           proj=(gm, wi16, tabs, widths), **geo)
        ya = _attention(q, k, v, attn_sink, layer=l, bsz=bsz, length=length, with_ctx_out=not last)
        yg = _hgrn(ph, hg_lb_logits, hgn, layer=l, bsz=bsz, tiles_per_batch=tpb)
        yr = _rglru(pr, rg_conv_w, conv_b, wa_bd, ba, wx_bd, bx, lam, layer=l, bsz=bsz, tiles_per_batch=tpb)
        xs, = _token_block(xs, mods_all, g2, f2_in, f2_out, layer=l, mod0=6, mix=(ya, yg, yr, wo16),
                           final_g=final_norm if last else None, latent_only=last, **geo)
    return xs.reshape(bsz, length, d)
```

```python
import functools

import jax
import jax.numpy as jnp
from jax import lax
from jax.experimental import pallas as pl
from jax.experimental.pallas import tpu as pltpu

F32 = jnp.float32
BF16 = jnp.bfloat16

GRID_W = 64
CTX_LEN = 256
N_MODS = 9
EPS = 1e-6
HEAD_DIM = 64
ATT_KV_HEADS = 2
ATT_GROUP = 4
WINDOW = 128
ATT_BLOCK = 128
ATT_QBLOCKS = 2
ROPE_BASE = 10000.0
HG_HEADS = 4
HG_DK = 64
HG_CHUNK = 16
RG_BLOCKS = 4
RG_CONV = 4
RG_C = 8.0

TILE = 256
TOKEN_TILE = 512
SUBLANES = 8
HALO = SUBLANES
VMEM_LIMIT = 52 * 1024 * 1024


def _cparams(sem):
    return pltpu.CompilerParams(dimension_semantics=sem, vmem_limit_bytes=VMEM_LIMIT)


def _const_spec(shape):
    nd = len(shape)
    return pl.BlockSpec(shape, lambda *_: (0,) * nd, pipeline_mode=pl.Buffered(1))


def _layer_spec(arr, l):
    nd = arr.ndim
    return pl.BlockSpec((None,) + arr.shape[1:], lambda *_: (l,) + (0,) * (nd - 1), pipeline_mode=pl.Buffered(1))


def _rms(xf, g):
    return xf * lax.rsqrt(jnp.mean(xf * xf, axis=-1, keepdims=True) + EPS) * g


def _dot(a, b):
    return jnp.dot(a, b, preferred_element_type=F32)


def _dot_nt(a, b):
    return lax.dot_general(a, b, (((1,), (1,)), ((), ())), preferred_element_type=F32)


def _dot_tn(a, b):
    return lax.dot_general(a, b, (((0,), (0,)), ((), ())), preferred_element_type=F32)


def _adaln_kernel(c_ref, w_ref, b_ref, o_ref):
    s = jax.nn.silu(c_ref[...]).astype(BF16)
    o_ref[0] = _dot(s, w_ref[0].astype(BF16)) + b_ref[0]


def _adaln(cond, ada_w, ada_b):
    depth, d, n = ada_w.shape
    tn = n // 8
    return pl.pallas_call(
        _adaln_kernel,
        grid=(depth, n // tn),
        in_specs=[
            pl.BlockSpec(cond.shape, lambda l, j: (0, 0)),
            pl.BlockSpec((1, d, tn), lambda l, j: (l, 0, j)),
            pl.BlockSpec((1, 1, tn), lambda l, j: (l, 0, j)),
        ],
        out_specs=pl.BlockSpec((1, cond.shape[0], tn), lambda l, j: (l, 0, j)),
        out_shape=jax.ShapeDtypeStruct((depth, cond.shape[0], n), F32),
        compiler_params=_cparams(("arbitrary", "arbitrary")),
        name="adaln",
    )(cond, ada_w, ada_b.reshape(depth, 1, n))


def _rope(t, cos, sa, sb):
    w = t.shape[1]
    rep = w // cos.shape[1]
    c, a, b = (jnp.concatenate([m] * rep, axis=1) if rep > 1 else m for m in (cos, sa, sb))
    half = HEAD_DIM // 4
    return t * c + pltpu.roll(t, w - half, 1) * a + pltpu.roll(t, half, 1) * b


def _token_kernel(*refs, mod0, n_ctx_tiles, two_src, has_mix, has_proj, has_final):
    it = iter(refs)
    if two_src:
        c_ref = next(it)
    x_ref, mods_ref, g_ref, win_ref, wout_ref = (next(it) for _ in range(5))
    if has_mix:
        ya_ref, yg_ref, yr_ref, wmix_ref = (next(it) for _ in range(4))
    if has_proj:
        gp_ref, wp_ref, cos_ref, sa_ref, sb_ref = (next(it) for _ in range(5))
    if has_final:
        fin_ref = next(it)
    o_ref = next(it)
    if has_proj:
        q_ref, k_ref, v_ref, ph_ref, pr_ref = (next(it) for _ in range(5))

    mods = mods_ref[0]
    shift, scale, gate = mods[mod0:mod0 + 1], mods[mod0 + 1:mod0 + 2], mods[mod0 + 2:mod0 + 3]
    f = wout_ref.shape[0]
    for r0 in range(0, x_ref.shape[0], TILE):
        rows = slice(r0, r0 + TILE)
        x = x_ref[rows, :]
        if two_src:
            x = jnp.where(pl.program_id(0) < n_ctx_tiles, c_ref[rows, :], x)
        if has_mix:
            y = jnp.concatenate([ya_ref[rows, :], yg_ref[rows, :], yr_ref[rows, :]], axis=-1)
            x = x + mods[5:6] * _dot(y, wmix_ref[...])
        h = (_rms(x, g_ref[...]) * (1.0 + scale) + shift).astype(BF16)
        a = _dot(h, win_ref[:, :f])
        b = _dot(h, win_ref[:, f:])
        u = (jax.nn.silu(a) * b).astype(BF16)
        out = x + (0.5 * gate) * _dot(u, wout_ref[...])
        o_ref[rows, :] = _rms(out, fin_ref[...]) if has_final else out
        if has_proj:
            h = (_rms(out, gp_ref[...]) * (1.0 + mods[4:5]) + mods[3:4]).astype(BF16)
            cos, sa, sb = cos_ref[rows, :], sa_ref[rows, :], sb_ref[rows, :]
            o = 0
            for ref, rotate, scl in ((q_ref, True, HEAD_DIM ** -0.5), (k_ref, True, None), (v_ref, False, None),
                                     (ph_ref, False, None), (pr_ref, False, None)):
                p = _dot(h, wp_ref[:, o:o + ref.shape[1]])
                o += ref.shape[1]
                if rotate:
                    p = _rope(p, cos, sa, sb)
                if scl is not None:
                    p = p * scl
                ref[rows, :] = p.astype(ref.dtype)


def _token_block(src, mods_all, g_all, w_in_all, w_out_all, *, layer, mod0, n_ctx_tiles, tiles_per_batch,
                 mix=None, proj=None, final_g=None, latent_only=False):
    two_src = isinstance(src, tuple)
    if two_src:
        ctx2, x2 = src
        n, d = ctx2.shape[0] + x2.shape[0], x2.shape[1]
    else:
        n, d = src.shape
    skip = n_ctx_tiles if latent_only else 0
    nt = n // TOKEN_TILE - skip
    ctx_row = mods_all.shape[1] - 1
    row = lambda i: (i + skip, 0)

    def mod_row(i):
        j = i + skip
        return (layer, jnp.where(j < n_ctx_tiles, ctx_row, (j - n_ctx_tiles) // tiles_per_batch), 0, 0)

    if two_src:
        in_specs = [pl.BlockSpec((TOKEN_TILE, d), lambda i: (jnp.minimum(i, n_ctx_tiles - 1), 0)),
                    pl.BlockSpec((TOKEN_TILE, d), lambda i: (jnp.maximum(i - n_ctx_tiles, 0), 0))]
        args = [ctx2, x2]
    else:
        in_specs, args = [pl.BlockSpec((TOKEN_TILE, d), row)], [src]
    in_specs += [pl.BlockSpec((None, 1, N_MODS, d), mod_row), _layer_spec(g_all, layer),
                 _layer_spec(w_in_all, layer), _layer_spec(w_out_all, layer)]
    args += [mods_all, g_all, w_in_all, w_out_all]
    if mix is not None:
        ya, yg, yr, w_mix_all = mix
        ya_row = (lambda i: (i, 0)) if ya.shape[0] == nt * TOKEN_TILE else row
        in_specs += [pl.BlockSpec((TOKEN_TILE, ya.shape[1]), ya_row), pl.BlockSpec((TOKEN_TILE, yg.shape[1]), row),
                     pl.BlockSpec((TOKEN_TILE, yr.shape[1]), row), _layer_spec(w_mix_all, layer)]
        args += [ya, yg, yr, w_mix_all]
    out_specs = [pl.BlockSpec((TOKEN_TILE, d), lambda i: (i, 0))]
    out_shape = [jax.ShapeDtypeStruct((nt * TOKEN_TILE, d), F32)]
    if proj is not None:
        gp_all, wp_all, tabs, widths = proj

        def tab_row(i):
            return (jnp.where(i < n_ctx_tiles, 0, 1 + (i - n_ctx_tiles) % tiles_per_batch), 0)

        in_specs += [_layer_spec(gp_all, layer), _layer_spec(wp_all, layer)]
        in_specs += [pl.BlockSpec((TOKEN_TILE, tabs[0].shape[1]), tab_row)] * 3
        args += [gp_all, wp_all, *tabs]
        nq, nk, nh, nr = widths
        out_specs += [pl.BlockSpec((TOKEN_TILE, wd), lambda i: (i, 0)) for wd in (nq, nk, nk, nh, nr)]
        out_shape += [jax.ShapeDtypeStruct((n, wd), dt) for wd, dt in
                      ((nq, BF16), (nk, BF16), (nk, BF16), (nh, F32), (nr, F32))]
    if final_g is not None:
        in_specs.append(_const_spec((1, d)))
        args.append(final_g.reshape(1, d))
    return pl.pallas_call(
        functools.partial(_token_kernel, mod0=mod0, n_ctx_tiles=n_ctx_tiles, two_src=two_src,
                          has_mix=mix is not None, has_proj=proj is not None, has_final=final_g is not None),
        grid=(nt,),
        in_specs=in_specs,
        out_specs=out_specs,
        out_shape=out_shape,
        compiler_params=_cparams(("arbitrary",)),
        name="ffn_out" if mix is not None else "ffn_in",
    )(*args)


def _attn_kernel(sink_ref, q_ref, kc_ref, vc_ref, *refs, layer, t0, n_ctx_blocks, length):
    nloc = ATT_QBLOCKS + 2
    k_refs, v_refs, o_ref = refs[:nloc], refs[nloc:2 * nloc], refs[2 * nloc]
    n0 = (pl.program_id(1) + t0) * ATT_QBLOCKS - n_ctx_blocks
    k_ctx, v_ctx = kc_ref[...], vc_ref[...]
    sw = k_ctx.shape[1]
    nslab = q_ref.shape[1] // sw

    def only(t, kv):
        lane = lax.broadcasted_iota(jnp.int32, t.shape, 1)
        return jnp.where((lane >= kv * HEAD_DIM) & (lane < (kv + 1) * HEAD_DIM), t, jnp.zeros_like(t))

    chains = [(j, kv) for j in range(ATT_QBLOCKS) for kv in range(ATT_KV_HEADS)]
    q, k_loc, v_loc, valid = [], [], [], []
    for j in range(ATT_QBLOCKS):
        rows = slice(j * ATT_BLOCK, (j + 1) * ATT_BLOCK)
        q.append(jnp.concatenate([q_ref[rows, g * sw:(g + 1) * sw] for g in range(nslab)], axis=0))
        k_loc.append(jnp.concatenate([r[...] for r in k_refs[j:j + 3]], axis=0))
        v_loc.append(jnp.concatenate([r[...] for r in v_refs[j:j + 3]], axis=0))
        n = n0 + j
        k_pos = (n - 1) * ATT_BLOCK + lax.broadcasted_iota(jnp.int32, (3 * ATT_BLOCK, ATT_BLOCK), 0)
        q_pos = n * ATT_BLOCK + lax.broadcasted_iota(jnp.int32, (3 * ATT_BLOCK, ATT_BLOCK), 1)
        ok = (jnp.abs(k_pos - q_pos) <= WINDOW) & (k_pos >= 0) & (k_pos < length) & (n >= 0)
        valid.append(jnp.concatenate([ok] * nslab, axis=1))
    sink = {kv: jnp.concatenate([jnp.full((1, ATT_BLOCK), sink_ref[layer, kv * ATT_GROUP + g], F32)
                                 for g in range(nslab)], axis=1) for kv in range(ATT_KV_HEADS)}
    s_loc = {c: jnp.where(valid[c[0]], _dot_nt(only(k_loc[c[0]], c[1]), q[c[0]]), -jnp.inf) for c in chains}
    s_ctx = {c: _dot_nt(only(k_ctx, c[1]), q[c[0]]) for c in chains}
    m = {c: jnp.maximum(jnp.maximum(s_loc[c].max(0, keepdims=True), s_ctx[c].max(0, keepdims=True)), sink[c[1]])
         for c in chains}
    p_loc = {c: jnp.exp(s_loc[c] - m[c]) for c in chains}
    p_ctx = {c: jnp.exp(s_ctx[c] - m[c]) for c in chains}
    denom = {c: p_loc[c].sum(0, keepdims=True) + p_ctx[c].sum(0, keepdims=True) + jnp.exp(sink[c[1]] - m[c])
             for c in chains}
    o = {c: _dot_tn(only(v_loc[c[0]], c[1]), p_loc[c].astype(BF16)) + _dot_tn(only(v_ctx, c[1]), p_ctx[c].astype(BF16))
         for c in chains}
    for j in range(ATT_QBLOCKS):
        out = sum(o[(j, kv)] / denom[(j, kv)] for kv in range(ATT_KV_HEADS)).T
        for g in range(nslab):
            o_ref[j * ATT_BLOCK:(j + 1) * ATT_BLOCK, g * sw:(g + 1) * sw] = (
                out[g * ATT_BLOCK:(g + 1) * ATT_BLOCK].astype(BF16))


def _attention(q, k, v, sink, *, layer, bsz, length, with_ctx_out):
    n, nq = q.shape
    nk = k.shape[1]
    qb = ATT_QBLOCKS * ATT_BLOCK
    cb = CTX_LEN // qb
    lb = length // qb
    assert CTX_LEN % qb == 0 and length % qb == 0
    t0 = 0 if with_ctx_out else cb
    nblk = length // ATT_BLOCK
    blk0 = bsz * CTX_LEN // ATT_BLOCK

    def q_row(b, i):
        t = i + t0
        return (jnp.where(t < cb, cb * b + t, cb * bsz + lb * b + (t - cb)), 0)

    def loc_row(off):
        def f(b, i):
            nn = jnp.clip((i + t0 - cb) * ATT_QBLOCKS + off, 0, nblk - 1)
            return (blk0 + nblk * b + nn, 0)
        return f

    def o_row(b, i):
        r, c = q_row(b, i)
        return (r if with_ctx_out else r - cb * bsz, c)

    ctx_row = lambda b, i: (b, 0)
    kspec = [pl.BlockSpec((ATT_BLOCK, nk), loc_row(o)) for o in range(-1, ATT_QBLOCKS + 1)]
    return pl.pallas_call(
        functools.partial(_attn_kernel, layer=layer, t0=t0, n_ctx_blocks=CTX_LEN // ATT_BLOCK, length=length),
        grid=(bsz, cb + lb - t0),
        in_specs=[pl.BlockSpec(memory_space=pltpu.SMEM), pl.BlockSpec((qb, nq), q_row),
                  pl.BlockSpec((CTX_LEN, nk), ctx_row), pl.BlockSpec((CTX_LEN, nk), ctx_row)] + kspec + kspec,
        out_specs=pl.BlockSpec((qb, nq), o_row),
        out_shape=jax.ShapeDtypeStruct((n - t0 * bsz * qb, nq), BF16),
        compiler_params=_cparams(("arbitrary", "arbitrary")),
        name="attention",
    )(sink, q, k, v, *([k] * len(kspec)), *([v] * len(kspec)))


def _seq_tile(d, i, tiles_per_batch):
    return jnp.where(d == 0, i, jnp.where(i == 0, 0, tiles_per_batch + 1 - i))


def _seq_row(b, tile, bsz, tiles_per_batch):
    return jnp.where(tile == 0, b, bsz + tiles_per_batch * b + tile - 1)


def _shift_rows(x, s, rev):
    if s == 0:
        return x
    return pltpu.roll(x, (x.shape[0] - s) if rev else s, 0)


def _hgrn_kernel(lbl_ref, hq_ref, fz_ref, hi_ref, hg_ref, ng_ref, jmat_ref, o_ref,
                 st_ref, ofwd_ref, upd_ref, sin_ref, *, layer, tiles_per_batch):
    d = pl.program_id(1)
    i = pl.program_id(2)
    tile = _seq_tile(d, i, tiles_per_batch)
    tt, w = hq_ref.shape
    c = HG_CHUNK

    @pl.when(i == 0)
    def _():
        st_ref[...] = jnp.zeros_like(st_ref)

    logits = lbl_ref[:, 0, 0, :]
    e = jnp.exp(logits - logits.max(0, keepdims=True))
    lbp = e / e.sum(0, keepdims=True)
    lb = jnp.zeros((1, w), F32)
    for j in range(1, layer + 1):
        lb = lb + lbp[j:j + 1]

    q = jax.nn.silu(hq_ref[...])
    f = lb + (1.0 - lb) * jax.nn.sigmoid(fz_ref[...])
    kk = 1.0 - f
    v = hi_ref[...]
    jmat = jmat_ref[...]
    pos = lax.broadcasted_iota(jnp.int32, (tt, w), 0) % c

    def run(rev):
        def in_chunk(s, r):
            return (pos <= c - 1 - s) if r else (pos >= s)

        def cumprod(y, r):
            s = 1
            while s < c:
                y = y * jnp.where(in_chunk(s, r), _shift_rows(y, s, r), 1.0)
                s *= 2
            return y

        p_in = cumprod(f, rev)
        p_out = cumprod(jnp.where(in_chunk(1, not rev), _shift_rows(f, 1, not rev), 1.0), not rev)
        half = c // 2
        nch = tt // c

        def halves(x):
            x4 = x.reshape(nch, 2, half, w)
            return (x4[:, 1], x4[:, 0]) if rev else (x4[:, 0], x4[:, 1])

        r8 = lax.broadcasted_iota(jnp.int32, (nch, half, w), 1)

        def rot(x, s):
            return x if s == 0 else pltpu.roll(x, (half - s) if rev else s, 1)

        def wrapped(s):
            return (r8 > half - 1 - s) if rev else (r8 < s)

        def pair_sum(wgt, vs):
            return (_dot(wgt.reshape(nch * half, w).astype(BF16), jmat) * vs.reshape(nch * half, w))

        (f_a, f_b), (v_a, v_b), (qd_a, qd_b) = halves(f), halves(v), halves(q)
        o_a = jnp.zeros((nch * half, w), F32)
        o_b = jnp.zeros((nch * half, w), F32)
        for delta in range(c):
            s = delta % half
            fr, vr = rot(f_a, s), rot(v_a, s)
            if delta < half:
                fs_a = jnp.where(wrapped(s), 1.0, fr)
                nxt_a = qd_a * fs_a
                o_a = o_a + pair_sum(qd_a - nxt_a, vr)
                qd_a = nxt_a
                fs_b = jnp.where(wrapped(s), fr, rot(f_b, s))
                vs_b = jnp.where(wrapped(s), vr, rot(v_b, s))
            else:
                fs_b = jnp.where(wrapped(s), 1.0, fr)
                vs_b = vr
            nxt_b = qd_b * fs_b
            o_b = o_b + pair_sum(qd_b - nxt_b, vs_b)
            qd_b = nxt_b
        o_a, o_b = o_a.reshape(nch, half, w), o_b.reshape(nch, half, w)
        o = jnp.stack([o_b, o_a] if rev else [o_a, o_b], axis=1).reshape(tt, w)
        p3 = p_in.reshape(nch, c, w)
        dec_end = p3[:, 0:1, :] if rev else p3[:, c - 1:c, :]
        q_in = (q * p_in).astype(BF16)
        lane3 = lax.broadcasted_iota(jnp.int32, (nch, c, w), 2) % (2 * HG_DK)

        def per_head(x):
            x3 = x.astype(BF16).reshape(nch, c, w)
            zero = jnp.zeros_like(x3)
            return jnp.concatenate([jnp.where(lane3 < HG_DK, x3, zero), jnp.where(lane3 >= HG_DK, x3, zero)], axis=1)

        v_hat, k_hat = per_head(v), per_head(kk * p_out)
        ntile = st_ref.shape[0]
        lt = w // ntile
        for ci in range(nch):
            for j in range(ntile):
                upd_ref[ci, j] = _dot_tn(v_hat[ci, :, j * lt:(j + 1) * lt], k_hat[ci, :, j * lt:(j + 1) * lt])
        st = [st_ref[j] for j in range(ntile)]
        for ci in (range(nch - 1, -1, -1) if rev else range(nch)):
            for j in range(ntile):
                sin_ref[ci, j] = st[j].astype(BF16)
                st[j] = st[j] * dec_end[ci][:, j * lt:(j + 1) * lt] + upd_ref[ci, j]
        for j in range(ntile):
            st_ref[j] = st[j]
        parts = [jnp.concatenate([_dot_nt(q_in[ci * c:(ci + 1) * c, j * lt:(j + 1) * lt], sin_ref[ci, j])
                                  for j in range(ntile)], axis=1) for ci in range(nch)]
        return o + jnp.concatenate(parts, axis=0)

    @pl.when(d == 0)
    def _():
        ofwd_ref[tile] = run(False)

    @pl.when(d == 1)
    def _():
        tot = run(True) + ofwd_ref[tile]
        sq = tot * tot
        hi = sq.astype(BF16)
        lo = (sq - hi.astype(F32)).astype(BF16)
        ms = (_dot(hi, jmat) + _dot(lo, jmat)) * (1.0 / HG_DK)
        y = tot * lax.rsqrt(ms + EPS)
        o_ref[...] = (y * ng_ref[...] * jax.nn.silu(hg_ref[...])).astype(BF16)


def _hgrn(ph, lb_logits, norm_g_all, *, layer, bsz, tiles_per_batch):
    n = ph.shape[0]
    depth = lb_logits.shape[0]
    w = norm_g_all.shape[-1]
    head = jnp.arange(w) // HG_DK
    same = head[:, None] == head[None, :]
    lt = 2 * HG_DK
    ntile = w // lt

    def col(cfn):
        def f(b, d, i):
            return (_seq_row(b, _seq_tile(d, i, tiles_per_batch), bsz, tiles_per_batch), cfn(d))
        return f

    def out_row(b, d, i):
        tile = jnp.where(d == 0, 0, _seq_tile(d, i, tiles_per_batch))
        return (_seq_row(b, tile, bsz, tiles_per_batch), 0)

    return pl.pallas_call(
        functools.partial(_hgrn_kernel, layer=layer, tiles_per_batch=tiles_per_batch),
        grid=(bsz, 2, tiles_per_batch + 1),
        in_specs=[
            pl.BlockSpec((depth, 1, 1, w), lambda b, d, i: (0, d, 0, 0)),
            pl.BlockSpec((TILE, w), col(lambda d: 0)),
            pl.BlockSpec((TILE, w), col(lambda d: 1 + d)),
            pl.BlockSpec((TILE, w), col(lambda d: 3)),
            pl.BlockSpec((TILE, w), col(lambda d: 4)),
            _layer_spec(norm_g_all, layer), _const_spec((w, w)),
        ],
        out_specs=pl.BlockSpec((TILE, w), out_row),
        out_shape=jax.ShapeDtypeStruct((n, w), BF16),
        scratch_shapes=[pltpu.VMEM((ntile, lt, lt), F32), pltpu.VMEM((tiles_per_batch + 1, TILE, w), F32),
                        pltpu.VMEM((TILE // HG_CHUNK, ntile, lt, lt), F32),
                        pltpu.VMEM((TILE // HG_CHUNK, ntile, lt, lt), BF16)],
        compiler_params=_cparams(("arbitrary", "arbitrary", "arbitrary")),
        name="hgrn2",
    )(lb_logits.reshape(depth, 2, 1, w), ph, ph, ph, ph, norm_g_all, same.astype(BF16))


def _rglru_kernel(x_ref, gate_ref, prev_ref, next_ref, cw_ref, cb_ref, wa_ref, ba_ref, wx_ref, bx_ref, lam_ref,
                  o_ref, h_ref, hfwd_ref, *, tiles_per_batch):
    d = pl.program_id(1)
    i = pl.program_id(2)
    tile = _seq_tile(d, i, tiles_per_batch)
    tt, w = x_ref.shape

    @pl.when(i == 0)
    def _():
        h_ref[...] = jnp.zeros_like(h_ref)

    x = x_ref[...]
    has_prev = (tile >= 2).astype(F32)
    has_next = jnp.logical_and(tile >= 1, tile < tiles_per_batch).astype(F32)
    prev = prev_ref[...] * has_prev
    nxt = next_ref[...] * has_next
    row = lax.broadcasted_iota(jnp.int32, (tt, w), 0)
    xm1 = jnp.where(row == 0, prev[HALO - 1:HALO], _shift_rows(x, 1, False))
    xm2 = jnp.where(row == 0, prev[HALO - 2:HALO - 1],
                    jnp.where(row == 1, prev[HALO - 1:HALO], _shift_rows(x, 2, False)))
    xp1 = jnp.where(row == tt - 1, nxt[0:1], _shift_rows(x, 1, True))
    cw = cw_ref[...]
    u = cb_ref[...] + xm2 * cw[0:1] + xm1 * cw[1:2] + x * cw[2:3] + xp1 * cw[3:4]

    u16 = u.astype(BF16)
    r = jax.nn.sigmoid(_dot(u16, wa_ref[0]) + ba_ref[0])
    ig = jax.nn.sigmoid(_dot(u16, wx_ref[0]) + bx_ref[0])
    log_a = -RG_C * r * jax.nn.softplus(-lam_ref[0])
    a = jnp.exp(log_a)
    z = jnp.sqrt(jnp.tanh(-log_a) * (1.0 + a * a)) * (ig * u)

    def scan(rev):
        ng = tt // SUBLANES
        aa, zz = a.reshape(ng, SUBLANES, w), z.reshape(ng, SUBLANES, w)
        r8 = lax.broadcasted_iota(jnp.int32, (ng, SUBLANES, w), 1)
        s = 1
        while s < SUBLANES:
            ok = (r8 <= SUBLANES - 1 - s) if rev else (r8 >= s)
            sh = (SUBLANES - s) if rev else s
            zz = zz + jnp.where(ok, aa * pltpu.roll(zz, sh, 1), 0.0)
            aa = aa * jnp.where(ok, pltpu.roll(aa, sh, 1), 1.0)
            s *= 2
        carry = h_ref[0:1, :]
        groups = [None] * ng
        for gi in (range(ng - 1, -1, -1) if rev else range(ng)):
            groups[gi] = aa[gi] * carry + zz[gi]
            carry = groups[gi][0:1] if rev else groups[gi][SUBLANES - 1:SUBLANES]
        h_ref[...] = jnp.broadcast_to(carry, h_ref.shape)
        return jnp.concatenate(groups, axis=0)

    @pl.when(d == 0)
    def _():
        hfwd_ref[tile] = scan(False)

    @pl.when(d == 1)
    def _():
        h = scan(True) + hfwd_ref[tile]
        o_ref[...] = (jax.nn.gelu(gate_ref[...]) * h).astype(BF16)


def _rglru(pr, conv_w, conv_b, wa, ba, wx, bx, lam, *, layer, bsz, tiles_per_batch):
    n = pr.shape[0]
    w = conv_b.shape[-1]
    hpt = TILE // HALO

    def seq_row(b, d, i):
        return _seq_row(b, _seq_tile(d, i, tiles_per_batch), bsz, tiles_per_batch)

    def out_row(b, d, i):
        tile = jnp.where(d == 0, 0, _seq_tile(d, i, tiles_per_batch))
        return (_seq_row(b, tile, bsz, tiles_per_batch), 0)

    per_dir = lambda b, d, i: (layer, d, 0, 0)
    return pl.pallas_call(
        functools.partial(_rglru_kernel, tiles_per_batch=tiles_per_batch),
        grid=(bsz, 2, tiles_per_batch + 1),
        in_specs=[
            pl.BlockSpec((TILE, w), lambda b, d, i: (seq_row(b, d, i), 0)),
            pl.BlockSpec((TILE, w), lambda b, d, i: (seq_row(b, d, i), 1)),
            pl.BlockSpec((HALO, w), lambda b, d, i: (jnp.maximum(seq_row(b, d, i) * hpt - 1, 0), 0)),
            pl.BlockSpec((HALO, w), lambda b, d, i: (jnp.minimum((seq_row(b, d, i) + 1) * hpt, n // HALO - 1), 0)),
            _layer_spec(conv_w, layer), _layer_spec(conv_b, layer),
            pl.BlockSpec((None, 1, w, w), per_dir), pl.BlockSpec((None, 1, 1, w), per_dir),
            pl.BlockSpec((None, 1, w, w), per_dir), pl.BlockSpec((None, 1, 1, w), per_dir),
            pl.BlockSpec((None, 1, 1, w), per_dir),
        ],
        out_specs=pl.BlockSpec((TILE, w), out_row),
        out_shape=jax.ShapeDtypeStruct((n, w), BF16),
        scratch_shapes=[pltpu.VMEM((SUBLANES, w), F32), pltpu.VMEM((tiles_per_batch + 1, TILE, w), F32)],
        compiler_params=_cparams(("arbitrary", "arbitrary", "arbitrary")),
        name="rglru",
    )(pr, pr, pr, pr, conv_w, conv_b, wa, ba, wx, bx, lam)


def _rope_tables(length):
    n_freq = HEAD_DIM // 4
    rows = length // GRID_W
    row = jnp.repeat(jnp.arange(rows), GRID_W)
    col = jnp.tile(jnp.arange(GRID_W), rows)
    inv_freq = ROPE_BASE ** (-jnp.arange(n_freq, dtype=F32) / n_freq)
    ang = jnp.stack([row[:, None] * inv_freq, col[:, None] * inv_freq], axis=1)
    cos, sin = jnp.cos(ang), jnp.sin(ang)
    zero = jnp.zeros_like(sin[:, 0])
    cos_h = jnp.concatenate([cos[:, 0], cos[:, 0], cos[:, 1], cos[:, 1]], axis=-1)
    sa_h = jnp.concatenate([-sin[:, 0], zero, -sin[:, 1], zero], axis=-1)
    sb_h = jnp.concatenate([zero, sin[:, 0], zero, sin[:, 1]], axis=-1)
    ctx = (jnp.ones((TOKEN_TILE, HEAD_DIM), F32), jnp.zeros((TOKEN_TILE, HEAD_DIM), F32), jnp.zeros((TOKEN_TILE, HEAD_DIM), F32))
    return tuple(jnp.tile(jnp.concatenate([c, t], axis=0), (1, 2)) for c, t in zip(ctx, (cos_h, sa_h, sb_h)))


def _block_diag(wb):
    nb, bw = wb.shape[-3], wb.shape[-1]
    out = jnp.zeros(wb.shape[:-3] + (nb * bw, nb * bw), wb.dtype)
    for j in range(nb):
        out = out.at[..., j * bw:(j + 1) * bw, j * bw:(j + 1) * bw].set(wb[..., j, :, :])
    return out


def _regroup_heads(t, axis):
    shp = t.shape
    t = t.reshape(shp[:axis] + (ATT_KV_HEADS, ATT_GROUP, HEAD_DIM) + shp[axis + 1:])
    return jnp.swapaxes(t, axis, axis + 1).reshape(shp)


def kernel(x, c, ctx, c_ctx, ada_w, ada_b, norm_ffn1, ffn1_w_in, ffn1_w_out, norm_mix, w_in, w_out, attn_sink, hg_lb_logits, hg_norm, rg_conv_w, rg_conv_b, rg_gate_a_w, rg_gate_a_b, rg_gate_x_w, rg_gate_x_b, rg_lambda, norm_ffn2, ffn2_w_in, ffn2_w_out, final_norm):
    bsz, length, d = x.shape
    depth = ada_w.shape[0]
    assert ctx.shape[1] == CTX_LEN == TILE and length % TOKEN_TILE == 0 and (bsz * CTX_LEN) % TOKEN_TILE == 0
    tpb = length // TILE
    geo = dict(n_ctx_tiles=bsz * CTX_LEN // TOKEN_TILE, tiles_per_batch=length // TOKEN_TILE)

    ctx2, x2 = ctx.reshape(bsz * CTX_LEN, d), x.reshape(bsz * length, d)
    cond = jnp.zeros((SUBLANES, d), F32).at[:bsz].set(c).at[bsz].set(c_ctx)
    mods_all = _adaln(cond, ada_w, ada_b).reshape(depth, SUBLANES, N_MODS, d)[:, :bsz + 1]
    tabs = _rope_tables(length)

    att_w = w_out.shape[1] // 2
    kv_w = ATT_KV_HEADS * HEAD_DIM
    hg_w = hg_norm.shape[1]
    rg_w = rg_conv_b.shape[1]
    widths = (att_w, kv_w, 5 * hg_w, 2 * rg_w)
    assert sum(widths) + kv_w == w_in.shape[2]

    to16 = lambda t: t.astype(BF16)
    f1_in, f1_out, f2_in, f2_out = map(to16, (ffn1_w_in, ffn1_w_out, ffn2_w_in, ffn2_w_out))
    wi16 = to16(jnp.concatenate([_regroup_heads(w_in[:, :, :att_w], 2), w_in[:, :, att_w:]], axis=2))
    wo16 = to16(jnp.concatenate([_regroup_heads(w_out[:, :att_w], 1), w_out[:, att_w:]], axis=1))
    wa_bd, wx_bd = to16(_block_diag(rg_gate_a_w)), to16(_block_diag(rg_gate_x_w))
    as_row = lambda t: t.reshape(t.shape[:-1] + (1, t.shape[-1]))
    g1, gm, g2, hgn = map(as_row, (norm_ffn1, norm_mix, norm_ffn2, hg_norm))
    conv_b, ba, bx, lam = map(as_row, (rg_conv_b, rg_gate_a_b, rg_gate_x_b, rg_lambda))

    xs = (ctx2, x2)
    for l in range(depth):
        last = l == depth - 1
        xs, q, k, v, ph, pr = _token_block(xs, mods_all, g1, f1_in, f1_out, layer=l, mod0=0,
                                           proj=(gm, wi16, tabs, widths), **geo)
        ya = _attention(q, k, v, attn_sink, layer=l, bsz=bsz, length=length, with_ctx_out=not last)
        yg = _hgrn(ph, hg_lb_logits, hgn, layer=l, bsz=bsz, tiles_per_batch=tpb)
        yr = _rglru(pr, rg_conv_w, conv_b, wa_bd, ba, wx_bd, bx, lam, layer=l, bsz=bsz, tiles_per_batch=tpb)
        xs, = _token_block(xs, mods_all, g2, f2_in, f2_out, layer=l, mod0=6, mix=(ya, yg, yr, wo16),
                           final_g=final_norm if last else None, latent_only=last, **geo)
    return xs.reshape(bsz, length, d)
```

```python
import functools

import jax
import jax.numpy as jnp
from jax import lax
from jax.experimental import pallas as pl
from jax.experimental.pallas import tpu as pltpu

F32 = jnp.float32
BF16 = jnp.bfloat16

GRID_W = 64
CTX_LEN = 256
N_MODS = 9
EPS = 1e-6
HEAD_DIM = 64
ATT_KV_HEADS = 2
ATT_GROUP = 4
WINDOW = 128
ATT_BLOCK = 128
ATT_QBLOCKS = 2
ROPE_BASE = 10000.0
HG_HEADS = 4
HG_DK = 64
HG_CHUNK = 16
RG_BLOCKS = 4
RG_CONV = 4
RG_C = 8.0

TILE = 256
TOKEN_TILE = 512
SUBLANES = 8
HALO = SUBLANES
VMEM_LIMIT = 52 * 1024 * 1024


def _cparams(sem):
    return pltpu.CompilerParams(dimension_semantics=sem, vmem_limit_bytes=VMEM_LIMIT)


def _const_spec(shape):
    nd = len(shape)
    return pl.BlockSpec(shape, lambda *_: (0,) * nd, pipeline_mode=pl.Buffered(1))


def _layer_spec(arr, l):
    nd = arr.ndim
    return pl.BlockSpec((None,) + arr.shape[1:], lambda *_: (l,) + (0,) * (nd - 1), pipeline_mode=pl.Buffered(1))


def _rms(xf, g):
    return xf * lax.rsqrt(jnp.mean(xf * xf, axis=-1, keepdims=True) + EPS) * g


def _dot(a, b):
    return jnp.dot(a, b, preferred_element_type=F32)


def _dot_nt(a, b):
    return lax.dot_general(a, b, (((1,), (1,)), ((), ())), preferred_element_type=F32)


def _dot_tn(a, b):
    return lax.dot_general(a, b, (((0,), (0,)), ((), ())), preferred_element_type=F32)


def _adaln_kernel(c_ref, w_ref, b_ref, o_ref):
    s = jax.nn.silu(c_ref[...]).astype(BF16)
    o_ref[0] = _dot(s, w_ref[0].astype(BF16)) + b_ref[0]


def _adaln(cond, ada_w, ada_b):
    depth, d, n = ada_w.shape
    tn = n // 8
    return pl.pallas_call(
        _adaln_kernel,
        grid=(depth, n // tn),
        in_specs=[
            pl.BlockSpec(cond.shape, lambda l, j: (0, 0)),
            pl.BlockSpec((1, d, tn), lambda l, j: (l, 0, j)),
            pl.BlockSpec((1, 1, tn), lambda l, j: (l, 0, j)),
        ],
        out_specs=pl.BlockSpec((1, cond.shape[0], tn), lambda l, j: (l, 0, j)),
        out_shape=jax.ShapeDtypeStruct((depth, cond.shape[0], n), F32),
        compiler_params=_cparams(("arbitrary", "arbitrary")),
        name="adaln",
    )(cond, ada_w, ada_b.reshape(depth, 1, n))


def _rope(t, cos, sa, sb):
    w = t.shape[1]
    rep = w // cos.shape[1]
    c, a, b = (jnp.concatenate([m] * rep, axis=1) if rep > 1 else m for m in (cos, sa, sb))
    half = HEAD_DIM // 4
    return t * c + pltpu.roll(t, w - half, 1) * a + pltpu.roll(t, half, 1) * b


def _token_kernel(*refs, mod0, n_ctx_tiles, two_src, has_mix, has_proj, has_final):
    it = iter(refs)
    if two_src:
        c_ref = next(it)
    x_ref, mods_ref, g_ref, win_ref, wout_ref = (next(it) for _ in range(5))
    if has_mix:
        ya_ref, yg_ref, yr_ref, wmix_ref = (next(it) for _ in range(4))
    if has_proj:
        gp_ref, wp_ref, cos_ref, sa_ref, sb_ref = (next(it) for _ in range(5))
    if has_final:
        fin_ref = next(it)
    o_ref = next(it)
    if has_proj:
        q_ref, k_ref, v_ref, ph_ref, pr_ref = (next(it) for _ in range(5))

    mods = mods_ref[0]
    shift, scale, gate = mods[mod0:mod0 + 1], mods[mod0 + 1:mod0 + 2], mods[mod0 + 2:mod0 + 3]
    f = wout_ref.shape[0]
    for r0 in range(0, x_ref.shape[0], TILE):
        rows = slice(r0, r0 + TILE)
        x = x_ref[rows, :]
        if two_src:
            x = jnp.where(pl.program_id(0) < n_ctx_tiles, c_ref[rows, :], x)
        if has_mix:
            y = jnp.concatenate([ya_ref[rows, :], yg_ref[rows, :], yr_ref[rows, :]], axis=-1)
            x = x + mods[5:6] * _dot(y, wmix_ref[...])
        h = (_rms(x, g_ref[...]) * (1.0 + scale) + shift).astype(BF16)
        a = _dot(h, win_ref[:, :f])
        b = _dot(h, win_ref[:, f:])
        u = (jax.nn.silu(a) * b).astype(BF16)
        out = x + (0.5 * gate) * _dot(u, wout_ref[...])
        o_ref[rows, :] = _rms(out, fin_ref[...]) if has_final else out
        if has_proj:
            h = (_rms(out, gp_ref[...]) * (1.0 + mods[4:5]) + mods[3:4]).astype(BF16)
            cos, sa, sb = cos_ref[rows, :], sa_ref[rows, :], sb_ref[rows, :]
            o = 0
            for ref, rotate, scl in ((q_ref, True, HEAD_DIM ** -0.5), (k_ref, True, None), (v_ref, False, None),
                                     (ph_ref, False, None), (pr_ref, False, None)):
                p = _dot(h, wp_ref[:, o:o + ref.shape[1]])
                o += ref.shape[1]
                if rotate:
                    p = _rope(p, cos, sa, sb)
                if scl is not None:
                    p = p * scl
                ref[rows, :] = p.astype(ref.dtype)


def _token_block(src, mods_all, g_all, w_in_all, w_out_all, *, layer, mod0, n_ctx_tiles, tiles_per_batch,
                 mix=None, proj=None, final_g=None, latent_only=False):
    two_src = isinstance(src, tuple)
    if two_src:
        ctx2, x2 = src
        n, d = ctx2.shape[0] + x2.shape[0], x2.shape[1]
    else:
        n, d = src.shape
    skip = n_ctx_tiles if latent_only else 0
    nt = n // TOKEN_TILE - skip
    ctx_row = mods_all.shape[1] - 1
    row = lambda i: (i + skip, 0)

    def mod_row(i):
        j = i + skip
        return (layer, jnp.where(j < n_ctx_tiles, ctx_row, (j - n_ctx_tiles) // tiles_per_batch), 0, 0)

    if two_src:
        in_specs = [pl.BlockSpec((TOKEN_TILE, d), lambda i: (jnp.minimum(i, n_ctx_tiles - 1), 0)),
                    pl.BlockSpec((TOKEN_TILE, d), lambda i: (jnp.maximum(i - n_ctx_tiles, 0), 0))]
        args = [ctx2, x2]
    else:
        in_specs, args = [pl.BlockSpec((TOKEN_TILE, d), row)], [src]
    in_specs += [pl.BlockSpec((None, 1, N_MODS, d), mod_row), _layer_spec(g_all, layer),
                 _layer_spec(w_in_all, layer), _layer_spec(w_out_all, layer)]
    args += [mods_all, g_all, w_in_all, w_out_all]
    if mix is not None:
        ya, yg, yr, w_mix_all = mix
        ya_row = (lambda i: (i, 0)) if ya.shape[0] == nt * TOKEN_TILE else row
        in_specs += [pl.BlockSpec((TOKEN_TILE, ya.shape[1]), ya_row), pl.BlockSpec((TOKEN_TILE, yg.shape[1]), row),
                     pl.BlockSpec((TOKEN_TILE, yr.shape[1]), row), _layer_spec(w_mix_all, layer)]
        args += [ya, yg, yr, w_mix_all]
    out_specs = [pl.BlockSpec((TOKEN_TILE, d), lambda i: (i, 0))]
    out_shape = [jax.ShapeDtypeStruct((nt * TOKEN_TILE, d), F32)]
    if proj is not None:
        gp_all, wp_all, tabs, widths = proj

        def tab_row(i):
            return (jnp.where(i < n_ctx_tiles, 0, 1 + (i - n_ctx_tiles) % tiles_per_batch), 0)

        in_specs += [_layer_spec(gp_all, layer), _layer_spec(wp_all, layer)]
        in_specs += [pl.BlockSpec((TOKEN_TILE, tabs[0].shape[1]), tab_row)] * 3
        args += [gp_all, wp_all, *tabs]
        nq, nk, nh, nr = widths
        out_specs += [pl.BlockSpec((TOKEN_TILE, wd), lambda i: (i, 0)) for wd in (nq, nk, nk, nh, nr)]
        out_shape += [jax.ShapeDtypeStruct((n, wd), dt) for wd, dt in
                      ((nq, BF16), (nk, BF16), (nk, BF16), (nh, F32), (nr, F32))]
    if final_g is not None:
        in_specs.append(_const_spec((1, d)))
        args.append(final_g.reshape(1, d))
    return pl.pallas_call(
        functools.partial(_token_kernel, mod0=mod0, n_ctx_tiles=n_ctx_tiles, two_src=two_src,
                          has_mix=mix is not None, has_proj=proj is not None, has_final=final_g is not None),
        grid=(nt,),
        in_specs=in_specs,
        out_specs=out_specs,
        out_shape=out_shape,
        compiler_params=_cparams(("arbitrary",)),
        name="ffn_out" if mix is not None else "ffn_in",
    )(*args)


def _attn_kernel(sink_ref, q_ref, kc_ref, vc_ref, *refs, layer, t0, n_ctx_blocks, length):
    nloc = ATT_QBLOCKS + 2
    k_refs, v_refs, o_ref = refs[:nloc], refs[nloc:2 * nloc], refs[2 * nloc]
    n0 = (pl.program_id(1) + t0) * ATT_QBLOCKS - n_ctx_blocks
    k_ctx, v_ctx = kc_ref[...], vc_ref[...]
    sw = k_ctx.shape[1]
    nslab = q_ref.shape[1] // sw

    def only(t, kv):
        lane = lax.broadcasted_iota(jnp.int32, t.shape, 1)
        return jnp.where((lane >= kv * HEAD_DIM) & (lane < (kv + 1) * HEAD_DIM), t, jnp.zeros_like(t))

    chains = [(j, kv) for j in range(ATT_QBLOCKS) for kv in range(ATT_KV_HEADS)]
    q, k_loc, v_loc, valid = [], [], [], []
    for j in range(ATT_QBLOCKS):
        rows = slice(j * ATT_BLOCK, (j + 1) * ATT_BLOCK)
        q.append(jnp.concatenate([q_ref[rows, g * sw:(g + 1) * sw] for g in range(nslab)], axis=0))
        k_loc.append(jnp.concatenate([r[...] for r in k_refs[j:j + 3]], axis=0))
        v_loc.append(jnp.concatenate([r[...] for r in v_refs[j:j + 3]], axis=0))
        n = n0 + j
        k_pos = (n - 1) * ATT_BLOCK + lax.broadcasted_iota(jnp.int32, (3 * ATT_BLOCK, ATT_BLOCK), 0)
        q_pos = n * ATT_BLOCK + lax.broadcasted_iota(jnp.int32, (3 * ATT_BLOCK, ATT_BLOCK), 1)
        ok = (jnp.abs(k_pos - q_pos) <= WINDOW) & (k_pos >= 0) & (k_pos < length) & (n >= 0)
        valid.append(jnp.concatenate([ok] * nslab, axis=1))
    sink = {kv: jnp.concatenate([jnp.full((1, ATT_BLOCK), sink_ref[layer, kv * ATT_GROUP + g], F32)
                                 for g in range(nslab)], axis=1) for kv in range(ATT_KV_HEADS)}
    s_loc = {c: jnp.where(valid[c[0]], _dot_nt(only(k_loc[c[0]], c[1]), q[c[0]]), -jnp.inf) for c in chains}
    s_ctx = {c: _dot_nt(only(k_ctx, c[1]), q[c[0]]) for c in chains}
    m = {c: jnp.maximum(jnp.maximum(s_loc[c].max(0, keepdims=True), s_ctx[c].max(0, keepdims=True)), sink[c[1]])
         for c in chains}
    p_loc = {c: jnp.exp(s_loc[c] - m[c]) for c in chains}
    p_ctx = {c: jnp.exp(s_ctx[c] - m[c]) for c in chains}
    denom = {c: p_loc[c].sum(0, keepdims=True) + p_ctx[c].sum(0, keepdims=True) + jnp.exp(sink[c[1]] - m[c])
             for c in chains}
    o = {c: _dot_tn(only(v_loc[c[0]], c[1]), p_loc[c].astype(BF16)) + _dot_tn(only(v_ctx, c[1]), p_ctx[c].astype(BF16))
         for c in chains}
    for j in range(ATT_QBLOCKS):
        out = sum(o[(j, kv)] / denom[(j, kv)] for kv in range(ATT_KV_HEADS)).T
        for g in range(nslab):
            o_ref[j * ATT_BLOCK:(j + 1) * ATT_BLOCK, g * sw:(g + 1) * sw] = (
                out[g * ATT_BLOCK:(g + 1) * ATT_BLOCK].astype(BF16))


def _attention(q, k, v, sink, *, layer, bsz, length, with_ctx_out):
    n, nq = q.shape
    nk = k.shape[1]
    qb = ATT_QBLOCKS * ATT_BLOCK
    cb = CTX_LEN // qb
    lb = length // qb
    assert CTX_LEN % qb == 0 and length % qb == 0
    t0 = 0 if with_ctx_out else cb
    nblk = length // ATT_BLOCK
    blk0 = bsz * CTX_LEN // ATT_BLOCK

    def q_row(b, i):
        t = i + t0
        return (jnp.where(t < cb, cb * b + t, cb * bsz + lb * b + (t - cb)), 0)

    def loc_row(off):
        def f(b, i):
            nn = jnp.clip((i + t0 - cb) * ATT_QBLOCKS + off, 0, nblk - 1)
            return (blk0 + nblk * b + nn, 0)
        return f

    def o_row(b, i):
        r, c = q_row(b, i)
        return (r if with_ctx_out else r - cb * bsz, c)

    ctx_row = lambda b, i: (b, 0)
    kspec = [pl.BlockSpec((ATT_BLOCK, nk), loc_row(o)) for o in range(-1, ATT_QBLOCKS + 1)]
    return pl.pallas_call(
        functools.partial(_attn_kernel, layer=layer, t0=t0, n_ctx_blocks=CTX_LEN // ATT_BLOCK, length=length),
        grid=(bsz, cb + lb - t0),
        in_specs=[pl.BlockSpec(memory_space=pltpu.SMEM), pl.BlockSpec((qb, nq), q_row),
                  pl.BlockSpec((CTX_LEN, nk), ctx_row), pl.BlockSpec((CTX_LEN, nk), ctx_row)] + kspec + kspec,
        out_specs=pl.BlockSpec((qb, nq), o_row),
        out_shape=jax.ShapeDtypeStruct((n - t0 * bsz * qb, nq), BF16),
        compiler_params=_cparams(("arbitrary", "arbitrary")),
        name="attention",
    )(sink, q, k, v, *([k] * len(kspec)), *([v] * len(kspec)))


def _seq_tile(d, i, tiles_per_batch):
    return jnp.where(d == 0, i, jnp.where(i == 0, 0, tiles_per_batch + 1 - i))


def _seq_row(b, tile, bsz, tiles_per_batch):
    return jnp.where(tile == 0, b, bsz + tiles_per_batch * b + tile - 1)


def _shift_rows(x, s, rev):
    if s == 0:
        return x
    return pltpu.roll(x, (x.shape[0] - s) if rev else s, 0)


def _hgrn_body(lbl_ref, hq_ref, fz_ref, hi_ref, hg_ref, ng_ref, jmat_ref, st_ref, upd_ref, *, layer):
    tt, w = hq_ref.shape
    c = HG_CHUNK

    logits = lbl_ref[:, 0, 0, :]
    e = jnp.exp(logits - logits.max(0, keepdims=True))
    lbp = e / e.sum(0, keepdims=True)
    lb = jnp.zeros((1, w), F32)
    for j in range(1, layer + 1):
        lb = lb + lbp[j:j + 1]

    q = jax.nn.silu(hq_ref[...])
    f = lb + (1.0 - lb) * jax.nn.sigmoid(fz_ref[...])
    kk = 1.0 - f
    v = hi_ref[...]
    jmat = jmat_ref[...]
    pos = lax.broadcasted_iota(jnp.int32, (tt, w), 0) % c

    def run(rev):
        def in_chunk(s, r):
            return (pos <= c - 1 - s) if r else (pos >= s)

        def cumprod(y, r):
            s = 1
            while s < c:
                y = y * jnp.where(in_chunk(s, r), _shift_rows(y, s, r), 1.0)
                s *= 2
            return y

        p_in = cumprod(f, rev)
        p_out = cumprod(jnp.where(in_chunk(1, not rev), _shift_rows(f, 1, not rev), 1.0), not rev)
        half = c // 2
        nch = tt // c

        def halves(x):
            x4 = x.reshape(nch, 2, half, w)
            return (x4[:, 1], x4[:, 0]) if rev else (x4[:, 0], x4[:, 1])

        r8 = lax.broadcasted_iota(jnp.int32, (nch, half, w), 1)

        def rot(x, s):
            return x if s == 0 else pltpu.roll(x, (half - s) if rev else s, 1)

        def wrapped(s):
            return (r8 > half - 1 - s) if rev else (r8 < s)

        def pair_sum(wgt, vs):
            return (_dot(wgt.reshape(nch * half, w).astype(BF16), jmat) * vs.reshape(nch * half, w))

        (f_a, f_b), (v_a, v_b), (qd_a, qd_b) = halves(f), halves(v), halves(q)
        o_a = jnp.zeros((nch * half, w), F32)
        o_b = jnp.zeros((nch * half, w), F32)
        for delta in range(c):
            s = delta % half
            fr, vr = rot(f_a, s), rot(v_a, s)
            if delta < half:
                fs_a = jnp.where(wrapped(s), 1.0, fr)
                nxt_a = qd_a * fs_a
                o_a = o_a + pair_sum(qd_a - nxt_a, vr)
                qd_a = nxt_a
                fs_b = jnp.where(wrapped(s), fr, rot(f_b, s))
                vs_b = jnp.where(wrapped(s), vr, rot(v_b, s))
            else:
                fs_b = jnp.where(wrapped(s), 1.0, fr)
                vs_b = vr
            nxt_b = qd_b * fs_b
            o_b = o_b + pair_sum(qd_b - nxt_b, vs_b)
            qd_b = nxt_b
        o_a, o_b = o_a.reshape(nch, half, w), o_b.reshape(nch, half, w)
        o = jnp.stack([o_b, o_a] if rev else [o_a, o_b], axis=1).reshape(tt, w)
        p3 = p_in.reshape(nch, c, w)
        dec_end = p3[:, 0:1, :] if rev else p3[:, c - 1:c, :]
        q_in = (q * p_in).astype(BF16)
        lane3 = lax.broadcasted_iota(jnp.int32, (nch, c, w), 2) % (2 * HG_DK)

        def per_head(x):
            x3 = x.astype(BF16).reshape(nch, c, w)
            zero = jnp.zeros_like(x3)
            return jnp.concatenate([jnp.where(lane3 < HG_DK, x3, zero), jnp.where(lane3 >= HG_DK, x3, zero)], axis=1)

        v_hat, k_hat = per_head(v), per_head(kk * p_out)
        ntile = st_ref.shape[0]
        lt = w // ntile
        for ci in range(nch):
            for j in range(ntile):
                upd_ref[ci, j] = _dot_tn(v_hat[ci, :, j * lt:(j + 1) * lt], k_hat[ci, :, j * lt:(j + 1) * lt])
        st = [st_ref[j] for j in range(ntile)]
        parts = [None] * nch
        for ci in (range(nch - 1, -1, -1) if rev else range(nch)):
            parts[ci] = jnp.concatenate([_dot_nt(q_in[ci * c:(ci + 1) * c, j * lt:(j + 1) * lt], st[j].astype(BF16))
                                         for j in range(ntile)], axis=1)
            for j in range(ntile):
                st[j] = st[j] * dec_end[ci][:, j * lt:(j + 1) * lt] + upd_ref[ci, j]
        for j in range(ntile):
            st_ref[j] = st[j]
        return o + jnp.concatenate(parts, axis=0)

    def finish(tot):
        sq = tot * tot
        hi = sq.astype(BF16)
        lo = (sq - hi.astype(F32)).astype(BF16)
        ms = (_dot(hi, jmat) + _dot(lo, jmat)) * (1.0 / HG_DK)
        y = tot * lax.rsqrt(ms + EPS)
        return (y * ng_ref[...] * jax.nn.silu(hg_ref[...])).astype(BF16)

    return run, finish


def _rglru_body(x_ref, gate_ref, prev_ref, next_ref, cw_ref, cb_ref, wa_ref, ba_ref, wx_ref, bx_ref, lam_ref, h_ref,
                *, tile, tiles_per_batch):
    tt, w = x_ref.shape

    x = x_ref[...]
    has_prev = (tile >= 2).astype(F32)
    has_next = jnp.logical_and(tile >= 1, tile < tiles_per_batch).astype(F32)
    prev = prev_ref[...] * has_prev
    nxt = next_ref[...] * has_next
    row = lax.broadcasted_iota(jnp.int32, (tt, w), 0)
    xm1 = jnp.where(row == 0, prev[HALO - 1:HALO], _shift_rows(x, 1, False))
    xm2 = jnp.where(row == 0, prev[HALO - 2:HALO - 1],
                    jnp.where(row == 1, prev[HALO - 1:HALO], _shift_rows(x, 2, False)))
    xp1 = jnp.where(row == tt - 1, nxt[0:1], _shift_rows(x, 1, True))
    cw = cw_ref[...]
    u = cb_ref[...] + xm2 * cw[0:1] + xm1 * cw[1:2] + x * cw[2:3] + xp1 * cw[3:4]

    u16 = u.astype(BF16)
    r = jax.nn.sigmoid(_dot(u16, wa_ref[0]) + ba_ref[0])
    ig = jax.nn.sigmoid(_dot(u16, wx_ref[0]) + bx_ref[0])
    log_a = -RG_C * r * jax.nn.softplus(-lam_ref[0])
    a = jnp.exp(log_a)
    z = jnp.sqrt(jnp.tanh(-log_a) * (1.0 + a * a)) * (ig * u)

    def scan(rev):
        ng = tt // SUBLANES
        aa, zz = a.reshape(ng, SUBLANES, w), z.reshape(ng, SUBLANES, w)
        r8 = lax.broadcasted_iota(jnp.int32, (ng, SUBLANES, w), 1)
        s = 1
        while s < SUBLANES:
            ok = (r8 <= SUBLANES - 1 - s) if rev else (r8 >= s)
            sh = (SUBLANES - s) if rev else s
            zz = zz + jnp.where(ok, aa * pltpu.roll(zz, sh, 1), 0.0)
            aa = aa * jnp.where(ok, pltpu.roll(aa, sh, 1), 1.0)
            s *= 2
        carry = h_ref[0:1, :]
        groups = [None] * ng
        for gi in (range(ng - 1, -1, -1) if rev else range(ng)):
            groups[gi] = aa[gi] * carry + zz[gi]
            carry = groups[gi][0:1] if rev else groups[gi][SUBLANES - 1:SUBLANES]
        h_ref[...] = jnp.broadcast_to(carry, h_ref.shape)
        return jnp.concatenate(groups, axis=0)

    def finish(h):
        return (jax.nn.gelu(gate_ref[...]) * h).astype(BF16)

    return scan, finish


def _mixers_kernel(lbl_ref, hq_ref, fz_ref, hi_ref, hg_ref, ng_ref, jmat_ref,
                   x_ref, gate_ref, prev_ref, next_ref, cw_ref, cb_ref, wa_ref, ba_ref, wx_ref, bx_ref, lam_ref,
                   yg_ref, yr_ref, st_ref, ofwd_ref, upd_ref, h_ref, hfwd_ref, *, layer, tiles_per_batch):
    d = pl.program_id(1)
    i = pl.program_id(2)
    tile = _seq_tile(d, i, tiles_per_batch)

    @pl.when(i == 0)
    def _():
        st_ref[...] = jnp.zeros_like(st_ref)
        h_ref[...] = jnp.zeros_like(h_ref)

    run, finish_hg = _hgrn_body(lbl_ref, hq_ref, fz_ref, hi_ref, hg_ref, ng_ref, jmat_ref, st_ref, upd_ref,
                                layer=layer)
    scan, finish_rg = _rglru_body(x_ref, gate_ref, prev_ref, next_ref, cw_ref, cb_ref, wa_ref, ba_ref, wx_ref, bx_ref,
                                  lam_ref, h_ref, tile=tile, tiles_per_batch=tiles_per_batch)

    @pl.when(d == 0)
    def _():
        ofwd_ref[tile] = run(False)
        hfwd_ref[tile] = scan(False)

    @pl.when(d == 1)
    def _():
        yg_ref[...] = finish_hg(run(True) + ofwd_ref[tile])
        yr_ref[...] = finish_rg(scan(True) + hfwd_ref[tile])


def _mixers(ph, pr, lb_logits, norm_g_all, conv_w, conv_b, wa, ba, wx, bx, lam, *, layer, bsz, tiles_per_batch):
    n = ph.shape[0]
    depth = lb_logits.shape[0]
    hw = norm_g_all.shape[-1]
    rw = conv_b.shape[-1]
    head = jnp.arange(hw) // HG_DK
    same = head[:, None] == head[None, :]
    lt = 2 * HG_DK
    ntile = hw // lt
    hpt = TILE // HALO

    def seq_row(b, d, i):
        return _seq_row(b, _seq_tile(d, i, tiles_per_batch), bsz, tiles_per_batch)

    def col(cfn):
        return lambda b, d, i: (seq_row(b, d, i), cfn(d))

    def out_row(b, d, i):
        tile = jnp.where(d == 0, 0, _seq_tile(d, i, tiles_per_batch))
        return (_seq_row(b, tile, bsz, tiles_per_batch), 0)

    per_dir = lambda b, d, i: (layer, d, 0, 0)
    return pl.pallas_call(
        functools.partial(_mixers_kernel, layer=layer, tiles_per_batch=tiles_per_batch),
        grid=(bsz, 2, tiles_per_batch + 1),
        in_specs=[
            pl.BlockSpec((depth, 1, 1, hw), lambda b, d, i: (0, d, 0, 0)),
            pl.BlockSpec((TILE, hw), col(lambda d: 0)),
            pl.BlockSpec((TILE, hw), col(lambda d: 1 + d)),
            pl.BlockSpec((TILE, hw), col(lambda d: 3)),
            pl.BlockSpec((TILE, hw), col(lambda d: 4)),
            _layer_spec(norm_g_all, layer), _const_spec((hw, hw)),
            pl.BlockSpec((TILE, rw), col(lambda d: 0)),
            pl.BlockSpec((TILE, rw), col(lambda d: 1)),
            pl.BlockSpec((HALO, rw), lambda b, d, i: (jnp.maximum(seq_row(b, d, i) * hpt - 1, 0), 0)),
            pl.BlockSpec((HALO, rw), lambda b, d, i: (jnp.minimum((seq_row(b, d, i) + 1) * hpt, n // HALO - 1), 0)),
            _layer_spec(conv_w, layer), _layer_spec(conv_b, layer),
            pl.BlockSpec((None, 1, rw, rw), per_dir), pl.BlockSpec((None, 1, 1, rw), per_dir),
            pl.BlockSpec((None, 1, rw, rw), per_dir), pl.BlockSpec((None, 1, 1, rw), per_dir),
            pl.BlockSpec((None, 1, 1, rw), per_dir),
        ],
        out_specs=[pl.BlockSpec((TILE, hw), out_row), pl.BlockSpec((TILE, rw), out_row)],
        out_shape=[jax.ShapeDtypeStruct((n, hw), BF16), jax.ShapeDtypeStruct((n, rw), BF16)],
        scratch_shapes=[pltpu.VMEM((ntile, lt, lt), F32), pltpu.VMEM((tiles_per_batch + 1, TILE, hw), F32),
                        pltpu.VMEM((TILE // HG_CHUNK, ntile, lt, lt), F32),
                        pltpu.VMEM((SUBLANES, rw), F32), pltpu.VMEM((tiles_per_batch + 1, TILE, rw), F32)],
        compiler_params=_cparams(("arbitrary", "arbitrary", "arbitrary")),
        name="mixers",
    )(lb_logits.reshape(depth, 2, 1, hw), ph, ph, ph, ph, norm_g_all, same.astype(BF16),
      pr, pr, pr, pr, conv_w, conv_b, wa, ba, wx, bx, lam)


def _rope_tables(length):
    n_freq = HEAD_DIM // 4
    rows = length // GRID_W
    row = jnp.repeat(jnp.arange(rows), GRID_W)
    col = jnp.tile(jnp.arange(GRID_W), rows)
    inv_freq = ROPE_BASE ** (-jnp.arange(n_freq, dtype=F32) / n_freq)
    ang = jnp.stack([row[:, None] * inv_freq, col[:, None] * inv_freq], axis=1)
    cos, sin = jnp.cos(ang), jnp.sin(ang)
    zero = jnp.zeros_like(sin[:, 0])
    cos_h = jnp.concatenate([cos[:, 0], cos[:, 0], cos[:, 1], cos[:, 1]], axis=-1)
    sa_h = jnp.concatenate([-sin[:, 0], zero, -sin[:, 1], zero], axis=-1)
    sb_h = jnp.concatenate([zero, sin[:, 0], zero, sin[:, 1]], axis=-1)
    ctx = (jnp.ones((TOKEN_TILE, HEAD_DIM), F32), jnp.zeros((TOKEN_TILE, HEAD_DIM), F32), jnp.zeros((TOKEN_TILE, HEAD_DIM), F32))
    return tuple(jnp.tile(jnp.concatenate([c, t], axis=0), (1, 2)) for c, t in zip(ctx, (cos_h, sa_h, sb_h)))


def _block_diag(wb):
    nb, bw = wb.shape[-3], wb.shape[-1]
    out = jnp.zeros(wb.shape[:-3] + (nb * bw, nb * bw), wb.dtype)
    for j in range(nb):
        out = out.at[..., j * bw:(j + 1) * bw, j * bw:(j + 1) * bw].set(wb[..., j, :, :])
    return out


def _regroup_heads(t, axis):
    shp = t.shape
    t = t.reshape(shp[:axis] + (ATT_KV_HEADS, ATT_GROUP, HEAD_DIM) + shp[axis + 1:])
    return jnp.swapaxes(t, axis, axis + 1).reshape(shp)


def kernel(x, c, ctx, c_ctx, ada_w, ada_b, norm_ffn1, ffn1_w_in, ffn1_w_out, norm_mix, w_in, w_out, attn_sink, hg_lb_logits, hg_norm, rg_conv_w, rg_conv_b, rg_gate_a_w, rg_gate_a_b, rg_gate_x_w, rg_gate_x_b, rg_lambda, norm_ffn2, ffn2_w_in, ffn2_w_out, final_norm):
    bsz, length, d = x.shape
    depth = ada_w.shape[0]
    assert ctx.shape[1] == CTX_LEN == TILE and length % TOKEN_TILE == 0 and (bsz * CTX_LEN) % TOKEN_TILE == 0
    tpb = length // TILE
    geo = dict(n_ctx_tiles=bsz * CTX_LEN // TOKEN_TILE, tiles_per_batch=length // TOKEN_TILE)

    ctx2, x2 = ctx.reshape(bsz * CTX_LEN, d), x.reshape(bsz * length, d)
    cond = jnp.zeros((SUBLANES, d), F32).at[:bsz].set(c).at[bsz].set(c_ctx)
    mods_all = _adaln(cond, ada_w, ada_b).reshape(depth, SUBLANES, N_MODS, d)[:, :bsz + 1]
    tabs = _rope_tables(length)

    att_w = w_out.shape[1] // 2
    kv_w = ATT_KV_HEADS * HEAD_DIM
    hg_w = hg_norm.shape[1]
    rg_w = rg_conv_b.shape[1]
    widths = (att_w, kv_w, 5 * hg_w, 2 * rg_w)
    assert sum(widths) + kv_w == w_in.shape[2]

    to16 = lambda t: t.astype(BF16)
    f1_in, f1_out, f2_in, f2_out = map(to16, (ffn1_w_in, ffn1_w_out, ffn2_w_in, ffn2_w_out))
    wi16 = to16(jnp.concatenate([_regroup_heads(w_in[:, :, :att_w], 2), w_in[:, :, att_w:]], axis=2))
    wo16 = to16(jnp.concatenate([_regroup_heads(w_out[:, :att_w], 1), w_out[:, att_w:]], axis=1))
    wa_bd, wx_bd = to16(_block_diag(rg_gate_a_w)), to16(_block_diag(rg_gate_x_w))
    as_row = lambda t: t.reshape(t.shape[:-1] + (1, t.shape[-1]))
    g1, gm, g2, hgn = map(as_row, (norm_ffn1, norm_mix, norm_ffn2, hg_norm))
    conv_b, ba, bx, lam = map(as_row, (rg_conv_b, rg_gate_a_b, rg_gate_x_b, rg_lambda))

    xs = (ctx2, x2)
    for l in range(depth):
        last = l == depth - 1
        xs, q, k, v, ph, pr = _token_block(xs, mods_all, g1, f1_in, f1_out, layer=l, mod0=0,
                                           proj=(gm, wi16, tabs, widths), **geo)
        ya = _attention(q, k, v, attn_sink, layer=l, bsz=bsz, length=length, with_ctx_out=not last)
        yg, yr = _mixers(ph, pr, hg_lb_logits, hgn, rg_conv_w, conv_b, wa_bd, ba, wx_bd, bx, lam,
                         layer=l, bsz=bsz, tiles_per_batch=tpb)
        xs, = _token_block(xs, mods_all, g2, f2_in, f2_out, layer=l, mod0=6, mix=(ya, yg, yr, wo16),
                           final_g=final_norm if last else None, latent_only=last, **geo)
    return xs.reshape(bsz, length, d)
```

```python
import functools
import math

import jax
import jax.numpy as jnp
from jax import lax
from jax.experimental import pallas as pl
from jax.experimental.pallas import tpu as pltpu

F32 = jnp.float32
BF16 = jnp.bfloat16

GRID_W = 64
CTX_LEN = 256
N_MODS = 9
EPS = 1e-6
HEAD_DIM = 64
ATT_KV_HEADS = 2
ATT_GROUP = 4
WINDOW = 128
ATT_BLOCK = 128
ATT_QBLOCKS = 2
ROPE_BASE = 10000.0
LOG2E = 1.4426950408889634
HG_HEADS = 4
HG_DK = 64
HG_CHUNK = 16
HG_MIN_DECAY = 1e-30
RG_BLOCKS = 4
RG_CONV = 4
RG_C = 8.0

TILE = 256
TOKEN_TILE = 512
SUBLANES = 8
HALO = SUBLANES
VMEM_LIMIT = 52 * 1024 * 1024


def _cparams(sem):
    return pltpu.CompilerParams(dimension_semantics=sem, vmem_limit_bytes=VMEM_LIMIT)


def _const_spec(shape):
    nd = len(shape)
    return pl.BlockSpec(shape, lambda *_: (0,) * nd, pipeline_mode=pl.Buffered(1))


def _layer_spec(arr, l):
    nd = arr.ndim
    return pl.BlockSpec((None,) + arr.shape[1:], lambda *_: (l,) + (0,) * (nd - 1), pipeline_mode=pl.Buffered(1))


def _rms(xf, g):
    return xf * lax.rsqrt(jnp.mean(xf * xf, axis=-1, keepdims=True) + EPS) * g


def _dot(a, b):
    return jnp.dot(a, b, preferred_element_type=F32)


def _dot_nt(a, b):
    return lax.dot_general(a, b, (((1,), (1,)), ((), ())), preferred_element_type=F32)


def _dot_tn(a, b):
    return lax.dot_general(a, b, (((0,), (0,)), ((), ())), preferred_element_type=F32)


def _adaln_kernel(c_ref, w_ref, b_ref, o_ref):
    s = jax.nn.silu(c_ref[...]).astype(BF16)
    o_ref[0] = _dot(s, w_ref[0].astype(BF16)) + b_ref[0]


def _adaln(cond, ada_w, ada_b):
    depth, d, n = ada_w.shape
    tn = n // 8
    return pl.pallas_call(
        _adaln_kernel,
        grid=(depth, n // tn),
        in_specs=[
            pl.BlockSpec(cond.shape, lambda l, j: (0, 0)),
            pl.BlockSpec((1, d, tn), lambda l, j: (l, 0, j)),
            pl.BlockSpec((1, 1, tn), lambda l, j: (l, 0, j)),
        ],
        out_specs=pl.BlockSpec((1, cond.shape[0], tn), lambda l, j: (l, 0, j)),
        out_shape=jax.ShapeDtypeStruct((depth, cond.shape[0], n), F32),
        compiler_params=_cparams(("arbitrary", "arbitrary")),
        name="adaln",
    )(cond, ada_w, ada_b.reshape(depth, 1, n))


def _attend(sink_ref, layer, q_blocks, k_loc, v_loc, k_ctx, v_ctx, n_blocks, length):
    sw = k_ctx[0].shape[1]
    nslab = q_blocks[0].shape[1] // sw

    def only(t, kv, fill=0.0):
        lane = lax.broadcasted_iota(jnp.int32, t.shape, 1)
        return jnp.where((lane >= kv * HEAD_DIM) & (lane < (kv + 1) * HEAD_DIM), t, jnp.full_like(t, fill))

    nq = len(q_blocks)
    chains = [(j, kv) for j in range(nq) for kv in range(ATT_KV_HEADS)]
    q, valid = [], []
    for j in range(nq):
        q.append(jnp.concatenate([q_blocks[j][:, g * sw:(g + 1) * sw] for g in range(nslab)], axis=0))
        n = n_blocks[j]
        k_pos = (n - 1) * ATT_BLOCK + lax.broadcasted_iota(jnp.int32, (3 * ATT_BLOCK, ATT_BLOCK), 0)
        q_pos = n * ATT_BLOCK + lax.broadcasted_iota(jnp.int32, (3 * ATT_BLOCK, ATT_BLOCK), 1)
        ok = (jnp.abs(k_pos - q_pos) <= WINDOW) & (k_pos >= 0) & (k_pos < length) & (n >= 0)
        valid.append(jnp.concatenate([ok] * nslab, axis=1))
    sink = {kv: jnp.concatenate([jnp.full((1, ATT_BLOCK), sink_ref[layer, kv * ATT_GROUP + g] * LOG2E, F32)
                                 for g in range(nslab)], axis=1) for kv in range(ATT_KV_HEADS)}
    s_loc = {c: jnp.where(valid[c[0]], _dot_nt(only(k_loc[c[0]], c[1]), q[c[0]]), -jnp.inf) for c in chains}
    s_ctx = {c: _dot_nt(only(k_ctx[c[0]], c[1]), q[c[0]]) for c in chains}
    m = {c: jnp.maximum(jnp.maximum(s_loc[c].max(0, keepdims=True), s_ctx[c].max(0, keepdims=True)), sink[c[1]])
         for c in chains}
    p_loc = {c: jnp.exp2(s_loc[c] - m[c]).astype(BF16) for c in chains}
    p_ctx = {c: jnp.exp2(s_ctx[c] - m[c]).astype(BF16) for c in chains}
    o = {c: (_dot_tn(only(v_loc[c[0]], c[1], 1.0), p_loc[c]) + _dot_tn(only(v_ctx[c[0]], c[1], 1.0), p_ctx[c]))
         for c in chains}
    row_kv = lax.broadcasted_iota(jnp.int32, (sw, nslab * ATT_BLOCK), 0) // HEAD_DIM
    outs = []
    for j in range(nq):
        scaled = []
        for kv in range(ATT_KV_HEADS):
            c = (j, kv)
            other = (1 - kv) * HEAD_DIM
            denom = o[c][other:other + 1, :] + jnp.exp2(sink[kv] - m[c])
            scaled.append(o[c] / denom)
        out = jnp.where(row_kv == 0, scaled[0], scaled[1]).T
        outs.append(jnp.concatenate([out[g * ATT_BLOCK:(g + 1) * ATT_BLOCK] for g in range(nslab)],
                                    axis=1).astype(BF16))
    return outs


def _rope(t, cos, sa, sb):
    w = t.shape[1]
    rep = w // cos.shape[1]
    c, a, b = (jnp.concatenate([m] * rep, axis=1) if rep > 1 else m for m in (cos, sa, sb))
    half = HEAD_DIM // 4
    return t * c + pltpu.roll(t, w - half, 1) * a + pltpu.roll(t, half, 1) * b


def _token_kernel(*refs, layer, mod0, n_ctx_tiles, tiles_per_batch, skip, length, two_src, has_mix, has_proj,
                  has_final):
    it = iter(refs)
    if two_src:
        c_ref = next(it)
    x_ref, mods_ref, g_ref, win_ref, wout_ref = (next(it) for _ in range(5))
    if has_mix:
        nloc = TOKEN_TILE // ATT_BLOCK + 2
        sink_ref, q_in_ref = next(it), next(it)
        kc_refs, vc_refs = [next(it) for _ in range(2)], [next(it) for _ in range(2)]
        kl_refs, vl_refs = [next(it) for _ in range(nloc)], [next(it) for _ in range(nloc)]
        yg_ref, yr_ref, wmix_ref = (next(it) for _ in range(3))
    if has_proj:
        gp_ref, wp_ref, cos_ref, sa_ref, sb_ref = (next(it) for _ in range(5))
    if has_final:
        fin_ref = next(it)
    o_ref = next(it)
    if has_proj:
        q_ref, k_ref, v_ref, ph_ref, pr_ref = (next(it) for _ in range(5))

    mods = mods_ref[0]
    shift, scale, gate = mods[mod0:mod0 + 1], mods[mod0 + 1:mod0 + 2], mods[mod0 + 2:mod0 + 3]
    f = wout_ref.shape[0]
    ya_all = None
    if has_mix:
        lt = pl.program_id(0) + skip - n_ctx_tiles
        bpt = TOKEN_TILE // ATT_BLOCK
        n_first = jnp.where(lt >= 0, (lt % tiles_per_batch) * bpt, -bpt)
        ya_all = _attend(
            sink_ref, layer,
            [q_in_ref[g * ATT_BLOCK:(g + 1) * ATT_BLOCK, :] for g in range(bpt)],
            [jnp.concatenate([r[...] for r in kl_refs[g:g + 3]], axis=0) for g in range(bpt)],
            [jnp.concatenate([r[...] for r in vl_refs[g:g + 3]], axis=0) for g in range(bpt)],
            [kc_refs[g * ATT_BLOCK // TILE][...] for g in range(bpt)],
            [vc_refs[g * ATT_BLOCK // TILE][...] for g in range(bpt)],
            [n_first + g for g in range(bpt)], length)
    for r0 in range(0, x_ref.shape[0], TILE):
        rows = slice(r0, r0 + TILE)
        x = x_ref[rows, :]
        if two_src:
            x = jnp.where(pl.program_id(0) < n_ctx_tiles, c_ref[rows, :], x)
        if has_mix:
            g0 = r0 // ATT_BLOCK
            ya = ya_all[g0:g0 + TILE // ATT_BLOCK]
            y = jnp.concatenate([jnp.concatenate(ya, axis=0), yg_ref[rows, :], yr_ref[rows, :]], axis=-1)
            x = x + mods[5:6] * _dot(y, wmix_ref[...])
        h = (_rms(x, g_ref[...]) * (1.0 + scale) + shift).astype(BF16)
        a = _dot(h, win_ref[:, :f])
        b = _dot(h, win_ref[:, f:])
        u = (jax.nn.silu(a) * b).astype(BF16)
        out = x + (0.5 * gate) * _dot(u, wout_ref[...])
        o_ref[rows, :] = _rms(out, fin_ref[...]) if has_final else out
        if has_proj:
            h = (_rms(out, gp_ref[...]) * (1.0 + mods[4:5]) + mods[3:4]).astype(BF16)
            cos, sa, sb = cos_ref[rows, :], sa_ref[rows, :], sb_ref[rows, :]
            o = 0
            for ref, rotate, scl in ((q_ref, True, LOG2E * HEAD_DIM ** -0.5), (k_ref, True, None), (v_ref, False, None),
                                     (ph_ref, False, None), (pr_ref, False, None)):
                p = _dot(h, wp_ref[:, o:o + ref.shape[1]])
                o += ref.shape[1]
                if rotate:
                    p = _rope(p, cos, sa, sb)
                if scl is not None:
                    p = p * scl
                ref[rows, :] = p.astype(ref.dtype)


def _token_block(src, mods_all, g_all, w_in_all, w_out_all, *, layer, mod0, n_ctx_tiles, tiles_per_batch,
                 mix=None, proj=None, final_g=None, latent_only=False):
    two_src = isinstance(src, tuple)
    if two_src:
        ctx2, x2 = src
        n, d = ctx2.shape[0] + x2.shape[0], x2.shape[1]
    else:
        n, d = src.shape
    skip = n_ctx_tiles if latent_only else 0
    nt = n // TOKEN_TILE - skip
    ctx_row = mods_all.shape[1] - 1
    row = lambda i: (i + skip, 0)

    def mod_row(i):
        j = i + skip
        return (layer, jnp.where(j < n_ctx_tiles, ctx_row, (j - n_ctx_tiles) // tiles_per_batch), 0, 0)

    if two_src:
        in_specs = [pl.BlockSpec((TOKEN_TILE, d), lambda i: (jnp.minimum(i, n_ctx_tiles - 1), 0)),
                    pl.BlockSpec((TOKEN_TILE, d), lambda i: (jnp.maximum(i - n_ctx_tiles, 0), 0))]
        args = [ctx2, x2]
    else:
        in_specs, args = [pl.BlockSpec((TOKEN_TILE, d), row)], [src]
    in_specs += [pl.BlockSpec((None, 1, N_MODS, d), mod_row), _layer_spec(g_all, layer),
                 _layer_spec(w_in_all, layer), _layer_spec(w_out_all, layer)]
    args += [mods_all, g_all, w_in_all, w_out_all]
    length = None
    if mix is not None:
        q, k, v, sink, yg, yr, w_mix_all, bsz, length = mix
        assert TOKEN_TILE == 2 * CTX_LEN and TILE == ATT_QBLOCKS * ATT_BLOCK and k.shape[1] == 2 * HEAD_DIM
        nblk = length // ATT_BLOCK
        blk0 = bsz * CTX_LEN // ATT_BLOCK
        bpt = TOKEN_TILE // ATT_BLOCK

        def lat(i):
            lt = i + skip - n_ctx_tiles
            return lt >= 0, jnp.where(lt >= 0, lt // tiles_per_batch, 0), jnp.where(lt >= 0, lt % tiles_per_batch, 0)

        def ctx_kv_row(h):
            def f(i):
                is_lat, b, _ = lat(i)
                return (jnp.where(is_lat, b, (TOKEN_TILE // CTX_LEN) * (i + skip) + h), 0)
            return f

        def loc_row(off):
            def f(i):
                _, b, jb = lat(i)
                return (blk0 + nblk * b + jnp.clip(jb * bpt + off, 0, nblk - 1), 0)
            return f

        cspec = [pl.BlockSpec((CTX_LEN, k.shape[1]), ctx_kv_row(h)) for h in range(2)]
        lspec = [pl.BlockSpec((ATT_BLOCK, k.shape[1]), loc_row(o)) for o in range(-1, bpt + 1)]
        in_specs += [pl.BlockSpec(memory_space=pltpu.SMEM), pl.BlockSpec((TOKEN_TILE, q.shape[1]), row)]
        in_specs += cspec + cspec + lspec + lspec
        in_specs += [pl.BlockSpec((TOKEN_TILE, yg.shape[1]), row), pl.BlockSpec((TOKEN_TILE, yr.shape[1]), row),
                     _layer_spec(w_mix_all, layer)]
        args += [sink, q, k, k, v, v] + [k] * len(lspec) + [v] * len(lspec) + [yg, yr, w_mix_all]
    out_specs = [pl.BlockSpec((TOKEN_TILE, d), lambda i: (i, 0))]
    out_shape = [jax.ShapeDtypeStruct((nt * TOKEN_TILE, d), F32)]
    if proj is not None:
        gp_all, wp_all, tabs, widths = proj

        def tab_row(i):
            return (jnp.where(i < n_ctx_tiles, 0, 1 + (i - n_ctx_tiles) % tiles_per_batch), 0)

        in_specs += [_layer_spec(gp_all, layer), _layer_spec(wp_all, layer)]
        in_specs += [pl.BlockSpec((TOKEN_TILE, tabs[0].shape[1]), tab_row)] * 3
        args += [gp_all, wp_all, *tabs]
        nq, nk, nh, nr = widths
        out_specs += [pl.BlockSpec((TOKEN_TILE, wd), lambda i: (i, 0)) for wd in (nq, nk, nk, nh, nr)]
        out_shape += [jax.ShapeDtypeStruct((n, wd), dt) for wd, dt in
                      ((nq, BF16), (nk, BF16), (nk, BF16), (nh, F32), (nr, F32))]
    if final_g is not None:
        in_specs.append(_const_spec((1, d)))
        args.append(final_g.reshape(1, d))
    return pl.pallas_call(
        functools.partial(_token_kernel, layer=layer, mod0=mod0, n_ctx_tiles=n_ctx_tiles,
                          tiles_per_batch=tiles_per_batch, skip=skip, length=length, two_src=two_src,
                          has_mix=mix is not None, has_proj=proj is not None, has_final=final_g is not None),
        grid=(nt,),
        in_specs=in_specs,
        out_specs=out_specs,
        out_shape=out_shape,
        compiler_params=_cparams(("arbitrary",)),
        name="ffn_out" if mix is not None else "ffn_in",
    )(*args)


def _seq_tile(d, i, tiles_per_batch):
    return jnp.where(d == 0, i, jnp.where(i == 0, 0, tiles_per_batch + 1 - i))


def _seq_row(b, tile, bsz, tiles_per_batch):
    return jnp.where(tile == 0, b, bsz + tiles_per_batch * b + tile - 1)


def _shift_rows(x, s, rev):
    if s == 0:
        return x
    return pltpu.roll(x, (x.shape[0] - s) if rev else s, 0)


def _hgrn_body(lbl_ref, hq_ref, fz_ref, hi_ref, hg_ref, ng_ref, jmat_ref, st_ref, upd_ref, *, layer):
    tt, w = hq_ref.shape
    c = HG_CHUNK

    logits = lbl_ref[:, 0, 0, :]
    e = jnp.exp(logits - logits.max(0, keepdims=True))
    lbp = e / e.sum(0, keepdims=True)
    lb = jnp.zeros((1, w), F32)
    for j in range(1, layer + 1):
        lb = lb + lbp[j:j + 1]

    q = jax.nn.silu(hq_ref[...])
    f = lb + (1.0 - lb) * jax.nn.sigmoid(fz_ref[...])
    kk = 1.0 - f
    v = hi_ref[...]
    jmat = jmat_ref[...]
    pos = lax.broadcasted_iota(jnp.int32, (tt, w), 0) % c

    def run(rev, factored):
        def in_chunk(s, r):
            return (pos <= c - 1 - s) if r else (pos >= s)

        def cumprod(y, r):
            s = 1
            while s < c:
                y = y * jnp.where(in_chunk(s, r), _shift_rows(y, s, r), 1.0)
                s *= 2
            return y

        p_in = cumprod(f, rev)
        p_out = cumprod(jnp.where(in_chunk(1, not rev), _shift_rows(f, 1, not rev), 1.0), not rev)
        half = c // 2
        nch = tt // c
        nhead = w // HG_DK
        lane_head = lax.broadcasted_iota(jnp.int32, (c, w), 1) // HG_DK

        def pairs_factored():
            x3 = (q * p_in).astype(BF16).reshape(nch, c, w)
            zero = jnp.zeros_like(x3)
            q_hat = jnp.concatenate([jnp.where(lane_head == h, x3, zero) for h in range(nhead)], axis=1)
            k_div = (kk / p_in).astype(BF16).reshape(nch, c, w)
            v3 = v.astype(BF16).reshape(nch, c, w)
            t_row = lax.broadcasted_iota(jnp.int32, (nhead * c, c), 0) % c
            s_col = lax.broadcasted_iota(jnp.int32, (nhead * c, c), 1)
            causal = (s_col >= t_row) if rev else (s_col <= t_row)
            outs = []
            for ci in range(nch):
                att = jnp.where(causal, _dot_nt(q_hat[ci], k_div[ci]), 0.0)
                oc = _dot(att.astype(BF16), v3[ci])
                outs.append(sum(jnp.where(lane_head == h, oc[h * c:(h + 1) * c], 0.0) for h in range(nhead)))
            return jnp.concatenate(outs, axis=0)

        def pairs_direct():
            def halves(x):
                x4 = x.reshape(nch, 2, half, w)
                return (x4[:, 1], x4[:, 0]) if rev else (x4[:, 0], x4[:, 1])

            r8 = lax.broadcasted_iota(jnp.int32, (nch, half, w), 1)

            def rot(x, s):
                return x if s == 0 else pltpu.roll(x, (half - s) if rev else s, 1)

            def wrapped(s):
                return (r8 > half - 1 - s) if rev else (r8 < s)

            def pair_sum(wgt, vs):
                return (_dot(wgt.reshape(nch * half, w).astype(BF16), jmat) * vs.reshape(nch * half, w))

            (f_a, f_b), (v_a, v_b), (qd_a, qd_b) = halves(f), halves(v), halves(q)
            o_a = jnp.zeros((nch * half, w), F32)
            o_b = jnp.zeros((nch * half, w), F32)
            for delta in range(c):
                s = delta % half
                fr, vr = rot(f_a, s), rot(v_a, s)
                if delta < half:
                    fs_a = jnp.where(wrapped(s), 1.0, fr)
                    nxt_a = qd_a * fs_a
                    o_a = o_a + pair_sum(qd_a - nxt_a, vr)
                    qd_a = nxt_a
                    fs_b = jnp.where(wrapped(s), fr, rot(f_b, s))
                    vs_b = jnp.where(wrapped(s), vr, rot(v_b, s))
                else:
                    fs_b = jnp.where(wrapped(s), 1.0, fr)
                    vs_b = vr
                nxt_b = qd_b * fs_b
                o_b = o_b + pair_sum(qd_b - nxt_b, vs_b)
                qd_b = nxt_b
            o_a, o_b = o_a.reshape(nch, half, w), o_b.reshape(nch, half, w)
            return jnp.stack([o_b, o_a] if rev else [o_a, o_b], axis=1).reshape(tt, w)

        o = pairs_factored() if factored else pairs_direct()
        p3 = p_in.reshape(nch, c, w)
        dec_end = p3[:, 0:1, :] if rev else p3[:, c - 1:c, :]
        q_in = (q * p_in).astype(BF16)
        lane3 = lax.broadcasted_iota(jnp.int32, (nch, c, w), 2) % (2 * HG_DK)

        def per_head(x):
            x3 = x.astype(BF16).reshape(nch, c, w)
            zero = jnp.zeros_like(x3)
            return jnp.concatenate([jnp.where(lane3 < HG_DK, x3, zero), jnp.where(lane3 >= HG_DK, x3, zero)], axis=1)

        v_hat, k_hat = per_head(v), per_head(kk * p_out)
        ntile = st_ref.shape[0]
        lt = w // ntile
        for ci in range(nch):
            for j in range(ntile):
                upd_ref[ci, j] = _dot_tn(v_hat[ci, :, j * lt:(j + 1) * lt], k_hat[ci, :, j * lt:(j + 1) * lt])
        st = [st_ref[j] for j in range(ntile)]
        parts = [None] * nch
        for ci in (range(nch - 1, -1, -1) if rev else range(nch)):
            parts[ci] = jnp.concatenate([_dot_nt(q_in[ci * c:(ci + 1) * c, j * lt:(j + 1) * lt], st[j].astype(BF16))
                                         for j in range(ntile)], axis=1)
            for j in range(ntile):
                st[j] = st[j] * dec_end[ci][:, j * lt:(j + 1) * lt] + upd_ref[ci, j]
        for j in range(ntile):
            st_ref[j] = st[j]
        return o + jnp.concatenate(parts, axis=0)

    def finish(tot):
        sq = tot * tot
        hi = sq.astype(BF16)
        lo = (sq - hi.astype(F32)).astype(BF16)
        ms = (_dot(hi, jmat) + _dot(lo, jmat)) * (1.0 / HG_DK)
        y = tot * lax.rsqrt(ms + EPS)
        return (y * ng_ref[...] * jax.nn.silu(hg_ref[...])).astype(BF16)

    def decay_ok(fz):
        fn = lb + (1.0 - lb) * jax.nn.sigmoid(fz)
        chunk_log = jnp.log(fn).reshape(tt // c, c, w).sum(axis=1)
        return jnp.min(chunk_log) >= math.log(HG_MIN_DECAY)

    return run, finish, decay_ok


def _rglru_body(x_ref, gate_ref, prev_ref, next_ref, cw_ref, cb_ref, wa_ref, ba_ref, wx_ref, bx_ref, lam_ref, h_ref,
                *, tile, tiles_per_batch):
    tt, w = x_ref.shape

    x = x_ref[...]
    has_prev = (tile >= 2).astype(F32)
    has_next = jnp.logical_and(tile >= 1, tile < tiles_per_batch).astype(F32)
    prev = prev_ref[...] * has_prev
    nxt = next_ref[...] * has_next
    row = lax.broadcasted_iota(jnp.int32, (tt, w), 0)
    xm1 = jnp.where(row == 0, prev[HALO - 1:HALO], _shift_rows(x, 1, False))
    xm2 = jnp.where(row == 0, prev[HALO - 2:HALO - 1],
                    jnp.where(row == 1, prev[HALO - 1:HALO], _shift_rows(x, 2, False)))
    xp1 = jnp.where(row == tt - 1, nxt[0:1], _shift_rows(x, 1, True))
    cw = cw_ref[...]
    u = cb_ref[...] + xm2 * cw[0:1] + xm1 * cw[1:2] + x * cw[2:3] + xp1 * cw[3:4]

    u16 = u.astype(BF16)
    r = jax.nn.sigmoid(_dot(u16, wa_ref[0]) + ba_ref[0])
    ig = jax.nn.sigmoid(_dot(u16, wx_ref[0]) + bx_ref[0])
    log_a = -RG_C * r * jax.nn.softplus(-lam_ref[0])
    a = jnp.exp(log_a)
    z = jnp.sqrt(jnp.tanh(-log_a) * (1.0 + a * a)) * (ig * u)

    def scan(rev):
        ng = tt // SUBLANES
        aa, zz = a.reshape(ng, SUBLANES, w), z.reshape(ng, SUBLANES, w)
        r8 = lax.broadcasted_iota(jnp.int32, (ng, SUBLANES, w), 1)
        s = 1
        while s < SUBLANES:
            ok = (r8 <= SUBLANES - 1 - s) if rev else (r8 >= s)
            sh = (SUBLANES - s) if rev else s
            zz = zz + jnp.where(ok, aa * pltpu.roll(zz, sh, 1), 0.0)
            aa = aa * jnp.where(ok, pltpu.roll(aa, sh, 1), 1.0)
            s *= 2
        carry = h_ref[0:1, :]
        groups = [None] * ng
        for gi in (range(ng - 1, -1, -1) if rev else range(ng)):
            groups[gi] = aa[gi] * carry + zz[gi]
            carry = groups[gi][0:1] if rev else groups[gi][SUBLANES - 1:SUBLANES]
        h_ref[...] = jnp.broadcast_to(carry, h_ref.shape)
        return jnp.concatenate(groups, axis=0)

    def finish(h):
        return (jax.nn.gelu(gate_ref[...]) * h).astype(BF16)

    return scan, finish


def _mixers_kernel(lbl_ref, hq_ref, fz_ref, hi_ref, hg_ref, ng_ref, jmat_ref,
                   x_ref, gate_ref, prev_ref, next_ref, cw_ref, cb_ref, wa_ref, ba_ref, wx_ref, bx_ref, lam_ref, fzn_ref,
                   yg_ref, yr_ref, st_ref, ofwd_ref, upd_ref, h_ref, hfwd_ref, ok_ref, *, layer, tiles_per_batch):
    d = pl.program_id(1)
    i = pl.program_id(2)
    tile = _seq_tile(d, i, tiles_per_batch)

    @pl.when(i == 0)
    def _():
        st_ref[...] = jnp.zeros_like(st_ref)
        h_ref[...] = jnp.zeros_like(h_ref)
        ok_ref[0] = 0

    factored = ok_ref[0] == 1
    run, finish_hg, decay_ok = _hgrn_body(lbl_ref, hq_ref, fz_ref, hi_ref, hg_ref, ng_ref, jmat_ref, st_ref, upd_ref,
                                          layer=layer)
    scan, finish_rg = _rglru_body(x_ref, gate_ref, prev_ref, next_ref, cw_ref, cb_ref, wa_ref, ba_ref, wx_ref, bx_ref,
                                  lam_ref, h_ref, tile=tile, tiles_per_batch=tiles_per_batch)

    for fac in (True, False):
        chosen = factored if fac else jnp.logical_not(factored)

        @pl.when(jnp.logical_and(d == 0, chosen))
        def _():
            ofwd_ref[tile] = run(False, fac)
            hfwd_ref[tile] = scan(False)

        @pl.when(jnp.logical_and(d == 1, chosen))
        def _():
            yg_ref[...] = finish_hg(run(True, fac) + ofwd_ref[tile])
            yr_ref[...] = finish_rg(scan(True) + hfwd_ref[tile])

    ok_ref[0] = decay_ok(fzn_ref[...]).astype(jnp.int32)


def _mixers(ph, pr, lb_logits, norm_g_all, conv_w, conv_b, wa, ba, wx, bx, lam, *, layer, bsz, tiles_per_batch):
    n = ph.shape[0]
    depth = lb_logits.shape[0]
    hw = norm_g_all.shape[-1]
    rw = conv_b.shape[-1]
    head = jnp.arange(hw) // HG_DK
    same = head[:, None] == head[None, :]
    lt = 2 * HG_DK
    ntile = hw // lt
    hpt = TILE // HALO

    def seq_row(b, d, i):
        return _seq_row(b, _seq_tile(d, i, tiles_per_batch), bsz, tiles_per_batch)

    def col(cfn):
        return lambda b, d, i: (seq_row(b, d, i), cfn(d))

    def out_row(b, d, i):
        tile = jnp.where(d == 0, 0, _seq_tile(d, i, tiles_per_batch))
        return (_seq_row(b, tile, bsz, tiles_per_batch), 0)

    per_dir = lambda b, d, i: (layer, d, 0, 0)
    return pl.pallas_call(
        functools.partial(_mixers_kernel, layer=layer, tiles_per_batch=tiles_per_batch),
        grid=(bsz, 2, tiles_per_batch + 1),
        in_specs=[
            pl.BlockSpec((depth, 1, 1, hw), lambda b, d, i: (0, d, 0, 0)),
            pl.BlockSpec((TILE, hw), col(lambda d: 0)),
            pl.BlockSpec((TILE, hw), col(lambda d: 1 + d)),
            pl.BlockSpec((TILE, hw), col(lambda d: 3)),
            pl.BlockSpec((TILE, hw), col(lambda d: 4)),
            _layer_spec(norm_g_all, layer), _const_spec((hw, hw)),
            pl.BlockSpec((TILE, rw), col(lambda d: 0)),
            pl.BlockSpec((TILE, rw), col(lambda d: 1)),
            pl.BlockSpec((HALO, rw), lambda b, d, i: (jnp.maximum(seq_row(b, d, i) * hpt - 1, 0), 0)),
            pl.BlockSpec((HALO, rw), lambda b, d, i: (jnp.minimum((seq_row(b, d, i) + 1) * hpt, n // HALO - 1), 0)),
            _layer_spec(conv_w, layer), _layer_spec(conv_b, layer),
            pl.BlockSpec((None, 1, rw, rw), per_dir), pl.BlockSpec((None, 1, 1, rw), per_dir),
            pl.BlockSpec((None, 1, rw, rw), per_dir), pl.BlockSpec((None, 1, 1, rw), per_dir),
            pl.BlockSpec((None, 1, 1, rw), per_dir),
            pl.BlockSpec((TILE, hw), lambda b, d, i: (seq_row(b, d, jnp.minimum(i + 1, tiles_per_batch)), 1 + d)),
        ],
        out_specs=[pl.BlockSpec((TILE, hw), out_row), pl.BlockSpec((TILE, rw), out_row)],
        out_shape=[jax.ShapeDtypeStruct((n, hw), BF16), jax.ShapeDtypeStruct((n, rw), BF16)],
        scratch_shapes=[pltpu.VMEM((ntile, lt, lt), F32), pltpu.VMEM((tiles_per_batch + 1, TILE, hw), F32),
                        pltpu.VMEM((TILE // HG_CHUNK, ntile, lt, lt), F32),
                        pltpu.VMEM((SUBLANES, rw), F32), pltpu.VMEM((tiles_per_batch + 1, TILE, rw), F32),
                        pltpu.SMEM((1,), jnp.int32)],
        compiler_params=_cparams(("arbitrary", "arbitrary", "arbitrary")),
        name="mixers",
    )(lb_logits.reshape(depth, 2, 1, hw), ph, ph, ph, ph, norm_g_all, same.astype(BF16),
      pr, pr, pr, pr, conv_w, conv_b, wa, ba, wx, bx, lam, ph)


def _rope_tables(length):
    n_freq = HEAD_DIM // 4
    rows = length // GRID_W
    row = jnp.repeat(jnp.arange(rows), GRID_W)
    col = jnp.tile(jnp.arange(GRID_W), rows)
    inv_freq = ROPE_BASE ** (-jnp.arange(n_freq, dtype=F32) / n_freq)
    ang = jnp.stack([row[:, None] * inv_freq, col[:, None] * inv_freq], axis=1)
    cos, sin = jnp.cos(ang), jnp.sin(ang)
    zero = jnp.zeros_like(sin[:, 0])
    cos_h = jnp.concatenate([cos[:, 0], cos[:, 0], cos[:, 1], cos[:, 1]], axis=-1)
    sa_h = jnp.concatenate([-sin[:, 0], zero, -sin[:, 1], zero], axis=-1)
    sb_h = jnp.concatenate([zero, sin[:, 0], zero, sin[:, 1]], axis=-1)
    ctx = (jnp.ones((TOKEN_TILE, HEAD_DIM), F32), jnp.zeros((TOKEN_TILE, HEAD_DIM), F32), jnp.zeros((TOKEN_TILE, HEAD_DIM), F32))
    return tuple(jnp.tile(jnp.concatenate([c, t], axis=0), (1, 2)) for c, t in zip(ctx, (cos_h, sa_h, sb_h)))


def _block_diag(wb):
    nb, bw = wb.shape[-3], wb.shape[-1]
    out = jnp.zeros(wb.shape[:-3] + (nb * bw, nb * bw), wb.dtype)
    for j in range(nb):
        out = out.at[..., j * bw:(j + 1) * bw, j * bw:(j + 1) * bw].set(wb[..., j, :, :])
    return out


def _regroup_heads(t, axis):
    shp = t.shape
    t = t.reshape(shp[:axis] + (ATT_KV_HEADS, ATT_GROUP, HEAD_DIM) + shp[axis + 1:])
    return jnp.swapaxes(t, axis, axis + 1).reshape(shp)


def kernel(x, c, ctx, c_ctx, ada_w, ada_b, norm_ffn1, ffn1_w_in, ffn1_w_out, norm_mix, w_in, w_out, attn_sink, hg_lb_logits, hg_norm, rg_conv_w, rg_conv_b, rg_gate_a_w, rg_gate_a_b, rg_gate_x_w, rg_gate_x_b, rg_lambda, norm_ffn2, ffn2_w_in, ffn2_w_out, final_norm):
    bsz, length, d = x.shape
    depth = ada_w.shape[0]
    assert ctx.shape[1] == CTX_LEN == TILE and length % TOKEN_TILE == 0 and (bsz * CTX_LEN) % TOKEN_TILE == 0
    tpb = length // TILE
    geo = dict(n_ctx_tiles=bsz * CTX_LEN // TOKEN_TILE, tiles_per_batch=length // TOKEN_TILE)

    ctx2, x2 = ctx.reshape(bsz * CTX_LEN, d), x.reshape(bsz * length, d)
    cond = jnp.zeros((SUBLANES, d), F32).at[:bsz].set(c).at[bsz].set(c_ctx)
    mods_all = _adaln(cond, ada_w, ada_b).reshape(depth, SUBLANES, N_MODS, d)[:, :bsz + 1]
    tabs = _rope_tables(length)

    att_w = w_out.shape[1] // 2
    kv_w = ATT_KV_HEADS * HEAD_DIM
    hg_w = hg_norm.shape[1]
    rg_w = rg_conv_b.shape[1]
    widths = (att_w, kv_w, 5 * hg_w, 2 * rg_w)
    assert sum(widths) + kv_w == w_in.shape[2]

    to16 = lambda t: t.astype(BF16)
    f1_in, f1_out, f2_in, f2_out = map(to16, (ffn1_w_in, ffn1_w_out, ffn2_w_in, ffn2_w_out))
    wi16 = to16(jnp.concatenate([_regroup_heads(w_in[:, :, :att_w], 2), w_in[:, :, att_w:]], axis=2))
    wo16 = to16(jnp.concatenate([_regroup_heads(w_out[:, :att_w], 1), w_out[:, att_w:]], axis=1))
    wa_bd, wx_bd = to16(_block_diag(rg_gate_a_w)), to16(_block_diag(rg_gate_x_w))
    as_row = lambda t: t.reshape(t.shape[:-1] + (1, t.shape[-1]))
    g1, gm, g2, hgn = map(as_row, (norm_ffn1, norm_mix, norm_ffn2, hg_norm))
    conv_b, ba, bx, lam = map(as_row, (rg_conv_b, rg_gate_a_b, rg_gate_x_b, rg_lambda))

    xs = (ctx2, x2)
    for l in range(depth):
        last = l == depth - 1
        xs, q, k, v, ph, pr = _token_block(xs, mods_all, g1, f1_in, f1_out, layer=l, mod0=0,
                                           proj=(gm, wi16, tabs, widths), **geo)
        yg, yr = _mixers(ph, pr, hg_lb_logits, hgn, rg_conv_w, conv_b, wa_bd, ba, wx_bd, bx, lam,
                         layer=l, bsz=bsz, tiles_per_batch=tpb)
        xs, = _token_block(xs, mods_all, g2, f2_in, f2_out, layer=l, mod0=6,
                           mix=(q, k, v, attn_sink, yg, yr, wo16, bsz, length),
                           final_g=final_norm if last else None, latent_only=last, **geo)
    return xs.reshape(bsz, length, d)
```

```python
import functools
import math

import jax
import jax.numpy as jnp
from jax import lax
from jax.experimental import pallas as pl
from jax.experimental.pallas import tpu as pltpu

F32 = jnp.float32
BF16 = jnp.bfloat16

GRID_W = 64
CTX_LEN = 256
N_MODS = 9
EPS = 1e-6
HEAD_DIM = 64
ATT_KV_HEADS = 2
ATT_GROUP = 4
WINDOW = 128
ATT_BLOCK = 128
ATT_QBLOCKS = 2
ROPE_BASE = 10000.0
LOG2E = 1.4426950408889634
HG_HEADS = 4
HG_DK = 64
HG_CHUNK = 16
HG_FAST_CHUNK = 32
HG_MIN_DECAY = 1e-30
RG_BLOCKS = 4
RG_CONV = 4
RG_C = 8.0

TILE = 256
TOKEN_TILE = 512
SUBLANES = 8
HALO = SUBLANES
VMEM_LIMIT = 52 * 1024 * 1024


def _cparams(sem):
    return pltpu.CompilerParams(dimension_semantics=sem, vmem_limit_bytes=VMEM_LIMIT)


def _const_spec(shape):
    nd = len(shape)
    return pl.BlockSpec(shape, lambda *_: (0,) * nd, pipeline_mode=pl.Buffered(1))


def _layer_spec(arr, l):
    nd = arr.ndim
    return pl.BlockSpec((None,) + arr.shape[1:], lambda *_: (l,) + (0,) * (nd - 1), pipeline_mode=pl.Buffered(1))


def _rms(xf, g):
    return xf * lax.rsqrt(jnp.mean(xf * xf, axis=-1, keepdims=True) + EPS) * g


def _dot(a, b):
    return jnp.dot(a, b, preferred_element_type=F32)


def _dot_nt(a, b):
    return lax.dot_general(a, b, (((1,), (1,)), ((), ())), preferred_element_type=F32)


def _dot_tn(a, b):
    return lax.dot_general(a, b, (((0,), (0,)), ((), ())), preferred_element_type=F32)


def _adaln_kernel(c_ref, w_ref, b_ref, o_ref):
    s = jax.nn.silu(c_ref[...]).astype(BF16)
    o_ref[0] = _dot(s, w_ref[0].astype(BF16)) + b_ref[0]


def _adaln(cond, ada_w, ada_b):
    depth, d, n = ada_w.shape
    tn = n // 8
    return pl.pallas_call(
        _adaln_kernel,
        grid=(depth, n // tn),
        in_specs=[
            pl.BlockSpec(cond.shape, lambda l, j: (0, 0)),
            pl.BlockSpec((1, d, tn), lambda l, j: (l, 0, j)),
            pl.BlockSpec((1, 1, tn), lambda l, j: (l, 0, j)),
        ],
        out_specs=pl.BlockSpec((1, cond.shape[0], tn), lambda l, j: (l, 0, j)),
        out_shape=jax.ShapeDtypeStruct((depth, cond.shape[0], n), F32),
        compiler_params=_cparams(("arbitrary", "arbitrary")),
        name="adaln",
    )(cond, ada_w, ada_b.reshape(depth, 1, n))


def _attend(sink_ref, layer, q_blocks, k_loc, v_loc, k_ctx, v_ctx, n_blocks, length):
    sw = k_ctx[0].shape[1]
    nslab = q_blocks[0].shape[1] // sw

    def only(t, kv, fill=0.0):
        lane = lax.broadcasted_iota(jnp.int32, t.shape, 1)
        return jnp.where((lane >= kv * HEAD_DIM) & (lane < (kv + 1) * HEAD_DIM), t, jnp.full_like(t, fill))

    nq = len(q_blocks)
    chains = [(j, kv) for j in range(nq) for kv in range(ATT_KV_HEADS)]
    q, valid = [], []
    for j in range(nq):
        q.append(jnp.concatenate([q_blocks[j][:, g * sw:(g + 1) * sw] for g in range(nslab)], axis=0))
        n = n_blocks[j]
        k_pos = (n - 1) * ATT_BLOCK + lax.broadcasted_iota(jnp.int32, (3 * ATT_BLOCK, ATT_BLOCK), 0)
        q_pos = n * ATT_BLOCK + lax.broadcasted_iota(jnp.int32, (3 * ATT_BLOCK, ATT_BLOCK), 1)
        ok = (jnp.abs(k_pos - q_pos) <= WINDOW) & (k_pos >= 0) & (k_pos < length) & (n >= 0)
        valid.append(jnp.concatenate([ok] * nslab, axis=1))
    sink = {kv: jnp.concatenate([jnp.full((1, ATT_BLOCK), sink_ref[layer, kv * ATT_GROUP + g] * LOG2E, F32)
                                 for g in range(nslab)], axis=1) for kv in range(ATT_KV_HEADS)}
    s_loc = {c: jnp.where(valid[c[0]], _dot_nt(only(k_loc[c[0]], c[1]), q[c[0]]), -jnp.inf) for c in chains}
    s_ctx = {c: _dot_nt(only(k_ctx[c[0]], c[1]), q[c[0]]) for c in chains}
    m = {c: jnp.maximum(jnp.maximum(s_loc[c].max(0, keepdims=True), s_ctx[c].max(0, keepdims=True)), sink[c[1]])
         for c in chains}
    p_loc = {c: jnp.exp2(s_loc[c] - m[c]).astype(BF16) for c in chains}
    p_ctx = {c: jnp.exp2(s_ctx[c] - m[c]).astype(BF16) for c in chains}
    o = {c: (_dot_tn(only(v_loc[c[0]], c[1], 1.0), p_loc[c]) + _dot_tn(only(v_ctx[c[0]], c[1], 1.0), p_ctx[c]))
         for c in chains}
    row_kv = lax.broadcasted_iota(jnp.int32, (sw, nslab * ATT_BLOCK), 0) // HEAD_DIM
    outs = []
    for j in range(nq):
        scaled = []
        for kv in range(ATT_KV_HEADS):
            c = (j, kv)
            other = (1 - kv) * HEAD_DIM
            denom = o[c][other:other + 1, :] + jnp.exp2(sink[kv] - m[c])
            scaled.append(o[c] / denom)
        out = jnp.where(row_kv == 0, scaled[0], scaled[1]).T
        outs.append(jnp.concatenate([out[g * ATT_BLOCK:(g + 1) * ATT_BLOCK] for g in range(nslab)],
                                    axis=1).astype(BF16))
    return outs


def _rope(t, cos, sa, sb):
    w = t.shape[1]
    rep = w // cos.shape[1]
    c, a, b = (jnp.concatenate([m] * rep, axis=1) if rep > 1 else m for m in (cos, sa, sb))
    half = HEAD_DIM // 4
    return t * c + pltpu.roll(t, w - half, 1) * a + pltpu.roll(t, half, 1) * b


def _token_kernel(*refs, layer, mod0, n_ctx_tiles, tiles_per_batch, skip, length, two_src, has_mix, has_proj,
                  has_final):
    it = iter(refs)
    if two_src:
        c_ref = next(it)
    x_ref, mods_ref, g_ref, win_ref, wout_ref = (next(it) for _ in range(5))
    if has_mix:
        nloc = TOKEN_TILE // ATT_BLOCK + 2
        sink_ref, q_in_ref = next(it), next(it)
        kc_refs, vc_refs = [next(it) for _ in range(2)], [next(it) for _ in range(2)]
        kl_refs, vl_refs = [next(it) for _ in range(nloc)], [next(it) for _ in range(nloc)]
        yg_ref, yr_ref, wmix_ref = (next(it) for _ in range(3))
    if has_proj:
        gp_ref, wp_ref, cos_ref, sa_ref, sb_ref = (next(it) for _ in range(5))
    if has_final:
        fin_ref = next(it)
    o_ref = next(it)
    if has_proj:
        q_ref, k_ref, v_ref, ph_ref, pr_ref = (next(it) for _ in range(5))

    mods = mods_ref[0]
    shift, scale, gate = mods[mod0:mod0 + 1], mods[mod0 + 1:mod0 + 2], mods[mod0 + 2:mod0 + 3]
    f = wout_ref.shape[0]
    ya_all = None
    if has_mix:
        lt = pl.program_id(0) + skip - n_ctx_tiles
        bpt = TOKEN_TILE // ATT_BLOCK
        n_first = jnp.where(lt >= 0, (lt % tiles_per_batch) * bpt, -bpt)
        ya_all = _attend(
            sink_ref, layer,
            [q_in_ref[g * ATT_BLOCK:(g + 1) * ATT_BLOCK, :] for g in range(bpt)],
            [jnp.concatenate([r[...] for r in kl_refs[g:g + 3]], axis=0) for g in range(bpt)],
            [jnp.concatenate([r[...] for r in vl_refs[g:g + 3]], axis=0) for g in range(bpt)],
            [kc_refs[g * ATT_BLOCK // TILE][...] for g in range(bpt)],
            [vc_refs[g * ATT_BLOCK // TILE][...] for g in range(bpt)],
            [n_first + g for g in range(bpt)], length)
    for r0 in range(0, x_ref.shape[0], TILE):
        rows = slice(r0, r0 + TILE)
        x = x_ref[rows, :]
        if two_src:
            x = jnp.where(pl.program_id(0) < n_ctx_tiles, c_ref[rows, :], x)
        if has_mix:
            g0 = r0 // ATT_BLOCK
            ya = ya_all[g0:g0 + TILE // ATT_BLOCK]
            y = jnp.concatenate([jnp.concatenate(ya, axis=0), yg_ref[rows, :], yr_ref[rows, :]], axis=-1)
            x = x + mods[5:6] * _dot(y, wmix_ref[...])
        h = (_rms(x, g_ref[...]) * (1.0 + scale) + shift).astype(BF16)
        a = _dot(h, win_ref[:, :f])
        b = _dot(h, win_ref[:, f:])
        u = (jax.nn.silu(a) * b).astype(BF16)
        out = x + (0.5 * gate) * _dot(u, wout_ref[...])
        o_ref[rows, :] = _rms(out, fin_ref[...]) if has_final else out
        if has_proj:
            h = (_rms(out, gp_ref[...]) * (1.0 + mods[4:5]) + mods[3:4]).astype(BF16)
            cos, sa, sb = cos_ref[rows, :], sa_ref[rows, :], sb_ref[rows, :]
            o = 0
            for ref, rotate, scl in ((q_ref, True, LOG2E * HEAD_DIM ** -0.5), (k_ref, True, None), (v_ref, False, None),
                                     (ph_ref, False, None), (pr_ref, False, None)):
                p = _dot(h, wp_ref[:, o:o + ref.shape[1]])
                o += ref.shape[1]
                if rotate:
                    p = _rope(p, cos, sa, sb)
                if scl is not None:
                    p = p * scl
                ref[rows, :] = p.astype(ref.dtype)


def _token_block(src, mods_all, g_all, w_in_all, w_out_all, *, layer, mod0, n_ctx_tiles, tiles_per_batch,
                 mix=None, proj=None, final_g=None, latent_only=False):
    two_src = isinstance(src, tuple)
    if two_src:
        ctx2, x2 = src
        n, d = ctx2.shape[0] + x2.shape[0], x2.shape[1]
    else:
        n, d = src.shape
    skip = n_ctx_tiles if latent_only else 0
    nt = n // TOKEN_TILE - skip
    ctx_row = mods_all.shape[1] - 1
    row = lambda i: (i + skip, 0)

    def mod_row(i):
        j = i + skip
        return (layer, jnp.where(j < n_ctx_tiles, ctx_row, (j - n_ctx_tiles) // tiles_per_batch), 0, 0)

    if two_src:
        in_specs = [pl.BlockSpec((TOKEN_TILE, d), lambda i: (jnp.minimum(i, n_ctx_tiles - 1), 0)),
                    pl.BlockSpec((TOKEN_TILE, d), lambda i: (jnp.maximum(i - n_ctx_tiles, 0), 0))]
        args = [ctx2, x2]
    else:
        in_specs, args = [pl.BlockSpec((TOKEN_TILE, d), row)], [src]
    in_specs += [pl.BlockSpec((None, 1, N_MODS, d), mod_row), _layer_spec(g_all, layer),
                 _layer_spec(w_in_all, layer), _layer_spec(w_out_all, layer)]
    args += [mods_all, g_all, w_in_all, w_out_all]
    length = None
    if mix is not None:
        q, k, v, sink, yg, yr, w_mix_all, bsz, length = mix
        assert TOKEN_TILE == 2 * CTX_LEN and TILE == ATT_QBLOCKS * ATT_BLOCK and k.shape[1] == 2 * HEAD_DIM
        nblk = length // ATT_BLOCK
        blk0 = bsz * CTX_LEN // ATT_BLOCK
        bpt = TOKEN_TILE // ATT_BLOCK

        def lat(i):
            lt = i + skip - n_ctx_tiles
            return lt >= 0, jnp.where(lt >= 0, lt // tiles_per_batch, 0), jnp.where(lt >= 0, lt % tiles_per_batch, 0)

        def ctx_kv_row(h):
            def f(i):
                is_lat, b, _ = lat(i)
                return (jnp.where(is_lat, b, (TOKEN_TILE // CTX_LEN) * (i + skip) + h), 0)
            return f

        def loc_row(off):
            def f(i):
                _, b, jb = lat(i)
                return (blk0 + nblk * b + jnp.clip(jb * bpt + off, 0, nblk - 1), 0)
            return f

        cspec = [pl.BlockSpec((CTX_LEN, k.shape[1]), ctx_kv_row(h)) for h in range(2)]
        lspec = [pl.BlockSpec((ATT_BLOCK, k.shape[1]), loc_row(o)) for o in range(-1, bpt + 1)]
        in_specs += [pl.BlockSpec(memory_space=pltpu.SMEM), pl.BlockSpec((TOKEN_TILE, q.shape[1]), row)]
        in_specs += cspec + cspec + lspec + lspec
        in_specs += [pl.BlockSpec((TOKEN_TILE, yg.shape[1]), row), pl.BlockSpec((TOKEN_TILE, yr.shape[1]), row),
                     _layer_spec(w_mix_all, layer)]
        args += [sink, q, k, k, v, v] + [k] * len(lspec) + [v] * len(lspec) + [yg, yr, w_mix_all]
    out_specs = [pl.BlockSpec((TOKEN_TILE, d), lambda i: (i, 0))]
    out_shape = [jax.ShapeDtypeStruct((nt * TOKEN_TILE, d), F32)]
    if proj is not None:
        gp_all, wp_all, tabs, widths = proj

        def tab_row(i):
            return (jnp.where(i < n_ctx_tiles, 0, 1 + (i - n_ctx_tiles) % tiles_per_batch), 0)

        in_specs += [_layer_spec(gp_all, layer), _layer_spec(wp_all, layer)]
        in_specs += [pl.BlockSpec((TOKEN_TILE, tabs[0].shape[1]), tab_row)] * 3
        args += [gp_all, wp_all, *tabs]
        nq, nk, nh, nr = widths
        out_specs += [pl.BlockSpec((TOKEN_TILE, wd), lambda i: (i, 0)) for wd in (nq, nk, nk, nh, nr)]
        out_shape += [jax.ShapeDtypeStruct((n, wd), dt) for wd, dt in
                      ((nq, BF16), (nk, BF16), (nk, BF16), (nh, F32), (nr, F32))]
    if final_g is not None:
        in_specs.append(_const_spec((1, d)))
        args.append(final_g.reshape(1, d))
    return pl.pallas_call(
        functools.partial(_token_kernel, layer=layer, mod0=mod0, n_ctx_tiles=n_ctx_tiles,
                          tiles_per_batch=tiles_per_batch, skip=skip, length=length, two_src=two_src,
                          has_mix=mix is not None, has_proj=proj is not None, has_final=final_g is not None),
        grid=(nt,),
        in_specs=in_specs,
        out_specs=out_specs,
        out_shape=out_shape,
        compiler_params=_cparams(("arbitrary",)),
        name="ffn_out" if mix is not None else "ffn_in",
    )(*args)


def _seq_tile(d, i, tiles_per_batch):
    return jnp.where(d == 0, i, jnp.where(i == 0, 0, tiles_per_batch + 1 - i))


def _seq_row(b, tile, bsz, tiles_per_batch):
    return jnp.where(tile == 0, b, bsz + tiles_per_batch * b + tile - 1)


def _shift_rows(x, s, rev):
    if s == 0:
        return x
    return pltpu.roll(x, (x.shape[0] - s) if rev else s, 0)


def _hgrn_body(lbl_ref, hq_ref, fz_ref, hi_ref, hg_ref, ng_ref, jmat_ref, st_ref, upd_ref, *, layer):
    tt, w = hq_ref.shape

    logits = lbl_ref[:, 0, 0, :]
    e = jnp.exp(logits - logits.max(0, keepdims=True))
    lbp = e / e.sum(0, keepdims=True)
    lb = jnp.zeros((1, w), F32)
    for j in range(1, layer + 1):
        lb = lb + lbp[j:j + 1]

    q = jax.nn.silu(hq_ref[...])
    f = lb + (1.0 - lb) * jax.nn.sigmoid(fz_ref[...])
    kk = 1.0 - f
    v = hi_ref[...]
    jmat = jmat_ref[...]

    def run(rev, factored):
        c = HG_FAST_CHUNK if factored else HG_CHUNK
        pos = lax.broadcasted_iota(jnp.int32, (tt, w), 0) % c

        def in_chunk(s, r):
            return (pos <= c - 1 - s) if r else (pos >= s)

        def cumprod(y, r):
            s = 1
            while s < c:
                y = y * jnp.where(in_chunk(s, r), _shift_rows(y, s, r), 1.0)
                s *= 2
            return y

        p_in = cumprod(f, rev)
        half = c // 2
        nch = tt // c
        p3 = p_in.reshape(nch, c, w)
        dec_end = p3[:, 0:1, :] if rev else p3[:, c - 1:c, :]
        if factored:
            k_div = kk / p_in
            k_out = (k_div.reshape(nch, c, w) * dec_end).reshape(tt, w)
        else:
            k_out = kk * cumprod(jnp.where(in_chunk(1, not rev), _shift_rows(f, 1, not rev), 1.0), not rev)
        nhead = w // HG_DK
        lane_head = lax.broadcasted_iota(jnp.int32, (c, w), 1) // HG_DK

        def pairs_factored():
            x3 = (q * p_in).astype(BF16).reshape(nch, c, w)
            zero = jnp.zeros_like(x3)
            q_hat = jnp.concatenate([jnp.where(lane_head == h, x3, zero) for h in range(nhead)], axis=1)
            k_div16 = k_div.astype(BF16).reshape(nch, c, w)
            v3 = v.astype(BF16).reshape(nch, c, w)
            t_row = lax.broadcasted_iota(jnp.int32, (nhead * c, c), 0) % c
            s_col = lax.broadcasted_iota(jnp.int32, (nhead * c, c), 1)
            causal = (s_col >= t_row) if rev else (s_col <= t_row)
            outs = []
            for ci in range(nch):
                att = jnp.where(causal, _dot_nt(q_hat[ci], k_div16[ci]), 0.0)
                oc = _dot(att.astype(BF16), v3[ci])
                outs.append(sum(jnp.where(lane_head == h, oc[h * c:(h + 1) * c], 0.0) for h in range(nhead)))
            return jnp.concatenate(outs, axis=0)

        def pairs_direct():
            def halves(x):
                x4 = x.reshape(nch, 2, half, w)
                return (x4[:, 1], x4[:, 0]) if rev else (x4[:, 0], x4[:, 1])

            r8 = lax.broadcasted_iota(jnp.int32, (nch, half, w), 1)

            def rot(x, s):
                return x if s == 0 else pltpu.roll(x, (half - s) if rev else s, 1)

            def wrapped(s):
                return (r8 > half - 1 - s) if rev else (r8 < s)

            def pair_sum(wgt, vs):
                return (_dot(wgt.reshape(nch * half, w).astype(BF16), jmat) * vs.reshape(nch * half, w))

            (f_a, f_b), (v_a, v_b), (qd_a, qd_b) = halves(f), halves(v), halves(q)
            o_a = jnp.zeros((nch * half, w), F32)
            o_b = jnp.zeros((nch * half, w), F32)
            for delta in range(c):
                s = delta % half
                fr, vr = rot(f_a, s), rot(v_a, s)
                if delta < half:
                    fs_a = jnp.where(wrapped(s), 1.0, fr)
                    nxt_a = qd_a * fs_a
                    o_a = o_a + pair_sum(qd_a - nxt_a, vr)
                    qd_a = nxt_a
                    fs_b = jnp.where(wrapped(s), fr, rot(f_b, s))
                    vs_b = jnp.where(wrapped(s), vr, rot(v_b, s))
                else:
                    fs_b = jnp.where(wrapped(s), 1.0, fr)
                    vs_b = vr
                nxt_b = qd_b * fs_b
                o_b = o_b + pair_sum(qd_b - nxt_b, vs_b)
                qd_b = nxt_b
            o_a, o_b = o_a.reshape(nch, half, w), o_b.reshape(nch, half, w)
            return jnp.stack([o_b, o_a] if rev else [o_a, o_b], axis=1).reshape(tt, w)

        o = pairs_factored() if factored else pairs_direct()
        q_in = (q * p_in).astype(BF16)
        lane3 = lax.broadcasted_iota(jnp.int32, (nch, c, w), 2) % (2 * HG_DK)

        def per_head(x):
            x3 = x.astype(BF16).reshape(nch, c, w)
            zero = jnp.zeros_like(x3)
            return jnp.concatenate([jnp.where(lane3 < HG_DK, x3, zero), jnp.where(lane3 >= HG_DK, x3, zero)], axis=1)

        v_hat, k_hat = per_head(v), per_head(k_out)
        ntile = st_ref.shape[0]
        lt = w // ntile
        for ci in range(nch):
            for j in range(ntile):
                upd_ref[ci, j] = _dot_tn(v_hat[ci, :, j * lt:(j + 1) * lt], k_hat[ci, :, j * lt:(j + 1) * lt])
        st = [st_ref[j] for j in range(ntile)]
        parts = [None] * nch
        for ci in (range(nch - 1, -1, -1) if rev else range(nch)):
            parts[ci] = jnp.concatenate([_dot_nt(q_in[ci * c:(ci + 1) * c, j * lt:(j + 1) * lt], st[j].astype(BF16))
                                         for j in range(ntile)], axis=1)
            for j in range(ntile):
                st[j] = st[j] * dec_end[ci][:, j * lt:(j + 1) * lt] + upd_ref[ci, j]
        for j in range(ntile):
            st_ref[j] = st[j]
        return o + jnp.concatenate(parts, axis=0)

    def finish(tot):
        sq = tot * tot
        hi = sq.astype(BF16)
        lo = (sq - hi.astype(F32)).astype(BF16)
        ms = (_dot(hi, jmat) + _dot(lo, jmat)) * (1.0 / HG_DK)
        y = tot * lax.rsqrt(ms + EPS)
        return (y * ng_ref[...] * jax.nn.silu(hg_ref[...])).astype(BF16)

    def decay_ok(fz):
        fn = lb + (1.0 - lb) * jax.nn.sigmoid(fz)
        c = HG_FAST_CHUNK
        chunk_log =jnp.log(fn).reshape(tt // c, c, w).sum(axis=1)
        return jnp.min(chunk_log) >= math.log(HG_MIN_DECAY)

    return run, finish, decay_ok


def _rglru_body(x_ref, gate_ref, prev_ref, next_ref, cw_ref, cb_ref, wa_ref, ba_ref, wx_ref, bx_ref, lam_ref, h_ref,
                *, tile, tiles_per_batch):
    tt, w = x_ref.shape

    x = x_ref[...]
    has_prev = (tile >= 2).astype(F32)
    has_next = jnp.logical_and(tile >= 1, tile < tiles_per_batch).astype(F32)
    prev = prev_ref[...] * has_prev
    nxt = next_ref[...] * has_next
    row = lax.broadcasted_iota(jnp.int32, (tt, w), 0)
    xm1 = jnp.where(row == 0, prev[HALO - 1:HALO], _shift_rows(x, 1, False))
    xm2 = jnp.where(row == 0, prev[HALO - 2:HALO - 1],
                    jnp.where(row == 1, prev[HALO - 1:HALO], _shift_rows(x, 2, False)))
    xp1 = jnp.where(row == tt - 1, nxt[0:1], _shift_rows(x, 1, True))
    cw = cw_ref[...]
    u = cb_ref[...] + xm2 * cw[0:1] + xm1 * cw[1:2] + x * cw[2:3] + xp1 * cw[3:4]

    u16 = u.astype(BF16)
    r = jax.nn.sigmoid(_dot(u16, wa_ref[0]) + ba_ref[0])
    ig = jax.nn.sigmoid(_dot(u16, wx_ref[0]) + bx_ref[0])
    log_a = -RG_C * r * jax.nn.softplus(-lam_ref[0])
    a = jnp.exp(log_a)
    z = jnp.sqrt(jnp.tanh(-log_a) * (1.0 + a * a)) * (ig * u)

    def scan(rev):
        ng = tt // SUBLANES
        aa, zz = a.reshape(ng, SUBLANES, w), z.reshape(ng, SUBLANES, w)
        r8 = lax.broadcasted_iota(jnp.int32, (ng, SUBLANES, w), 1)
        s = 1
        while s < SUBLANES:
            ok = (r8 <= SUBLANES - 1 - s) if rev else (r8 >= s)
            sh = (SUBLANES - s) if rev else s
            zz = zz + jnp.where(ok, aa * pltpu.roll(zz, sh, 1), 0.0)
            aa = aa * jnp.where(ok, pltpu.roll(aa, sh, 1), 1.0)
            s *= 2
        carry = h_ref[0:1, :]
        groups = [None] * ng
        for gi in (range(ng - 1, -1, -1) if rev else range(ng)):
            groups[gi] = aa[gi] * carry + zz[gi]
            carry = groups[gi][0:1] if rev else groups[gi][SUBLANES - 1:SUBLANES]
        h_ref[...] = jnp.broadcast_to(carry, h_ref.shape)
        return jnp.concatenate(groups, axis=0)

    def finish(h):
        return (jax.nn.gelu(gate_ref[...]) * h).astype(BF16)

    return scan, finish


def _mixers_kernel(lbl_ref, hq_ref, fz_ref, hi_ref, hg_ref, ng_ref, jmat_ref,
                   x_ref, gate_ref, prev_ref, next_ref, cw_ref, cb_ref, wa_ref, ba_ref, wx_ref, bx_ref, lam_ref, fzn_ref,
                   yg_ref, yr_ref, st_ref, ofwd_ref, upd_ref, h_ref, hfwd_ref, ok_ref, *, layer, tiles_per_batch):
    d = pl.program_id(1)
    i = pl.program_id(2)
    tile = _seq_tile(d, i, tiles_per_batch)

    @pl.when(i == 0)
    def _():
        st_ref[...] = jnp.zeros_like(st_ref)
        h_ref[...] = jnp.zeros_like(h_ref)
        ok_ref[0] = 0

    factored = ok_ref[0] == 1
    run, finish_hg, decay_ok = _hgrn_body(lbl_ref, hq_ref, fz_ref, hi_ref, hg_ref, ng_ref, jmat_ref, st_ref, upd_ref,
                                          layer=layer)
    scan, finish_rg = _rglru_body(x_ref, gate_ref, prev_ref, next_ref, cw_ref, cb_ref, wa_ref, ba_ref, wx_ref, bx_ref,
                                  lam_ref, h_ref, tile=tile, tiles_per_batch=tiles_per_batch)

    for fac in (True, False):
        chosen = factored if fac else jnp.logical_not(factored)

        @pl.when(jnp.logical_and(d == 0, chosen))
        def _():
            ofwd_ref[tile] = run(False, fac)
            hfwd_ref[tile] = scan(False)

        @pl.when(jnp.logical_and(d == 1, chosen))
        def _():
            yg_ref[...] = finish_hg(run(True, fac) + ofwd_ref[tile])
            yr_ref[...] = finish_rg(scan(True) + hfwd_ref[tile])

    ok_ref[0] = decay_ok(fzn_ref[...]).astype(jnp.int32)


def _mixers(ph, pr, lb_logits, norm_g_all, conv_w, conv_b, wa, ba, wx, bx, lam, *, layer, bsz, tiles_per_batch):
    n = ph.shape[0]
    depth = lb_logits.shape[0]
    hw = norm_g_all.shape[-1]
    rw = conv_b.shape[-1]
    head = jnp.arange(hw) // HG_DK
    same = head[:, None] == head[None, :]
    lt = 2 * HG_DK
    ntile = hw // lt
    hpt = TILE // HALO

    def seq_row(b, d, i):
        return _seq_row(b, _seq_tile(d, i, tiles_per_batch), bsz, tiles_per_batch)

    def col(cfn):
        return lambda b, d, i: (seq_row(b, d, i), cfn(d))

    def out_row(b, d, i):
        tile = jnp.where(d == 0, 0, _seq_tile(d, i, tiles_per_batch))
        return (_seq_row(b, tile, bsz, tiles_per_batch), 0)

    per_dir = lambda b, d, i: (layer, d, 0, 0)
    return pl.pallas_call(
        functools.partial(_mixers_kernel, layer=layer, tiles_per_batch=tiles_per_batch),
        grid=(bsz, 2, tiles_per_batch + 1),
        in_specs=[
            pl.BlockSpec((depth, 1, 1, hw), lambda b, d, i: (0, d, 0, 0)),
            pl.BlockSpec((TILE, hw), col(lambda d: 0)),
            pl.BlockSpec((TILE, hw), col(lambda d: 1 + d)),
            pl.BlockSpec((TILE, hw), col(lambda d: 3)),
            pl.BlockSpec((TILE, hw), col(lambda d: 4)),
            _layer_spec(norm_g_all, layer), _const_spec((hw, hw)),
            pl.BlockSpec((TILE, rw), col(lambda d: 0)),
            pl.BlockSpec((TILE, rw), col(lambda d: 1)),
            pl.BlockSpec((HALO, rw), lambda b, d, i: (jnp.maximum(seq_row(b, d, i) * hpt - 1, 0), 0)),
            pl.BlockSpec((HALO, rw), lambda b, d, i: (jnp.minimum((seq_row(b, d, i) + 1) * hpt, n // HALO - 1), 0)),
            _layer_spec(conv_w, layer), _layer_spec(conv_b, layer),
            pl.BlockSpec((None, 1, rw, rw), per_dir), pl.BlockSpec((None, 1, 1, rw), per_dir),
            pl.BlockSpec((None, 1, rw, rw), per_dir), pl.BlockSpec((None, 1, 1, rw), per_dir),
            pl.BlockSpec((None, 1, 1, rw), per_dir),
            pl.BlockSpec((TILE, hw), lambda b, d, i: (seq_row(b, d, jnp.minimum(i + 1, tiles_per_batch)), 1 + d)),
        ],
        out_specs=[pl.BlockSpec((TILE, hw), out_row), pl.BlockSpec((TILE, rw), out_row)],
        out_shape=[jax.ShapeDtypeStruct((n, hw), BF16), jax.ShapeDtypeStruct((n, rw), BF16)],
        scratch_shapes=[pltpu.VMEM((ntile, lt, lt), F32), pltpu.VMEM((tiles_per_batch + 1, TILE, hw), F32),
                        pltpu.VMEM((TILE // HG_CHUNK, ntile, lt, lt), F32),
                        pltpu.VMEM((SUBLANES, rw), F32), pltpu.VMEM((tiles_per_batch + 1, TILE, rw), F32),
                        pltpu.SMEM((1,), jnp.int32)],
        compiler_params=_cparams(("arbitrary", "arbitrary", "arbitrary")),
        name="mixers",
    )(lb_logits.reshape(depth, 2, 1, hw), ph, ph, ph, ph, norm_g_all, same.astype(BF16),
      pr, pr, pr, pr, conv_w, conv_b, wa, ba, wx, bx, lam, ph)


def _rope_tables(length):
    n_freq = HEAD_DIM // 4
    rows = length // GRID_W
    row = jnp.repeat(jnp.arange(rows), GRID_W)
    col = jnp.tile(jnp.arange(GRID_W), rows)
    inv_freq = ROPE_BASE ** (-jnp.arange(n_freq, dtype=F32) / n_freq)
    ang = jnp.stack([row[:, None] * inv_freq, col[:, None] * inv_freq], axis=1)
    cos, sin = jnp.cos(ang), jnp.sin(ang)
    zero = jnp.zeros_like(sin[:, 0])
    cos_h = jnp.concatenate([cos[:, 0], cos[:, 0], cos[:, 1], cos[:, 1]], axis=-1)
    sa_h = jnp.concatenate([-sin[:, 0], zero, -sin[:, 1], zero], axis=-1)
    sb_h = jnp.concatenate([zero, sin[:, 0], zero, sin[:, 1]], axis=-1)
    ctx = (jnp.ones((TOKEN_TILE, HEAD_DIM), F32), jnp.zeros((TOKEN_TILE, HEAD_DIM), F32), jnp.zeros((TOKEN_TILE, HEAD_DIM), F32))
    return tuple(jnp.tile(jnp.concatenate([c, t], axis=0), (1, 2)) for c, t in zip(ctx, (cos_h, sa_h, sb_h)))


def _block_diag(wb):
    nb, bw = wb.shape[-3], wb.shape[-1]
    out = jnp.zeros(wb.shape[:-3] + (nb * bw, nb * bw), wb.dtype)
    for j in range(nb):
        out = out.at[..., j * bw:(j + 1) * bw, j * bw:(j + 1) * bw].set(wb[..., j, :, :])
    return out


def _regroup_heads(t, axis):
    shp = t.shape
    t = t.reshape(shp[:axis] + (ATT_KV_HEADS, ATT_GROUP, HEAD_DIM) + shp[axis + 1:])
    return jnp.swapaxes(t, axis, axis + 1).reshape(shp)


def kernel(x, c, ctx, c_ctx, ada_w, ada_b, norm_ffn1, ffn1_w_in, ffn1_w_out, norm_mix, w_in, w_out, attn_sink, hg_lb_logits, hg_norm, rg_conv_w, rg_conv_b, rg_gate_a_w, rg_gate_a_b, rg_gate_x_w, rg_gate_x_b, rg_lambda, norm_ffn2, ffn2_w_in, ffn2_w_out, final_norm):
    bsz, length, d = x.shape
    depth = ada_w.shape[0]
    assert ctx.shape[1] == CTX_LEN == TILE and length % TOKEN_TILE == 0 and (bsz * CTX_LEN) % TOKEN_TILE == 0
    tpb = length // TILE
    geo = dict(n_ctx_tiles=bsz * CTX_LEN // TOKEN_TILE, tiles_per_batch=length // TOKEN_TILE)

    ctx2, x2 = ctx.reshape(bsz * CTX_LEN, d), x.reshape(bsz * length, d)
    cond = jnp.zeros((SUBLANES, d), F32).at[:bsz].set(c).at[bsz].set(c_ctx)
    mods_all = _adaln(cond, ada_w, ada_b).reshape(depth, SUBLANES, N_MODS, d)[:, :bsz + 1]
    tabs = _rope_tables(length)

    att_w = w_out.shape[1] // 2
    kv_w = ATT_KV_HEADS * HEAD_DIM
    hg_w = hg_norm.shape[1]
    rg_w = rg_conv_b.shape[1]
    widths = (att_w, kv_w, 5 * hg_w, 2 * rg_w)
    assert sum(widths) + kv_w == w_in.shape[2]

    to16 = lambda t: t.astype(BF16)
    f1_in, f1_out, f2_in, f2_out = map(to16, (ffn1_w_in, ffn1_w_out, ffn2_w_in, ffn2_w_out))
    wi16 = to16(jnp.concatenate([_regroup_heads(w_in[:, :, :att_w], 2), w_in[:, :, att_w:]], axis=2))
    wo16 = to16(jnp.concatenate([_regroup_heads(w_out[:, :att_w], 1), w_out[:, att_w:]], axis=1))
    wa_bd, wx_bd = to16(_block_diag(rg_gate_a_w)), to16(_block_diag(rg_gate_x_w))
    as_row = lambda t: t.reshape(t.shape[:-1] + (1, t.shape[-1]))
    g1, gm, g2, hgn = map(as_row, (norm_ffn1, norm_mix, norm_ffn2, hg_norm))
    conv_b, ba, bx, lam = map(as_row, (rg_conv_b, rg_gate_a_b, rg_gate_x_b, rg_lambda))

    xs = (ctx2, x2)
    for l in range(depth):
        last = l == depth - 1
        xs, q, k, v, ph, pr = _token_block(xs, mods_all, g1, f1_in, f1_out, layer=l, mod0=0,
                                           proj=(gm, wi16, tabs, widths), **geo)
        yg, yr = _mixers(ph, pr, hg_lb_logits, hgn, rg_conv_w, conv_b, wa_bd, ba, wx_bd, bx, lam,
                         layer=l, bsz=bsz, tiles_per_batch=tpb)
        xs, = _token_block(xs, mods_all, g2, f2_in, f2_out, layer=l, mod0=6,
                           mix=(q, k, v, attn_sink, yg, yr, wo16, bsz, length),
                           final_g=final_norm if last else None, latent_only=last, **geo)
    return xs.reshape(bsz, length, d)
```

```python
import functools
import math

import jax
import jax.numpy as jnp
import numpy as np
from jax import lax
from jax.experimental import pallas as pl
from jax.experimental.pallas import tpu as pltpu

F32 = jnp.float32
BF16 = jnp.bfloat16

GRID_W = 64
CTX_LEN = 256
N_MODS = 9
EPS = 1e-6
HEAD_DIM = 64
ATT_KV_HEADS = 2
ATT_GROUP = 4
WINDOW = 128
ATT_BLOCK = 128
ATT_QBLOCKS = 2
ROPE_BASE = 10000.0
LOG2E = 1.4426950408889634
HG_HEADS = 4
HG_DK = 64
HG_CHUNK = 16
HG_FAST_CHUNK = 32
HG_MIN_DECAY = 1e-30
RG_BLOCKS = 4
RG_CONV = 4
RG_C = 8.0

TILE = 256
TOKEN_TILE = 512
SUBLANES = 8
HALO = SUBLANES
VMEM_LIMIT = 52 * 1024 * 1024


def _cparams(sem):
    return pltpu.CompilerParams(dimension_semantics=sem, vmem_limit_bytes=VMEM_LIMIT)


def _const_spec(shape):
    nd = len(shape)
    return pl.BlockSpec(shape, lambda *_: (0,) * nd, pipeline_mode=pl.Buffered(1))


def _layer_spec(arr, l):
    nd = arr.ndim
    return pl.BlockSpec((None,) + arr.shape[1:], lambda *_: (l,) + (0,) * (nd - 1), pipeline_mode=pl.Buffered(1))


def _rms(xf, g):
    return xf * lax.rsqrt(jnp.mean(xf * xf, axis=-1, keepdims=True) + EPS) * g


def _dot(a, b):
    return jnp.dot(a, b, preferred_element_type=F32)


def _dot_nt(a, b):
    return lax.dot_general(a, b, (((1,), (1,)), ((), ())), preferred_element_type=F32)


def _dot_tn(a, b):
    return lax.dot_general(a, b, (((0,), (0,)), ((), ())), preferred_element_type=F32)


def _adaln_kernel(c_ref, w_ref, b_ref, o_ref):
    s = jax.nn.silu(c_ref[...]).astype(BF16)
    o_ref[0] = _dot(s, w_ref[0].astype(BF16)) + b_ref[0]


def _adaln(cond, ada_w, ada_b):
    depth, d, n = ada_w.shape
    tn = n // 8
    return pl.pallas_call(
        _adaln_kernel,
        grid=(depth, n // tn),
        in_specs=[
            pl.BlockSpec(cond.shape, lambda l, j: (0, 0)),
            pl.BlockSpec((1, d, tn), lambda l, j: (l, 0, j)),
            pl.BlockSpec((1, 1, tn), lambda l, j: (l, 0, j)),
        ],
        out_specs=pl.BlockSpec((1, cond.shape[0], tn), lambda l, j: (l, 0, j)),
        out_shape=jax.ShapeDtypeStruct((depth, cond.shape[0], n), F32),
        compiler_params=_cparams(("arbitrary", "arbitrary")),
        name="adaln",
    )(cond, ada_w, ada_b.reshape(depth, 1, n))


def _attend(sink_ref, layer, q_blocks, k_loc, v_loc, k_ctx, v_ctx, n_blocks, length):
    sw = k_ctx[0].shape[1]
    nslab = q_blocks[0].shape[1] // sw

    def only(t, kv, fill=0.0):
        lane = lax.broadcasted_iota(jnp.int32, t.shape, 1)
        return jnp.where((lane >= kv * HEAD_DIM) & (lane < (kv + 1) * HEAD_DIM), t, jnp.full_like(t, fill))

    nq = len(q_blocks)
    chains = [(j, kv) for j in range(nq) for kv in range(ATT_KV_HEADS)]
    q, valid = [], []
    for j in range(nq):
        q.append(jnp.concatenate([q_blocks[j][:, g * sw:(g + 1) * sw] for g in range(nslab)], axis=0))
        n = n_blocks[j]
        k_pos = (n - 1) * ATT_BLOCK + lax.broadcasted_iota(jnp.int32, (3 * ATT_BLOCK, ATT_BLOCK), 0)
        q_pos = n * ATT_BLOCK + lax.broadcasted_iota(jnp.int32, (3 * ATT_BLOCK, ATT_BLOCK), 1)
        ok = (jnp.abs(k_pos - q_pos) <= WINDOW) & (k_pos >= 0) & (k_pos < length) & (n >= 0)
        valid.append(jnp.concatenate([ok] * nslab, axis=1))
    sink = {kv: jnp.concatenate([jnp.full((1, ATT_BLOCK), sink_ref[layer, kv * ATT_GROUP + g] * LOG2E, F32)
                                 for g in range(nslab)], axis=1) for kv in range(ATT_KV_HEADS)}
    s_loc = {c: jnp.where(valid[c[0]], _dot_nt(only(k_loc[c[0]], c[1]), q[c[0]]), -jnp.inf) for c in chains}
    s_ctx = {c: _dot_nt(only(k_ctx[c[0]], c[1]), q[c[0]]) for c in chains}
    m = {c: jnp.maximum(jnp.maximum(s_loc[c].max(0, keepdims=True), s_ctx[c].max(0, keepdims=True)), sink[c[1]])
         for c in chains}
    p_loc = {c: jnp.exp2(s_loc[c] - m[c]).astype(BF16) for c in chains}
    p_ctx = {c: jnp.exp2(s_ctx[c] - m[c]).astype(BF16) for c in chains}
    o = {c: (_dot_tn(only(v_loc[c[0]], c[1], 1.0), p_loc[c]) + _dot_tn(only(v_ctx[c[0]], c[1], 1.0), p_ctx[c]))
         for c in chains}
    row_kv = lax.broadcasted_iota(jnp.int32, (sw, nslab * ATT_BLOCK), 0) // HEAD_DIM
    outs = []
    for j in range(nq):
        scaled = []
        for kv in range(ATT_KV_HEADS):
            c = (j, kv)
            other = (1 - kv) * HEAD_DIM
            denom = o[c][other:other + 1, :] + jnp.exp2(sink[kv] - m[c])
            scaled.append(o[c] / denom)
        out = jnp.where(row_kv == 0, scaled[0], scaled[1]).T
        outs.append(jnp.concatenate([out[g * ATT_BLOCK:(g + 1) * ATT_BLOCK] for g in range(nslab)],
                                    axis=1).astype(BF16))
    return outs


def _rope(t, cos, sa, sb):
    w = t.shape[1]
    rep = w // cos.shape[1]
    c, a, b = (jnp.concatenate([m] * rep, axis=1) if rep > 1 else m for m in (cos, sa, sb))
    half = HEAD_DIM // 4
    return t * c + pltpu.roll(t, w - half, 1) * a + pltpu.roll(t, half, 1) * b


def _token_kernel(*refs, layer, mod0, n_ctx_tiles, tiles_per_batch, skip, length, two_src, has_mix, has_proj,
                  has_final):
    it = iter(refs)
    if two_src:
        c_ref = next(it)
    x_ref, mods_ref, g_ref, win_ref, wout_ref = (next(it) for _ in range(5))
    if has_mix:
        nloc = TOKEN_TILE // ATT_BLOCK + 2
        sink_ref, q_in_ref = next(it), next(it)
        kc_refs, vc_refs = [next(it) for _ in range(2)], [next(it) for _ in range(2)]
        kl_refs, vl_refs = [next(it) for _ in range(nloc)], [next(it) for _ in range(nloc)]
        yg_ref, yr_ref, wmix_ref = (next(it) for _ in range(3))
    if has_proj:
        gp_ref, wp_ref, cos_ref, sa_ref, sb_ref = (next(it) for _ in range(5))
    if has_final:
        fin_ref = next(it)
    o_ref = next(it)
    if has_proj:
        q_ref, k_ref, v_ref, ph_ref, pr_ref = (next(it) for _ in range(5))

    mods = mods_ref[0]
    shift, scale, gate = mods[mod0:mod0 + 1], mods[mod0 + 1:mod0 + 2], mods[mod0 + 2:mod0 + 3]
    f = wout_ref.shape[0]
    ya_all = None
    if has_mix:
        lt = pl.program_id(0) + skip - n_ctx_tiles
        bpt = TOKEN_TILE // ATT_BLOCK
        n_first = jnp.where(lt >= 0, (lt % tiles_per_batch) * bpt, -bpt)
        ya_all = _attend(
            sink_ref, layer,
            [q_in_ref[g * ATT_BLOCK:(g + 1) * ATT_BLOCK, :] for g in range(bpt)],
            [jnp.concatenate([r[...] for r in kl_refs[g:g + 3]], axis=0) for g in range(bpt)],
            [jnp.concatenate([r[...] for r in vl_refs[g:g + 3]], axis=0) for g in range(bpt)],
            [kc_refs[g * ATT_BLOCK // TILE][...] for g in range(bpt)],
            [vc_refs[g * ATT_BLOCK // TILE][...] for g in range(bpt)],
            [n_first + g for g in range(bpt)], length)
    for r0 in range(0, x_ref.shape[0], TILE):
        rows = slice(r0, r0 + TILE)
        x = x_ref[rows, :]
        if two_src:
            x = jnp.where(pl.program_id(0) < n_ctx_tiles, c_ref[rows, :], x)
        if has_mix:
            g0 = r0 // ATT_BLOCK
            ya = ya_all[g0:g0 + TILE // ATT_BLOCK]
            y = jnp.concatenate([jnp.concatenate(ya, axis=0), yg_ref[rows, :], yr_ref[rows, :]], axis=-1)
            x = x + mods[5:6] * _dot(y, wmix_ref[...])
        h = (_rms(x, g_ref[...]) * (1.0 + scale) + shift).astype(BF16)
        a = _dot(h, win_ref[:, :f])
        b = _dot(h, win_ref[:, f:])
        u = (jax.nn.silu(a) * b).astype(BF16)
        out = x + (0.5 * gate) * _dot(u, wout_ref[...])
        o_ref[rows, :] = _rms(out, fin_ref[...]) if has_final else out
        if has_proj:
            h = (_rms(out, gp_ref[...]) * (1.0 + mods[4:5]) + mods[3:4]).astype(BF16)
            cos, sa, sb = cos_ref[rows, :], sa_ref[rows, :], sb_ref[rows, :]
            o = 0
            for ref, rotate, scl in ((q_ref, True, LOG2E * HEAD_DIM ** -0.5), (k_ref, True, None), (v_ref, False, None),
                                     (ph_ref, False, None), (pr_ref, False, None)):
                p = _dot(h, wp_ref[:, o:o + ref.shape[1]])
                o += ref.shape[1]
                if rotate:
                    p = _rope(p, cos, sa, sb)
                if scl is not None:
                    p = p * scl
                ref[rows, :] = p.astype(ref.dtype)


def _token_block(src, mods_all, g_all, w_in_all, w_out_all, *, layer, mod0, n_ctx_tiles, tiles_per_batch,
                 mix=None, proj=None, final_g=None, latent_only=False):
    two_src = isinstance(src, tuple)
    if two_src:
        ctx2, x2 = src
        n, d = ctx2.shape[0] + x2.shape[0], x2.shape[1]
    else:
        n, d = src.shape
    skip = n_ctx_tiles if latent_only else 0
    nt = n // TOKEN_TILE - skip
    ctx_row = mods_all.shape[1] - 1
    row = lambda i: (i + skip, 0)

    def mod_row(i):
        j = i + skip
        return (layer, jnp.where(j < n_ctx_tiles, ctx_row, (j - n_ctx_tiles) // tiles_per_batch), 0, 0)

    if two_src:
        in_specs = [pl.BlockSpec((TOKEN_TILE, d), lambda i: (jnp.minimum(i, n_ctx_tiles - 1), 0)),
                    pl.BlockSpec((TOKEN_TILE, d), lambda i: (jnp.maximum(i - n_ctx_tiles, 0), 0))]
        args = [ctx2, x2]
    else:
        in_specs, args = [pl.BlockSpec((TOKEN_TILE, d), row)], [src]
    in_specs += [pl.BlockSpec((None, 1, N_MODS, d), mod_row), _layer_spec(g_all, layer),
                 _layer_spec(w_in_all, layer), _layer_spec(w_out_all, layer)]
    args += [mods_all, g_all, w_in_all, w_out_all]
    length = None
    if mix is not None:
        q, k, v, sink, yg, yr, w_mix_all, bsz, length = mix
        assert TOKEN_TILE == 2 * CTX_LEN and TILE == ATT_QBLOCKS * ATT_BLOCK and k.shape[1] == 2 * HEAD_DIM
        nblk = length // ATT_BLOCK
        blk0 = bsz * CTX_LEN // ATT_BLOCK
        bpt = TOKEN_TILE // ATT_BLOCK

        def lat(i):
            lt = i + skip - n_ctx_tiles
            return lt >= 0, jnp.where(lt >= 0, lt // tiles_per_batch, 0), jnp.where(lt >= 0, lt % tiles_per_batch, 0)

        def ctx_kv_row(h):
            def f(i):
                is_lat, b, _ = lat(i)
                return (jnp.where(is_lat, b, (TOKEN_TILE // CTX_LEN) * (i + skip) + h), 0)
            return f

        def loc_row(off):
            def f(i):
                _, b, jb = lat(i)
                return (blk0 + nblk * b + jnp.clip(jb * bpt + off, 0, nblk - 1), 0)
            return f

        cspec = [pl.BlockSpec((CTX_LEN, k.shape[1]), ctx_kv_row(h)) for h in range(2)]
        lspec = [pl.BlockSpec((ATT_BLOCK, k.shape[1]), loc_row(o)) for o in range(-1, bpt + 1)]
        in_specs += [pl.BlockSpec(memory_space=pltpu.SMEM), pl.BlockSpec((TOKEN_TILE, q.shape[1]), row)]
        in_specs += cspec + cspec + lspec + lspec
        in_specs += [pl.BlockSpec((TOKEN_TILE, yg.shape[1]), row), pl.BlockSpec((TOKEN_TILE, yr.shape[1]), row),
                     _layer_spec(w_mix_all, layer)]
        args += [sink, q, k, k, v, v] + [k] * len(lspec) + [v] * len(lspec) + [yg, yr, w_mix_all]
    out_specs = [pl.BlockSpec((TOKEN_TILE, d), lambda i: (i, 0))]
    out_shape = [jax.ShapeDtypeStruct((nt * TOKEN_TILE, d), F32)]
    if proj is not None:
        gp_all, wp_all, tabs, widths = proj

        def tab_row(i):
            return (jnp.where(i < n_ctx_tiles, 0, 1 + (i - n_ctx_tiles) % tiles_per_batch), 0)

        in_specs += [_layer_spec(gp_all, layer), _layer_spec(wp_all, layer)]
        in_specs += [pl.BlockSpec((TOKEN_TILE, tabs[0].shape[1]), tab_row)] * 3
        args += [gp_all, wp_all, *tabs]
        nq, nk, nh, nr = widths
        out_specs += [pl.BlockSpec((TOKEN_TILE, wd), lambda i: (i, 0)) for wd in (nq, nk, nk, nh, nr)]
        out_shape += [jax.ShapeDtypeStruct((n, wd), dt) for wd, dt in
                      ((nq, BF16), (nk, BF16), (nk, BF16), (nh, F32), (nr, F32))]
    if final_g is not None:
        in_specs.append(_const_spec((1, d)))
        args.append(final_g.reshape(1, d))
    return pl.pallas_call(
        functools.partial(_token_kernel, layer=layer, mod0=mod0, n_ctx_tiles=n_ctx_tiles,
                          tiles_per_batch=tiles_per_batch, skip=skip, length=length, two_src=two_src,
                          has_mix=mix is not None, has_proj=proj is not None, has_final=final_g is not None),
        grid=(nt,),
        in_specs=in_specs,
        out_specs=out_specs,
        out_shape=out_shape,
        compiler_params=_cparams(("arbitrary",)),
        name="ffn_out" if mix is not None else "ffn_in",
    )(*args)


def _seq_tile(d, i, tiles_per_batch):
    return jnp.where(d == 0, i, jnp.where(i == 0, 0, tiles_per_batch + 1 - i))


def _seq_row(b, tile, bsz, tiles_per_batch):
    return jnp.where(tile == 0, b, bsz + tiles_per_batch * b + tile - 1)


def _shift_rows(x, s, rev):
    if s == 0:
        return x
    return pltpu.roll(x, (x.shape[0] - s) if rev else s, 0)


def _hgrn_body(lbl_ref, hq_ref, fz_ref, hi_ref, hg_ref, ng_ref, jmat_ref, st_ref, upd_ref, *, layer):
    tt, w = hq_ref.shape

    logits = lbl_ref[:, 0, 0, :]
    e = jnp.exp(logits - logits.max(0, keepdims=True))
    lbp = e / e.sum(0, keepdims=True)
    lb = jnp.zeros((1, w), F32)
    for j in range(1, layer + 1):
        lb = lb + lbp[j:j + 1]

    q = jax.nn.silu(hq_ref[...])
    f = lb + (1.0 - lb) * jax.nn.sigmoid(fz_ref[...])
    kk = 1.0 - f
    v = hi_ref[...]
    jmat = jmat_ref[...]

    def run(rev, factored):
        c = HG_FAST_CHUNK if factored else HG_CHUNK
        pos = lax.broadcasted_iota(jnp.int32, (tt, w), 0) % c

        def in_chunk(s, r):
            return (pos <= c - 1 - s) if r else (pos >= s)

        def cumprod(y, r):
            s = 1
            while s < c:
                y = y * jnp.where(in_chunk(s, r), _shift_rows(y, s, r), 1.0)
                s *= 2
            return y

        p_in = cumprod(f, rev)
        half = c // 2
        nch = tt // c
        p3 = p_in.reshape(nch, c, w)
        dec_end = p3[:, 0:1, :] if rev else p3[:, c - 1:c, :]
        if factored:
            k_div = kk / p_in
            k_out = (k_div.reshape(nch, c, w) * dec_end).reshape(tt, w)
        else:
            k_out = kk * cumprod(jnp.where(in_chunk(1, not rev), _shift_rows(f, 1, not rev), 1.0), not rev)
        nhead = w // HG_DK
        lane_head = lax.broadcasted_iota(jnp.int32, (c, w), 1) // HG_DK

        def pairs_factored():
            x3 = (q * p_in).astype(BF16).reshape(nch, c, w)
            zero = jnp.zeros_like(x3)
            q_hat = jnp.concatenate([jnp.where(lane_head == h, x3, zero) for h in range(nhead)], axis=1)
            k_div16 = k_div.astype(BF16).reshape(nch, c, w)
            v3 = v.astype(BF16).reshape(nch, c, w)
            t_row = lax.broadcasted_iota(jnp.int32, (nhead * c, c), 0) % c
            s_col = lax.broadcasted_iota(jnp.int32, (nhead * c, c), 1)
            causal = (s_col >= t_row) if rev else (s_col <= t_row)
            outs = []
            for ci in range(nch):
                att = jnp.where(causal, _dot_nt(q_hat[ci], k_div16[ci]), 0.0)
                oc = _dot(att.astype(BF16), v3[ci])
                outs.append(sum(jnp.where(lane_head == h, oc[h * c:(h + 1) * c], 0.0) for h in range(nhead)))
            return jnp.concatenate(outs, axis=0)

        def pairs_direct():
            def halves(x):
                x4 = x.reshape(nch, 2, half, w)
                return (x4[:, 1], x4[:, 0]) if rev else (x4[:, 0], x4[:, 1])

            r8 = lax.broadcasted_iota(jnp.int32, (nch, half, w), 1)

            def rot(x, s):
                return x if s == 0 else pltpu.roll(x, (half - s) if rev else s, 1)

            def wrapped(s):
                return (r8 > half - 1 - s) if rev else (r8 < s)

            def pair_sum(wgt, vs):
                return (_dot(wgt.reshape(nch * half, w).astype(BF16), jmat) * vs.reshape(nch * half, w))

            (f_a, f_b), (v_a, v_b), (qd_a, qd_b) = halves(f), halves(v), halves(q)
            o_a = jnp.zeros((nch * half, w), F32)
            o_b = jnp.zeros((nch * half, w), F32)
            for delta in range(c):
                s = delta % half
                fr, vr = rot(f_a, s), rot(v_a, s)
                if delta < half:
                    fs_a = jnp.where(wrapped(s), 1.0, fr)
                    nxt_a = qd_a * fs_a
                    o_a = o_a + pair_sum(qd_a - nxt_a, vr)
                    qd_a = nxt_a
                    fs_b = jnp.where(wrapped(s), fr, rot(f_b, s))
                    vs_b = jnp.where(wrapped(s), vr, rot(v_b, s))
                else:
                    fs_b = jnp.where(wrapped(s), 1.0, fr)
                    vs_b = vr
                nxt_b = qd_b * fs_b
                o_b = o_b + pair_sum(qd_b - nxt_b, vs_b)
                qd_b = nxt_b
            o_a, o_b = o_a.reshape(nch, half, w), o_b.reshape(nch, half, w)
            return jnp.stack([o_b, o_a] if rev else [o_a, o_b], axis=1).reshape(tt, w)

        o = pairs_factored() if factored else pairs_direct()
        q_in = (q * p_in).astype(BF16)
        lane3 = lax.broadcasted_iota(jnp.int32, (nch, c, w), 2) % (2 * HG_DK)

        def per_head(x):
            x3 = x.astype(BF16).reshape(nch, c, w)
            zero = jnp.zeros_like(x3)
            return jnp.concatenate([jnp.where(lane3 < HG_DK, x3, zero), jnp.where(lane3 >= HG_DK, x3, zero)], axis=1)

        v_hat, k_hat = per_head(v), per_head(k_out)
        ntile = st_ref.shape[0]
        lt = w // ntile
        for ci in range(nch):
            for j in range(ntile):
                upd_ref[ci, j] = _dot_tn(v_hat[ci, :, j * lt:(j + 1) * lt], k_hat[ci, :, j * lt:(j + 1) * lt])
        st = [st_ref[j] for j in range(ntile)]
        parts = [None] * nch
        for ci in (range(nch - 1, -1, -1) if rev else range(nch)):
            parts[ci] = jnp.concatenate([_dot_nt(q_in[ci * c:(ci + 1) * c, j * lt:(j + 1) * lt], st[j].astype(BF16))
                                         for j in range(ntile)], axis=1)
            for j in range(ntile):
                st[j] = st[j] * dec_end[ci][:, j * lt:(j + 1) * lt] + upd_ref[ci, j]
        for j in range(ntile):
            st_ref[j] = st[j]
        return o + jnp.concatenate(parts, axis=0)

    def finish(tot):
        sq = tot * tot
        hi = sq.astype(BF16)
        lo = (sq - hi.astype(F32)).astype(BF16)
        ms = (_dot(hi, jmat) + _dot(lo, jmat)) * (1.0 / HG_DK)
        y = tot * lax.rsqrt(ms + EPS)
        return (y * ng_ref[...] * jax.nn.silu(hg_ref[...])).astype(BF16)

    def decay_ok(fz):
        fn = lb + (1.0 - lb) * jax.nn.sigmoid(fz)
        c = HG_FAST_CHUNK
        chunk_log = jnp.log(fn).reshape(tt // c, c, w).sum(axis=1)
        return jnp.min(chunk_log) >= math.log(HG_MIN_DECAY)

    return run, finish, decay_ok


def _rglru_body(x_ref, gate_ref, prev_ref, next_ref, cw_ref, cb_ref, wa_ref, ba_ref, wx_ref, bx_ref, lam_ref, h_ref,
                *, tile, tiles_per_batch):
    tt, w = x_ref.shape

    x = x_ref[...]
    has_prev = (tile >= 2).astype(F32)
    has_next = jnp.logical_and(tile >= 1, tile < tiles_per_batch).astype(F32)
    prev = prev_ref[...] * has_prev
    nxt = next_ref[...] * has_next
    row = lax.broadcasted_iota(jnp.int32, (tt, w), 0)
    xm1 = jnp.where(row == 0, prev[HALO - 1:HALO], _shift_rows(x, 1, False))
    xm2 = jnp.where(row == 0, prev[HALO - 2:HALO - 1],
                    jnp.where(row == 1, prev[HALO - 1:HALO], _shift_rows(x, 2, False)))
    xp1 = jnp.where(row == tt - 1, nxt[0:1], _shift_rows(x, 1, True))
    cw = cw_ref[...]
    u = cb_ref[...] + xm2 * cw[0:1] + xm1 * cw[1:2] + x * cw[2:3] + xp1 * cw[3:4]

    u16 = u.astype(BF16)
    r = jax.nn.sigmoid(_dot(u16, wa_ref[0]) + ba_ref[0])
    ig = jax.nn.sigmoid(_dot(u16, wx_ref[0]) + bx_ref[0])
    log_a = -RG_C * r * jax.nn.softplus(-lam_ref[0])
    a = jnp.exp(log_a)
    z = jnp.sqrt(jnp.tanh(-log_a) * (1.0 + a * a)) * (ig * u)

    def scan(rev):
        ng = tt // SUBLANES
        aa, zz = a.reshape(ng, SUBLANES, w), z.reshape(ng, SUBLANES, w)
        r8 = lax.broadcasted_iota(jnp.int32, (ng, SUBLANES, w), 1)
        s = 1
        while s < SUBLANES:
            ok = (r8 <= SUBLANES - 1 - s) if rev else (r8 >= s)
            sh = (SUBLANES - s) if rev else s
            zz = zz + jnp.where(ok, aa * pltpu.roll(zz, sh, 1), 0.0)
            aa = aa * jnp.where(ok, pltpu.roll(aa, sh, 1), 1.0)
            s *= 2
        carry = h_ref[0:1, :]
        groups = [None] * ng
        for gi in (range(ng - 1, -1, -1) if rev else range(ng)):
            groups[gi] = aa[gi] * carry + zz[gi]
            carry = groups[gi][0:1] if rev else groups[gi][SUBLANES - 1:SUBLANES]
        h_ref[...] = jnp.broadcast_to(carry, h_ref.shape)
        return jnp.concatenate(groups, axis=0)

    def finish(h):
        return (jax.nn.gelu(gate_ref[...]) * h).astype(BF16)

    return scan, finish


def _mixers_kernel(lbl_ref, hq_ref, fz_ref, hi_ref, hg_ref, ng_ref, jmat_ref,
                   x_ref, gate_ref, prev_ref, next_ref, cw_ref, cb_ref, wa_ref, ba_ref, wx_ref, bx_ref, lam_ref, fzn_ref,
                   yg_ref, yr_ref, st_ref, ofwd_ref, upd_ref, h_ref, hfwd_ref, ok_ref, *, layer, tiles_per_batch):
    d = pl.program_id(1)
    i = pl.program_id(2)
    tile = _seq_tile(d, i, tiles_per_batch)

    @pl.when(i == 0)
    def _():
        st_ref[...] = jnp.zeros_like(st_ref)
        h_ref[...] = jnp.zeros_like(h_ref)
        ok_ref[0] = 0

    factored = ok_ref[0] == 1
    run, finish_hg, decay_ok = _hgrn_body(lbl_ref, hq_ref, fz_ref, hi_ref, hg_ref, ng_ref, jmat_ref, st_ref, upd_ref,
                                          layer=layer)
    scan, finish_rg = _rglru_body(x_ref, gate_ref, prev_ref, next_ref, cw_ref, cb_ref, wa_ref, ba_ref, wx_ref, bx_ref,
                                  lam_ref, h_ref, tile=tile, tiles_per_batch=tiles_per_batch)

    for fac in (True, False):
        chosen = factored if fac else jnp.logical_not(factored)

        @pl.when(jnp.logical_and(d == 0, chosen))
        def _():
            ofwd_ref[tile] = run(False, fac)
            hfwd_ref[tile] = scan(False)

        @pl.when(jnp.logical_and(d == 1, chosen))
        def _():
            yg_ref[...] = finish_hg(run(True, fac) + ofwd_ref[tile])
            yr_ref[...] = finish_rg(scan(True) + hfwd_ref[tile])

    ok_ref[0] = decay_ok(fzn_ref[...]).astype(jnp.int32)


def _mixers(ph, pr, lb_logits, norm_g_all, conv_w, conv_b, wa, ba, wx, bx, lam, *, layer, bsz, tiles_per_batch):
    n = ph.shape[0]
    depth = lb_logits.shape[0]
    hw = norm_g_all.shape[-1]
    rw = conv_b.shape[-1]
    head = jnp.arange(hw) // HG_DK
    same = head[:, None] == head[None, :]
    lt = 2 * HG_DK
    ntile = hw // lt
    hpt = TILE // HALO

    def seq_row(b, d, i):
        return _seq_row(b, _seq_tile(d, i, tiles_per_batch), bsz, tiles_per_batch)

    def col(cfn):
        return lambda b, d, i: (seq_row(b, d, i), cfn(d))

    def out_row(b, d, i):
        tile = jnp.where(d == 0, 0, _seq_tile(d, i, tiles_per_batch))
        return (_seq_row(b, tile, bsz, tiles_per_batch), 0)

    per_dir = lambda b, d, i: (layer, d, 0, 0)
    return pl.pallas_call(
        functools.partial(_mixers_kernel, layer=layer, tiles_per_batch=tiles_per_batch),
        grid=(bsz, 2, tiles_per_batch + 1),
        in_specs=[
            pl.BlockSpec((depth, 1, 1, hw), lambda b, d, i: (0, d, 0, 0)),
            pl.BlockSpec((TILE, hw), col(lambda d: 0)),
            pl.BlockSpec((TILE, hw), col(lambda d: 1 + d)),
            pl.BlockSpec((TILE, hw), col(lambda d: 3)),
            pl.BlockSpec((TILE, hw), col(lambda d: 4)),
            _layer_spec(norm_g_all, layer), _const_spec((hw, hw)),
            pl.BlockSpec((TILE, rw), col(lambda d: 0)),
            pl.BlockSpec((TILE, rw), col(lambda d: 1)),
            pl.BlockSpec((HALO, rw), lambda b, d, i: (jnp.maximum(seq_row(b, d, i) * hpt - 1, 0), 0)),
            pl.BlockSpec((HALO, rw), lambda b, d, i: (jnp.minimum((seq_row(b, d, i) + 1) * hpt, n // HALO - 1), 0)),
            _layer_spec(conv_w, layer), _layer_spec(conv_b, layer),
            pl.BlockSpec((None, 1, rw, rw), per_dir), pl.BlockSpec((None, 1, 1, rw), per_dir),
            pl.BlockSpec((None, 1, rw, rw), per_dir), pl.BlockSpec((None, 1, 1, rw), per_dir),
            pl.BlockSpec((None, 1, 1, rw), per_dir),
            pl.BlockSpec((TILE, hw), lambda b, d, i: (seq_row(b, d, jnp.minimum(i + 1, tiles_per_batch)), 1 + d)),
        ],
        out_specs=[pl.BlockSpec((TILE, hw), out_row), pl.BlockSpec((TILE, rw), out_row)],
        out_shape=[jax.ShapeDtypeStruct((n, hw), BF16), jax.ShapeDtypeStruct((n, rw), BF16)],
        scratch_shapes=[pltpu.VMEM((ntile, lt, lt), F32), pltpu.VMEM((tiles_per_batch + 1, TILE, hw), F32),
                        pltpu.VMEM((TILE // HG_CHUNK, ntile, lt, lt), F32),
                        pltpu.VMEM((SUBLANES, rw), F32), pltpu.VMEM((tiles_per_batch + 1, TILE, rw), F32),
                        pltpu.SMEM((1,), jnp.int32)],
        compiler_params=_cparams(("arbitrary", "arbitrary", "arbitrary")),
        name="mixers",
    )(lb_logits.reshape(depth, 2, 1, hw), ph, ph, ph, ph, norm_g_all, same.astype(BF16),
      pr, pr, pr, pr, conv_w, conv_b, wa, ba, wx, bx, lam, ph)


def _rope_tables(length):
    n_freq = HEAD_DIM // 4
    rows = length // GRID_W
    row = np.repeat(np.arange(rows), GRID_W).astype(np.float32)
    col = np.tile(np.arange(GRID_W), rows).astype(np.float32)
    inv_freq = (np.float32(ROPE_BASE) ** (-np.arange(n_freq, dtype=np.float32) / np.float32(n_freq))).astype(np.float32)
    ang = np.stack([row[:, None] * inv_freq, col[:, None] * inv_freq], axis=1)
    cos, sin = np.cos(ang), np.sin(ang)
    zero = np.zeros_like(sin[:, 0])
    cos_h = np.concatenate([cos[:, 0], cos[:, 0], cos[:, 1], cos[:, 1]], axis=-1)
    sa_h = np.concatenate([-sin[:, 0], zero, -sin[:, 1], zero], axis=-1)
    sb_h = np.concatenate([zero, sin[:, 0], zero, sin[:, 1]], axis=-1)
    ctx = (np.ones((TOKEN_TILE, HEAD_DIM), np.float32), np.zeros((TOKEN_TILE, HEAD_DIM), np.float32),
           np.zeros((TOKEN_TILE, HEAD_DIM), np.float32))
    return tuple(jnp.asarray(np.tile(np.concatenate([c, t], axis=0), (1, 2))) for c, t in zip(ctx, (cos_h, sa_h, sb_h)))


def _block_diag(wb):
    nb, bw = wb.shape[-3], wb.shape[-1]
    out = jnp.zeros(wb.shape[:-3] + (nb * bw, nb * bw), wb.dtype)
    for j in range(nb):
        out = out.at[..., j * bw:(j + 1) * bw, j * bw:(j + 1) * bw].set(wb[..., j, :, :])
    return out


def _regroup_heads(t, axis):
    shp = t.shape
    t = t.reshape(shp[:axis] + (ATT_KV_HEADS, ATT_GROUP, HEAD_DIM) + shp[axis + 1:])
    return jnp.swapaxes(t, axis, axis + 1).reshape(shp)


def kernel(x, c, ctx, c_ctx, ada_w, ada_b, norm_ffn1, ffn1_w_in, ffn1_w_out, norm_mix, w_in, w_out, attn_sink, hg_lb_logits, hg_norm, rg_conv_w, rg_conv_b, rg_gate_a_w, rg_gate_a_b, rg_gate_x_w, rg_gate_x_b, rg_lambda, norm_ffn2, ffn2_w_in, ffn2_w_out, final_norm):
    bsz, length, d = x.shape
    depth = ada_w.shape[0]
    assert ctx.shape[1] == CTX_LEN == TILE and length % TOKEN_TILE == 0 and (bsz * CTX_LEN) % TOKEN_TILE == 0
    tpb = length // TILE
    geo = dict(n_ctx_tiles=bsz * CTX_LEN // TOKEN_TILE, tiles_per_batch=length // TOKEN_TILE)

    ctx2, x2 = ctx.reshape(bsz * CTX_LEN, d), x.reshape(bsz * length, d)
    cond = jnp.zeros((SUBLANES, d), F32).at[:bsz].set(c).at[bsz].set(c_ctx)
    mods_all = _adaln(cond, ada_w, ada_b).reshape(depth, SUBLANES, N_MODS, d)[:, :bsz + 1]
    tabs = _rope_tables(length)

    att_w = w_out.shape[1] // 2
    kv_w = ATT_KV_HEADS * HEAD_DIM
    hg_w = hg_norm.shape[1]
    rg_w = rg_conv_b.shape[1]
    widths = (att_w, kv_w, 5 * hg_w, 2 * rg_w)
    assert sum(widths) + kv_w == w_in.shape[2]

    to16 = lambda t: t.astype(BF16)
    f1_in, f1_out, f2_in, f2_out = map(to16, (ffn1_w_in, ffn1_w_out, ffn2_w_in, ffn2_w_out))
    wi16 = to16(jnp.concatenate([_regroup_heads(w_in[:, :, :att_w], 2), w_in[:, :, att_w:]], axis=2))
    wo16 = to16(jnp.concatenate([_regroup_heads(w_out[:, :att_w], 1), w_out[:, att_w:]], axis=1))
    wa_bd, wx_bd = to16(_block_diag(rg_gate_a_w)), to16(_block_diag(rg_gate_x_w))
    as_row = lambda t: t.reshape(t.shape[:-1] + (1, t.shape[-1]))
    g1, gm, g2, hgn = map(as_row, (norm_ffn1, norm_mix, norm_ffn2, hg_norm))
    conv_b, ba, bx, lam = map(as_row, (rg_conv_b, rg_gate_a_b, rg_gate_x_b, rg_lambda))

    xs = (ctx2, x2)
    for l in range(depth):
        last = l == depth - 1
        xs, q, k, v, ph, pr = _token_block(xs, mods_all, g1, f1_in, f1_out, layer=l, mod0=0,
                                           proj=(gm, wi16, tabs, widths), **geo)
        yg, yr = _mixers(ph, pr, hg_lb_logits, hgn, rg_conv_w, conv_b, wa_bd, ba, wx_bd, bx, lam,
                         layer=l, bsz=bsz, tiles_per_batch=tpb)
        xs, = _token_block(xs, mods_all, g2, f2_in, f2_out, layer=l, mod0=6,
                           mix=(q, k, v, attn_sink, yg, yr, wo16, bsz, length),
                           final_g=final_norm if last else None, latent_only=last, **geo)
    return xs.reshape(bsz, length, d)
```

```python
import functools
import math

import jax
import jax.numpy as jnp
import numpy as np
from jax import lax
from jax.experimental import pallas as pl
from jax.experimental.pallas import tpu as pltpu

F32 = jnp.float32
BF16 = jnp.bfloat16

GRID_W = 64
CTX_LEN = 256
N_MODS = 9
EPS = 1e-6
HEAD_DIM = 64
ATT_KV_HEADS = 2
ATT_GROUP = 4
WINDOW = 128
ATT_BLOCK = 128
ATT_QBLOCKS = 2
ROPE_BASE = 10000.0
LOG2E = 1.4426950408889634
HG_DK = 64
HG_CHUNK = 16
HG_FAST_CHUNK = 32
HG_MIN_DECAY = 1e-30
RG_C = 8.0
assert ATT_KV_HEADS == 2

TILE = 256
TOKEN_TILE = 512
SUBLANES = 8
HALO = SUBLANES
ADALN_COL_BLOCKS = 8
VMEM_LIMIT = 52 * 1024 * 1024


def _cparams(sem):
    return pltpu.CompilerParams(dimension_semantics=sem, vmem_limit_bytes=VMEM_LIMIT)


def _const_spec(shape):
    nd = len(shape)
    return pl.BlockSpec(shape, lambda *_: (0,) * nd, pipeline_mode=pl.Buffered(1))


def _layer_spec(arr, l):
    nd = arr.ndim
    return pl.BlockSpec((None,) + arr.shape[1:], lambda *_: (l,) + (0,) * (nd - 1), pipeline_mode=pl.Buffered(1))


def _rms(xf, g):
    return xf * lax.rsqrt(jnp.mean(xf * xf, axis=-1, keepdims=True) + EPS) * g


def _dot(a, b):
    return jnp.dot(a, b, preferred_element_type=F32)


def _dot_nt(a, b):
    return lax.dot_general(a, b, (((1,), (1,)), ((), ())), preferred_element_type=F32)


def _dot_tn(a, b):
    return lax.dot_general(a, b, (((0,), (0,)), ((), ())), preferred_element_type=F32)


def _adaln_kernel(c_ref, w_ref, b_ref, o_ref):
    s = jax.nn.silu(c_ref[...]).astype(BF16)
    o_ref[0] = _dot(s, w_ref[0].astype(BF16)) + b_ref[0]


def _adaln(cond, ada_w, ada_b):
    depth, d, n = ada_w.shape
    tn = n // ADALN_COL_BLOCKS
    return pl.pallas_call(
        _adaln_kernel,
        grid=(depth, n // tn),
        in_specs=[
            pl.BlockSpec(cond.shape, lambda l, j: (0, 0)),
            pl.BlockSpec((1, d, tn), lambda l, j: (l, 0, j)),
            pl.BlockSpec((1, 1, tn), lambda l, j: (l, 0, j)),
        ],
        out_specs=pl.BlockSpec((1, cond.shape[0], tn), lambda l, j: (l, 0, j)),
        out_shape=jax.ShapeDtypeStruct((depth, cond.shape[0], n), F32),
        compiler_params=_cparams(("arbitrary", "arbitrary")),
        name="adaln",
    )(cond, ada_w, ada_b.reshape(depth, 1, n))


def _attend(sink_ref, layer, q_blocks, k_loc, v_loc, k_ctx, v_ctx, n_blocks, length):
    sw = k_ctx[0].shape[1]
    nslab = q_blocks[0].shape[1] // sw

    def only(t, kv, fill=0.0):
        lane = lax.broadcasted_iota(jnp.int32, t.shape, 1)
        return jnp.where((lane >= kv * HEAD_DIM) & (lane < (kv + 1) * HEAD_DIM), t, jnp.full_like(t, fill))

    nq = len(q_blocks)
    chains = [(j, kv) for j in range(nq) for kv in range(ATT_KV_HEADS)]
    q, valid = [], []
    for j in range(nq):
        q.append(jnp.concatenate([q_blocks[j][:, g * sw:(g + 1) * sw] for g in range(nslab)], axis=0))
        n = n_blocks[j]
        k_pos = (n - 1) * ATT_BLOCK + lax.broadcasted_iota(jnp.int32, (3 * ATT_BLOCK, ATT_BLOCK), 0)
        q_pos = n * ATT_BLOCK + lax.broadcasted_iota(jnp.int32, (3 * ATT_BLOCK, ATT_BLOCK), 1)
        ok = (jnp.abs(k_pos - q_pos) <= WINDOW) & (k_pos >= 0) & (k_pos < length) & (n >= 0)
        valid.append(jnp.concatenate([ok] * nslab, axis=1))
    sink = {kv: jnp.concatenate([jnp.full((1, ATT_BLOCK), sink_ref[layer, kv * ATT_GROUP + g] * LOG2E, F32)
                                 for g in range(nslab)], axis=1) for kv in range(ATT_KV_HEADS)}
    s_loc = {c: jnp.where(valid[c[0]], _dot_nt(only(k_loc[c[0]], c[1]), q[c[0]]), -jnp.inf) for c in chains}
    s_ctx = {c: _dot_nt(only(k_ctx[c[0]], c[1]), q[c[0]]) for c in chains}
    m = {c: jnp.maximum(jnp.maximum(s_loc[c].max(0, keepdims=True), s_ctx[c].max(0, keepdims=True)), sink[c[1]])
         for c in chains}
    p_loc = {c: jnp.exp2(s_loc[c] - m[c]).astype(BF16) for c in chains}
    p_ctx = {c: jnp.exp2(s_ctx[c] - m[c]).astype(BF16) for c in chains}
    o = {c: (_dot_tn(only(v_loc[c[0]], c[1], 1.0), p_loc[c]) + _dot_tn(only(v_ctx[c[0]], c[1], 1.0), p_ctx[c]))
         for c in chains}
    row_kv = lax.broadcasted_iota(jnp.int32, (sw, nslab * ATT_BLOCK), 0) // HEAD_DIM
    outs = []
    for j in range(nq):
        scaled = []
        for kv in range(ATT_KV_HEADS):
            c = (j, kv)
            other = (1 - kv) * HEAD_DIM
            denom = o[c][other:other + 1, :] + jnp.exp2(sink[kv] - m[c])
            scaled.append(o[c] / denom)
        out = jnp.where(row_kv == 0, scaled[0], scaled[1]).T
        outs.append(jnp.concatenate([out[g * ATT_BLOCK:(g + 1) * ATT_BLOCK] for g in range(nslab)],
                                    axis=1).astype(BF16))
    return outs


def _rope(t, cos, sa, sb):
    w = t.shape[1]
    rep = w // cos.shape[1]
    c, a, b = (jnp.concatenate([m] * rep, axis=1) if rep > 1 else m for m in (cos, sa, sb))
    half = HEAD_DIM // 4
    return t * c + pltpu.roll(t, w - half, 1) * a + pltpu.roll(t, half, 1) * b


def _token_kernel(*refs, layer, mod0, n_ctx_tiles, tiles_per_batch, skip, length, two_src, has_mix, has_proj,
                  has_final):
    it = iter(refs)
    if two_src:
        c_ref = next(it)
    x_ref, mods_ref, g_ref, win_ref, wout_ref = (next(it) for _ in range(5))
    if has_mix:
        nloc = TOKEN_TILE // ATT_BLOCK + 2
        sink_ref, q_in_ref = next(it), next(it)
        nctx = TOKEN_TILE // CTX_LEN
        kc_refs, vc_refs = [next(it) for _ in range(nctx)], [next(it) for _ in range(nctx)]
        kl_refs, vl_refs = [next(it) for _ in range(nloc)], [next(it) for _ in range(nloc)]
        yg_ref, yr_ref, wmix_ref = (next(it) for _ in range(3))
    if has_proj:
        gp_ref, wp_ref, cos_ref, sa_ref, sb_ref = (next(it) for _ in range(5))
    if has_final:
        fin_ref = next(it)
    o_ref = next(it)
    if has_proj:
        q_ref, k_ref, v_ref, ph_ref, pr_ref = (next(it) for _ in range(5))

    mods = mods_ref[0]
    shift, scale, gate = mods[mod0:mod0 + 1], mods[mod0 + 1:mod0 + 2], mods[mod0 + 2:mod0 + 3]
    f = wout_ref.shape[0]
    ya_all = None
    if has_mix:
        lt = pl.program_id(0) + skip - n_ctx_tiles
        bpt = TOKEN_TILE // ATT_BLOCK
        n_first = jnp.where(lt >= 0, (lt % tiles_per_batch) * bpt, -bpt)
        ya_all = _attend(
            sink_ref, layer,
            [q_in_ref[g * ATT_BLOCK:(g + 1) * ATT_BLOCK, :] for g in range(bpt)],
            [jnp.concatenate([r[...] for r in kl_refs[g:g + 3]], axis=0) for g in range(bpt)],
            [jnp.concatenate([r[...] for r in vl_refs[g:g + 3]], axis=0) for g in range(bpt)],
            [kc_refs[g * ATT_BLOCK // TILE][...] for g in range(bpt)],
            [vc_refs[g * ATT_BLOCK // TILE][...] for g in range(bpt)],
            [n_first + g for g in range(bpt)], length)
    for r0 in range(0, x_ref.shape[0], TILE):
        rows = slice(r0, r0 + TILE)
        x = x_ref[rows, :]
        if two_src:
            x = jnp.where(pl.program_id(0) < n_ctx_tiles, c_ref[rows, :], x)
        if has_mix:
            g0 = r0 // ATT_BLOCK
            ya = ya_all[g0:g0 + TILE // ATT_BLOCK]
            y = jnp.concatenate([jnp.concatenate(ya, axis=0), yg_ref[rows, :], yr_ref[rows, :]], axis=-1)
            x = x + mods[5:6] * _dot(y, wmix_ref[...])
        h = (_rms(x, g_ref[...]) * (1.0 + scale) + shift).astype(BF16)
        a = _dot(h, win_ref[:, :f])
        b = _dot(h, win_ref[:, f:])
        u = (jax.nn.silu(a) * b).astype(BF16)
        out = x + (0.5 * gate) * _dot(u, wout_ref[...])
        o_ref[rows, :] = _rms(out, fin_ref[...]) if has_final else out
        if has_proj:
            h = (_rms(out, gp_ref[...]) * (1.0 + mods[4:5]) + mods[3:4]).astype(BF16)
            cos, sa, sb = cos_ref[rows, :], sa_ref[rows, :], sb_ref[rows, :]
            o = 0
            for ref, rotate, scl in ((q_ref, True, LOG2E * HEAD_DIM ** -0.5), (k_ref, True, None), (v_ref, False, None),
                                     (ph_ref, False, None), (pr_ref, False, None)):
                p = _dot(h, wp_ref[:, o:o + ref.shape[1]])
                o += ref.shape[1]
                if rotate:
                    p = _rope(p, cos, sa, sb)
                if scl is not None:
                    p = p * scl
                ref[rows, :] = p.astype(ref.dtype)


def _token_block(src, mods_all, g_all, w_in_all, w_out_all, *, layer, mod0, n_ctx_tiles, tiles_per_batch, ctx_row,
                 mix=None, proj=None, final_g=None, latent_only=False):
    two_src = isinstance(src, tuple)
    if two_src:
        ctx2, x2 = src
        n, d = ctx2.shape[0] + x2.shape[0], x2.shape[1]
    else:
        n, d = src.shape
    skip = n_ctx_tiles if latent_only else 0
    nt = n // TOKEN_TILE - skip
    row = lambda i: (i + skip, 0)

    def mod_row(i):
        j = i + skip
        return (layer, jnp.where(j < n_ctx_tiles, ctx_row, (j - n_ctx_tiles) // tiles_per_batch), 0, 0)

    if two_src:
        in_specs = [pl.BlockSpec((TOKEN_TILE, d), lambda i: (jnp.minimum(i, n_ctx_tiles - 1), 0)),
                    pl.BlockSpec((TOKEN_TILE, d), lambda i: (jnp.maximum(i - n_ctx_tiles, 0), 0))]
        args = [ctx2, x2]
    else:
        in_specs, args = [pl.BlockSpec((TOKEN_TILE, d), row)], [src]
    in_specs += [pl.BlockSpec((None, 1, N_MODS, d), mod_row), _layer_spec(g_all, layer),
                 _layer_spec(w_in_all, layer), _layer_spec(w_out_all, layer)]
    args += [mods_all, g_all, w_in_all, w_out_all]
    length = None
    if mix is not None:
        q, k, v, sink, yg, yr, w_mix_all, bsz, length = mix
        assert TOKEN_TILE == 2 * CTX_LEN and TILE == ATT_QBLOCKS * ATT_BLOCK and k.shape[1] == 2 * HEAD_DIM
        nblk = length // ATT_BLOCK
        blk0 = bsz * CTX_LEN // ATT_BLOCK
        bpt = TOKEN_TILE // ATT_BLOCK

        def lat(i):
            lt = i + skip - n_ctx_tiles
            return lt >= 0, jnp.where(lt >= 0, lt // tiles_per_batch, 0), jnp.where(lt >= 0, lt % tiles_per_batch, 0)

        def ctx_kv_row(h):
            def f(i):
                is_lat, b, _ = lat(i)
                return (jnp.where(is_lat, b, (TOKEN_TILE // CTX_LEN) * (i + skip) + h), 0)
            return f

        def loc_row(off):
            def f(i):
                _, b, jb = lat(i)
                return (blk0 + nblk * b + jnp.clip(jb * bpt + off, 0, nblk - 1), 0)
            return f

        cspec = [pl.BlockSpec((CTX_LEN, k.shape[1]), ctx_kv_row(h)) for h in range(TOKEN_TILE // CTX_LEN)]
        lspec = [pl.BlockSpec((ATT_BLOCK, k.shape[1]), loc_row(o)) for o in range(-1, bpt + 1)]
        in_specs += [pl.BlockSpec(memory_space=pltpu.SMEM), pl.BlockSpec((TOKEN_TILE, q.shape[1]), row)]
        in_specs += cspec + cspec + lspec + lspec
        in_specs += [pl.BlockSpec((TOKEN_TILE, yg.shape[1]), row), pl.BlockSpec((TOKEN_TILE, yr.shape[1]), row),
                     _layer_spec(w_mix_all, layer)]
        args += [sink, q, k, k, v, v] + [k] * len(lspec) + [v] * len(lspec) + [yg, yr, w_mix_all]
    out_specs = [pl.BlockSpec((TOKEN_TILE, d), lambda i: (i, 0))]
    out_shape = [jax.ShapeDtypeStruct((nt * TOKEN_TILE, d), F32)]
    if proj is not None:
        gp_all, wp_all, tabs, widths = proj

        def tab_row(i):
            return (jnp.where(i < n_ctx_tiles, 0, 1 + (i - n_ctx_tiles) % tiles_per_batch), 0)

        in_specs += [_layer_spec(gp_all, layer), _layer_spec(wp_all, layer)]
        in_specs += [pl.BlockSpec((TOKEN_TILE, tabs[0].shape[1]), tab_row)] * 3
        args += [gp_all, wp_all, *tabs]
        nq, nk, nh, nr = widths
        out_specs += [pl.BlockSpec((TOKEN_TILE, wd), lambda i: (i, 0)) for wd in (nq, nk, nk, nh, nr)]
        out_shape += [jax.ShapeDtypeStruct((n, wd), dt) for wd, dt in
                      ((nq, BF16), (nk, BF16), (nk, BF16), (nh, F32), (nr, F32))]
    if final_g is not None:
        in_specs.append(_const_spec((1, d)))
        args.append(final_g.reshape(1, d))
    return pl.pallas_call(
        functools.partial(_token_kernel, layer=layer, mod0=mod0, n_ctx_tiles=n_ctx_tiles,
                          tiles_per_batch=tiles_per_batch, skip=skip, length=length, two_src=two_src,
                          has_mix=mix is not None, has_proj=proj is not None, has_final=final_g is not None),
        grid=(nt,),
        in_specs=in_specs,
        out_specs=out_specs,
        out_shape=out_shape,
        compiler_params=_cparams(("arbitrary",)),
        name="ffn_out" if mix is not None else "ffn_in",
    )(*args)


def _seq_tile(d, i, tiles_per_batch):
    return jnp.where(d == 0, i, jnp.where(i == 0, 0, tiles_per_batch + 1 - i))


def _seq_row(b, tile, bsz, tiles_per_batch):
    return jnp.where(tile == 0, b, bsz + tiles_per_batch * b + tile - 1)


def _shift_rows(x, s, rev):
    if s == 0:
        return x
    return pltpu.roll(x, (x.shape[0] - s) if rev else s, 0)


def _hgrn_body(lbl_ref, hq_ref, fz_ref, hi_ref, hg_ref, ng_ref, jmat_ref, st_ref, upd_ref, *, layer):
    tt, w = hq_ref.shape

    logits = lbl_ref[:, 0, 0, :]
    e = jnp.exp(logits - logits.max(0, keepdims=True))
    lbp = e / e.sum(0, keepdims=True)
    lb = jnp.zeros((1, w), F32)
    for j in range(1, layer + 1):
        lb = lb + lbp[j:j + 1]

    q = jax.nn.silu(hq_ref[...])
    f = lb + (1.0 - lb) * jax.nn.sigmoid(fz_ref[...])
    kk = 1.0 - f
    v = hi_ref[...]
    jmat = jmat_ref[...]

    def run(rev, factored):
        c = HG_FAST_CHUNK if factored else HG_CHUNK
        pos = lax.broadcasted_iota(jnp.int32, (tt, w), 0) % c

        def in_chunk(s, r):
            return (pos <= c - 1 - s) if r else (pos >= s)

        def cumprod(y, r):
            s = 1
            while s < c:
                y = y * jnp.where(in_chunk(s, r), _shift_rows(y, s, r), 1.0)
                s *= 2
            return y

        p_in = cumprod(f, rev)
        half = c // 2
        nch = tt // c
        p3 = p_in.reshape(nch, c, w)
        dec_end = p3[:, 0:1, :] if rev else p3[:, c - 1:c, :]
        if factored:
            k_div = kk / p_in
            k_out = (k_div.reshape(nch, c, w) * dec_end).reshape(tt, w)
        else:
            k_out = kk * cumprod(jnp.where(in_chunk(1, not rev), _shift_rows(f, 1, not rev), 1.0), not rev)
        nhead = w // HG_DK
        lane_head = lax.broadcasted_iota(jnp.int32, (c, w), 1) // HG_DK

        def pairs_factored():
            x3 = (q * p_in).astype(BF16).reshape(nch, c, w)
            zero = jnp.zeros_like(x3)
            q_hat = jnp.concatenate([jnp.where(lane_head == h, x3, zero) for h in range(nhead)], axis=1)
            k_div16 = k_div.astype(BF16).reshape(nch, c, w)
            v3 = v.astype(BF16).reshape(nch, c, w)
            t_row = lax.broadcasted_iota(jnp.int32, (nhead * c, c), 0) % c
            s_col = lax.broadcasted_iota(jnp.int32, (nhead * c, c), 1)
            causal = (s_col >= t_row) if rev else (s_col <= t_row)
            outs = []
            for ci in range(nch):
                att = jnp.where(causal, _dot_nt(q_hat[ci], k_div16[ci]), 0.0)
                oc = _dot(att.astype(BF16), v3[ci])
                outs.append(sum(jnp.where(lane_head == h, oc[h * c:(h + 1) * c], 0.0) for h in range(nhead)))
            return jnp.concatenate(outs, axis=0)

        def pairs_direct():
            def halves(x):
                x4 = x.reshape(nch, 2, half, w)
                return (x4[:, 1], x4[:, 0]) if rev else (x4[:, 0], x4[:, 1])

            r8 = lax.broadcasted_iota(jnp.int32, (nch, half, w), 1)

            def rot(x, s):
                return x if s == 0 else pltpu.roll(x, (half - s) if rev else s, 1)

            def wrapped(s):
                return (r8 > half - 1 - s) if rev else (r8 < s)

            def pair_sum(wgt, vs):
                return (_dot(wgt.reshape(nch * half, w).astype(BF16), jmat) * vs.reshape(nch * half, w))

            (f_a, f_b), (v_a, v_b), (qd_a, qd_b) = halves(f), halves(v), halves(q)
            o_a = jnp.zeros((nch * half, w), F32)
            o_b = jnp.zeros((nch * half, w), F32)
            for delta in range(c):
                s = delta % half
                fr, vr = rot(f_a, s), rot(v_a, s)
                if delta < half:
                    fs_a = jnp.where(wrapped(s), 1.0, fr)
                    nxt_a = qd_a * fs_a
                    o_a = o_a + pair_sum(qd_a - nxt_a, vr)
                    qd_a = nxt_a
                    fs_b = jnp.where(wrapped(s), fr, rot(f_b, s))
                    vs_b = jnp.where(wrapped(s), vr, rot(v_b, s))
                else:
                    fs_b = jnp.where(wrapped(s), 1.0, fr)
                    vs_b = vr
                nxt_b = qd_b * fs_b
                o_b = o_b + pair_sum(qd_b - nxt_b, vs_b)
                qd_b = nxt_b
            o_a, o_b = o_a.reshape(nch, half, w), o_b.reshape(nch, half, w)
            return jnp.stack([o_b, o_a] if rev else [o_a, o_b], axis=1).reshape(tt, w)

        o = pairs_factored() if factored else pairs_direct()
        q_in = (q * p_in).astype(BF16)
        lane3 = lax.broadcasted_iota(jnp.int32, (nch, c, w), 2) % (2 * HG_DK)

        def per_head(x):
            x3 = x.astype(BF16).reshape(nch, c, w)
            zero = jnp.zeros_like(x3)
            return jnp.concatenate([jnp.where(lane3 < HG_DK, x3, zero), jnp.where(lane3 >= HG_DK, x3, zero)], axis=1)

        v_hat, k_hat = per_head(v), per_head(k_out)
        ntile = st_ref.shape[0]
        lt = w // ntile
        for ci in range(nch):
            for j in range(ntile):
                upd_ref[ci, j] = _dot_tn(v_hat[ci, :, j * lt:(j + 1) * lt], k_hat[ci, :, j * lt:(j + 1) * lt])
        st = [st_ref[j] for j in range(ntile)]
        parts = [None] * nch
        for ci in (range(nch - 1, -1, -1) if rev else range(nch)):
            parts[ci] = jnp.concatenate([_dot_nt(q_in[ci * c:(ci + 1) * c, j * lt:(j + 1) * lt], st[j].astype(BF16))
                                         for j in range(ntile)], axis=1)
            for j in range(ntile):
                st[j] = st[j] * dec_end[ci][:, j * lt:(j + 1) * lt] + upd_ref[ci, j]
        for j in range(ntile):
            st_ref[j] = st[j]
        return o + jnp.concatenate(parts, axis=0)

    def finish(tot):
        sq = tot * tot
        hi = sq.astype(BF16)
        lo = (sq - hi.astype(F32)).astype(BF16)
        ms = (_dot(hi, jmat) + _dot(lo, jmat)) * (1.0 / HG_DK)
        y = tot * lax.rsqrt(ms + EPS)
        return (y * ng_ref[...] * jax.nn.silu(hg_ref[...])).astype(BF16)

    def decay_ok(fz):
        fn = lb + (1.0 - lb) * jax.nn.sigmoid(fz)
        c = HG_FAST_CHUNK
        chunk_log = jnp.log(fn).reshape(tt // c, c, w).sum(axis=1)
        return jnp.min(chunk_log) >= math.log(HG_MIN_DECAY)

    return run, finish, decay_ok


def _rglru_body(x_ref, gate_ref, prev_ref, next_ref, cw_ref, cb_ref, wa_ref, ba_ref, wx_ref, bx_ref, lam_ref, h_ref,
                *, tile, tiles_per_batch):
    tt, w = x_ref.shape

    x = x_ref[...]
    has_prev = (tile >= 2).astype(F32)
    has_next = jnp.logical_and(tile >= 1, tile < tiles_per_batch).astype(F32)
    prev = prev_ref[...] * has_prev
    nxt = next_ref[...] * has_next
    row = lax.broadcasted_iota(jnp.int32, (tt, w), 0)
    xm1 = jnp.where(row == 0, prev[HALO - 1:HALO], _shift_rows(x, 1, False))
    xm2 = jnp.where(row == 0, prev[HALO - 2:HALO - 1],
                    jnp.where(row == 1, prev[HALO - 1:HALO], _shift_rows(x, 2, False)))
    xp1 = jnp.where(row == tt - 1, nxt[0:1], _shift_rows(x, 1, True))
    cw = cw_ref[...]
    u = cb_ref[...] + xm2 * cw[0:1] + xm1 * cw[1:2] + x * cw[2:3] + xp1 * cw[3:4]

    u16 = u.astype(BF16)
    r = jax.nn.sigmoid(_dot(u16, wa_ref[0]) + ba_ref[0])
    ig = jax.nn.sigmoid(_dot(u16, wx_ref[0]) + bx_ref[0])
    log_a = -RG_C * r * jax.nn.softplus(-lam_ref[0])
    a = jnp.exp(log_a)
    z = jnp.sqrt(jnp.tanh(-log_a) * (1.0 + a * a)) * (ig * u)

    def scan(rev):
        ng = tt // SUBLANES
        aa, zz = a.reshape(ng, SUBLANES, w), z.reshape(ng, SUBLANES, w)
        r8 = lax.broadcasted_iota(jnp.int32, (ng, SUBLANES, w), 1)
        s = 1
        while s < SUBLANES:
            ok = (r8 <= SUBLANES - 1 - s) if rev else (r8 >= s)
            sh = (SUBLANES - s) if rev else s
            zz = zz + jnp.where(ok, aa * pltpu.roll(zz, sh, 1), 0.0)
            aa = aa * jnp.where(ok, pltpu.roll(aa, sh, 1), 1.0)
            s *= 2
        carry = h_ref[0:1, :]
        groups = [None] * ng
        for gi in (range(ng - 1, -1, -1) if rev else range(ng)):
            groups[gi] = aa[gi] * carry + zz[gi]
            carry = groups[gi][0:1] if rev else groups[gi][SUBLANES - 1:SUBLANES]
        h_ref[...] = jnp.broadcast_to(carry, h_ref.shape)
        return jnp.concatenate(groups, axis=0)

    def finish(h):
        return (jax.nn.gelu(gate_ref[...]) * h).astype(BF16)

    return scan, finish


def _mixers_kernel(lbl_ref, hq_ref, fz_ref, hi_ref, hg_ref, ng_ref, jmat_ref,
                   x_ref, gate_ref, prev_ref, next_ref, cw_ref, cb_ref, wa_ref, ba_ref, wx_ref, bx_ref, lam_ref, fzn_ref,
                   yg_ref, yr_ref, st_ref, ofwd_ref, upd_ref, h_ref, hfwd_ref, ok_ref, *, layer, tiles_per_batch):
    d = pl.program_id(1)
    i = pl.program_id(2)
    tile = _seq_tile(d, i, tiles_per_batch)

    @pl.when(i == 0)
    def _():
        st_ref[...] = jnp.zeros_like(st_ref)
        h_ref[...] = jnp.zeros_like(h_ref)
        ok_ref[0] = 0

    factored = ok_ref[0] == 1
    run, finish_hg, decay_ok = _hgrn_body(lbl_ref, hq_ref, fz_ref, hi_ref, hg_ref, ng_ref, jmat_ref, st_ref, upd_ref,
                                          layer=layer)
    scan, finish_rg = _rglru_body(x_ref, gate_ref, prev_ref, next_ref, cw_ref, cb_ref, wa_ref, ba_ref, wx_ref, bx_ref,
                                  lam_ref, h_ref, tile=tile, tiles_per_batch=tiles_per_batch)

    for fac in (True, False):
        chosen = factored if fac else jnp.logical_not(factored)

        @pl.when(jnp.logical_and(d == 0, chosen))
        def _():
            ofwd_ref[tile] = run(False, fac)
            hfwd_ref[tile] = scan(False)

        @pl.when(jnp.logical_and(d == 1, chosen))
        def _():
            yg_ref[...] = finish_hg(run(True, fac) + ofwd_ref[tile])
            yr_ref[...] = finish_rg(scan(True) + hfwd_ref[tile])

    ok_ref[0] = decay_ok(fzn_ref[...]).astype(jnp.int32)


def _mixers(ph, pr, lb_logits, norm_g_all, conv_w, conv_b, wa, ba, wx, bx, lam, *, layer, bsz, tiles_per_batch):
    n = ph.shape[0]
    depth = lb_logits.shape[0]
    hw = norm_g_all.shape[-1]
    rw = conv_b.shape[-1]
    head = jnp.arange(hw) // HG_DK
    same = head[:, None] == head[None, :]
    lt = 2 * HG_DK
    ntile = hw // lt
    hpt = TILE // HALO

    def seq_row(b, d, i):
        return _seq_row(b, _seq_tile(d, i, tiles_per_batch), bsz, tiles_per_batch)

    def col(cfn):
        return lambda b, d, i: (seq_row(b, d, i), cfn(d))

    def out_row(b, d, i):
        tile = jnp.where(d == 0, 0, _seq_tile(d, i, tiles_per_batch))
        return (_seq_row(b, tile, bsz, tiles_per_batch), 0)

    per_dir = lambda b, d, i: (layer, d, 0, 0)
    return pl.pallas_call(
        functools.partial(_mixers_kernel, layer=layer, tiles_per_batch=tiles_per_batch),
        grid=(bsz, 2, tiles_per_batch + 1),
        in_specs=[
            pl.BlockSpec((depth, 1, 1, hw), lambda b, d, i: (0, d, 0, 0)),
            pl.BlockSpec((TILE, hw), col(lambda d: 0)),
            pl.BlockSpec((TILE, hw), col(lambda d: 1 + d)),
            pl.BlockSpec((TILE, hw), col(lambda d: 3)),
            pl.BlockSpec((TILE, hw), col(lambda d: 4)),
            _layer_spec(norm_g_all, layer), _const_spec((hw, hw)),
            pl.BlockSpec((TILE, rw), col(lambda d: 0)),
            pl.BlockSpec((TILE, rw), col(lambda d: 1)),
            pl.BlockSpec((HALO, rw), lambda b, d, i: (jnp.maximum(seq_row(b, d, i) * hpt - 1, 0), 0)),
            pl.BlockSpec((HALO, rw), lambda b, d, i: (jnp.minimum((seq_row(b, d, i) + 1) * hpt, n // HALO - 1), 0)),
            _layer_spec(conv_w, layer), _layer_spec(conv_b, layer),
            pl.BlockSpec((None, 1, rw, rw), per_dir), pl.BlockSpec((None, 1, 1, rw), per_dir),
            pl.BlockSpec((None, 1, rw, rw), per_dir), pl.BlockSpec((None, 1, 1, rw), per_dir),
            pl.BlockSpec((None, 1, 1, rw), per_dir),
            pl.BlockSpec((TILE, hw), lambda b, d, i: (seq_row(b, d, jnp.minimum(i + 1, tiles_per_batch)), 1 + d)),
        ],
        out_specs=[pl.BlockSpec((TILE, hw), out_row), pl.BlockSpec((TILE, rw), out_row)],
        out_shape=[jax.ShapeDtypeStruct((n, hw), BF16), jax.ShapeDtypeStruct((n, rw), BF16)],
        scratch_shapes=[pltpu.VMEM((ntile, lt, lt), F32), pltpu.VMEM((tiles_per_batch + 1, TILE, hw), F32),
                        pltpu.VMEM((TILE // HG_CHUNK, ntile, lt, lt), F32),
                        pltpu.VMEM((SUBLANES, rw), F32), pltpu.VMEM((tiles_per_batch + 1, TILE, rw), F32),
                        pltpu.SMEM((1,), jnp.int32)],
        compiler_params=_cparams(("arbitrary", "arbitrary", "arbitrary")),
        name="mixers",
    )(lb_logits.reshape(depth, 2, 1, hw), ph, ph, ph, ph, norm_g_all, same.astype(BF16),
      pr, pr, pr, pr, conv_w, conv_b, wa, ba, wx, bx, lam, ph)


def _rope_tables(length):
    n_freq = HEAD_DIM // 4
    rows = length // GRID_W
    row = np.repeat(np.arange(rows), GRID_W).astype(np.float32)
    col = np.tile(np.arange(GRID_W), rows).astype(np.float32)
    inv_freq = (np.float32(ROPE_BASE) ** (-np.arange(n_freq, dtype=np.float32) / np.float32(n_freq))).astype(np.float32)
    ang = np.stack([row[:, None] * inv_freq, col[:, None] * inv_freq], axis=1)
    cos, sin = np.cos(ang), np.sin(ang)
    zero = np.zeros_like(sin[:, 0])
    cos_h = np.concatenate([cos[:, 0], cos[:, 0], cos[:, 1], cos[:, 1]], axis=-1)
    sa_h = np.concatenate([-sin[:, 0], zero, -sin[:, 1], zero], axis=-1)
    sb_h = np.concatenate([zero, sin[:, 0], zero, sin[:, 1]], axis=-1)
    ctx = (np.ones((TOKEN_TILE, HEAD_DIM), np.float32), np.zeros((TOKEN_TILE, HEAD_DIM), np.float32),
           np.zeros((TOKEN_TILE, HEAD_DIM), np.float32))
    return tuple(jnp.asarray(np.tile(np.concatenate([c, t], axis=0), (1, 2))) for c, t in zip(ctx, (cos_h, sa_h, sb_h)))


def _block_diag(wb):
    nb, bw = wb.shape[-3], wb.shape[-1]
    out = jnp.zeros(wb.shape[:-3] + (nb * bw, nb * bw), wb.dtype)
    for j in range(nb):
        out = out.at[..., j * bw:(j + 1) * bw, j * bw:(j + 1) * bw].set(wb[..., j, :, :])
    return out


def _regroup_heads(t, axis):
    shp = t.shape
    t = t.reshape(shp[:axis] + (ATT_KV_HEADS, ATT_GROUP, HEAD_DIM) + shp[axis + 1:])
    return jnp.swapaxes(t, axis, axis + 1).reshape(shp)


def kernel(x, c, ctx, c_ctx, ada_w, ada_b, norm_ffn1, ffn1_w_in, ffn1_w_out, norm_mix, w_in, w_out, attn_sink, hg_lb_logits, hg_norm, rg_conv_w, rg_conv_b, rg_gate_a_w, rg_gate_a_b, rg_gate_x_w, rg_gate_x_b, rg_lambda, norm_ffn2, ffn2_w_in, ffn2_w_out, final_norm):
    bsz, length, d = x.shape
    depth = ada_w.shape[0]
    assert ctx.shape[1] == CTX_LEN == TILE and length % TOKEN_TILE == 0 and (bsz * CTX_LEN) % TOKEN_TILE == 0
    tpb = length // TILE
    geo = dict(n_ctx_tiles=bsz * CTX_LEN // TOKEN_TILE, tiles_per_batch=length // TOKEN_TILE, ctx_row=bsz)

    ctx2, x2 = ctx.reshape(bsz * CTX_LEN, d), x.reshape(bsz * length, d)
    assert bsz + 1 <= SUBLANES
    cond = jnp.concatenate([c, c_ctx[None], jnp.zeros((SUBLANES - bsz - 1, d), F32)], axis=0)
    mods_all = _adaln(cond, ada_w, ada_b).reshape(depth, SUBLANES, N_MODS, d)
    tabs = _rope_tables(length)

    att_w = w_out.shape[1] // 2
    kv_w = ATT_KV_HEADS * HEAD_DIM
    hg_w = hg_norm.shape[1]
    rg_w = rg_conv_b.shape[1]
    widths = (att_w, kv_w, 5 * hg_w, 2 * rg_w)
    assert sum(widths) + kv_w == w_in.shape[2]

    to16 = lambda t: t.astype(BF16)
    f1_in, f1_out, f2_in, f2_out = map(to16, (ffn1_w_in, ffn1_w_out, ffn2_w_in, ffn2_w_out))
    wi16, wo16 = to16(w_in), to16(w_out)
    wi16 = wi16.at[:, :, :att_w].set(_regroup_heads(wi16[:, :, :att_w], 2))
    wo16 = wo16.at[:, :att_w].set(_regroup_heads(wo16[:, :att_w], 1))
    wa_bd, wx_bd = to16(_block_diag(rg_gate_a_w)), to16(_block_diag(rg_gate_x_w))
    as_row = lambda t: t.reshape(t.shape[:-1] + (1, t.shape[-1]))
    g1, gm, g2, hgn = map(as_row, (norm_ffn1, norm_mix, norm_ffn2, hg_norm))
    conv_b, ba, bx, lam = map(as_row, (rg_conv_b, rg_gate_a_b, rg_gate_x_b, rg_lambda))

    xs = (ctx2, x2)
    for l in range(depth):
        last = l == depth - 1
        xs, q, k, v, ph, pr = _token_block(xs, mods_all, g1, f1_in, f1_out, layer=l, mod0=0,
                                           proj=(gm, wi16, tabs, widths), **geo)
        yg, yr = _mixers(ph, pr, hg_lb_logits, hgn, rg_conv_w, conv_b, wa_bd, ba, wx_bd, bx, lam,
                         layer=l, bsz=bsz, tiles_per_batch=tpb)
        xs, = _token_block(xs, mods_all, g2, f2_in, f2_out, layer=l, mod0=6,
                           mix=(q, k, v, attn_sink, yg, yr, wo16, bsz, length),
                           final_g=final_norm if last else None, latent_only=last, **geo)
    return xs.reshape(bsz, length, d)
```

```python
import functools
import math

import jax
import jax.numpy as jnp
import numpy as np
from jax import lax
from jax.experimental import pallas as pl
from jax.experimental.pallas import tpu as pltpu

F32 = jnp.float32
BF16 = jnp.bfloat16

GRID_W = 64
CTX_LEN = 256
N_MODS = 9
EPS = 1e-6
HEAD_DIM = 64
ATT_KV_HEADS = 2
ATT_GROUP = 4
WINDOW = 128
ATT_BLOCK = 128
ATT_QBLOCKS = 2
ROPE_BASE = 10000.0
LOG2E = 1.4426950408889634
HG_DK = 64
HG_CHUNK = 16
HG_FAST_CHUNK = 32
HG_MIN_DECAY = 1e-30
RG_C = 8.0
assert ATT_KV_HEADS == 2

TILE = 256
TOKEN_TILE = 512
SUBLANES = 8
HALO = SUBLANES
CAST_LANES = 128
ADALN_COL_BLOCKS = 8
VMEM_LIMIT = 52 * 1024 * 1024


def _cparams(sem):
    return pltpu.CompilerParams(dimension_semantics=sem, vmem_limit_bytes=VMEM_LIMIT)


def _const_spec(shape):
    nd = len(shape)
    return pl.BlockSpec(shape, lambda *_: (0,) * nd, pipeline_mode=pl.Buffered(1))


def _layer_spec(arr, l):
    nd = arr.ndim
    return pl.BlockSpec((None,) + arr.shape[1:], lambda *_: (l,) + (0,) * (nd - 1), pipeline_mode=pl.Buffered(1))


def _rms(xf, g):
    return xf * lax.rsqrt(jnp.mean(xf * xf, axis=-1, keepdims=True) + EPS) * g


def _dot(a, b):
    return jnp.dot(a, b, preferred_element_type=F32)


def _dot_nt(a, b):
    return lax.dot_general(a, b, (((1,), (1,)), ((), ())), preferred_element_type=F32)


def _dot_tn(a, b):
    return lax.dot_general(a, b, (((0,), (0,)), ((), ())), preferred_element_type=F32)


def _adaln_kernel(c_ref, w_ref, b_ref, o_ref):
    s = jax.nn.silu(c_ref[...]).astype(BF16)
    o_ref[0] = _dot(s, w_ref[0].astype(BF16)) + b_ref[0]


def _adaln(cond, ada_w, ada_b):
    depth, d, n = ada_w.shape
    tn = n // ADALN_COL_BLOCKS
    return pl.pallas_call(
        _adaln_kernel,
        grid=(depth, n // tn),
        in_specs=[
            pl.BlockSpec(cond.shape, lambda l, j: (0, 0)),
            pl.BlockSpec((1, d, tn), lambda l, j: (l, 0, j)),
            pl.BlockSpec((1, 1, tn), lambda l, j: (l, 0, j)),
        ],
        out_specs=pl.BlockSpec((1, cond.shape[0], tn), lambda l, j: (l, 0, j)),
        out_shape=jax.ShapeDtypeStruct((depth, cond.shape[0], n), F32),
        compiler_params=_cparams(("arbitrary", "arbitrary")),
        name="adaln",
    )(cond, ada_w, ada_b.reshape(depth, 1, n))


def _attend(sink_ref, layer, q_blocks, k_loc, v_loc, k_ctx, v_ctx, n_blocks, length):
    sw = k_ctx[0].shape[1]
    nslab = q_blocks[0].shape[1] // sw

    def only(t, kv, fill=0.0):
        lane = lax.broadcasted_iota(jnp.int32, t.shape, 1)
        return jnp.where((lane >= kv * HEAD_DIM) & (lane < (kv + 1) * HEAD_DIM), t, jnp.full_like(t, fill))

    nq = len(q_blocks)
    chains = [(j, kv) for j in range(nq) for kv in range(ATT_KV_HEADS)]
    q, valid = [], []
    for j in range(nq):
        q.append(jnp.concatenate([q_blocks[j][:, g * sw:(g + 1) * sw] for g in range(nslab)], axis=0))
        n = n_blocks[j]
        k_pos = (n - 1) * ATT_BLOCK + lax.broadcasted_iota(jnp.int32, (3 * ATT_BLOCK, ATT_BLOCK), 0)
        q_pos = n * ATT_BLOCK + lax.broadcasted_iota(jnp.int32, (3 * ATT_BLOCK, ATT_BLOCK), 1)
        ok = (jnp.abs(k_pos - q_pos) <= WINDOW) & (k_pos >= 0) & (k_pos < length) & (n >= 0)
        valid.append(jnp.concatenate([ok] * nslab, axis=1))
    sink = {kv: jnp.concatenate([jnp.full((1, ATT_BLOCK), sink_ref[layer, kv * ATT_GROUP + g] * LOG2E, F32)
                                 for g in range(nslab)], axis=1) for kv in range(ATT_KV_HEADS)}
    s_loc = {c: jnp.where(valid[c[0]], _dot_nt(only(k_loc[c[0]], c[1]), q[c[0]]), -jnp.inf) for c in chains}
    s_ctx = {c: _dot_nt(only(k_ctx[c[0]], c[1]), q[c[0]]) for c in chains}
    m = {c: jnp.maximum(jnp.maximum(s_loc[c].max(0, keepdims=True), s_ctx[c].max(0, keepdims=True)), sink[c[1]])
         for c in chains}
    p_loc = {c: jnp.exp2(s_loc[c] - m[c]).astype(BF16) for c in chains}
    p_ctx = {c: jnp.exp2(s_ctx[c] - m[c]).astype(BF16) for c in chains}
    o = {c: (_dot_tn(only(v_loc[c[0]], c[1], 1.0), p_loc[c]) + _dot_tn(only(v_ctx[c[0]], c[1], 1.0), p_ctx[c]))
         for c in chains}
    row_kv = lax.broadcasted_iota(jnp.int32, (sw, nslab * ATT_BLOCK), 0) // HEAD_DIM
    outs = []
    for j in range(nq):
        scaled = []
        for kv in range(ATT_KV_HEADS):
            c = (j, kv)
            other = (1 - kv) * HEAD_DIM
            denom = o[c][other:other + 1, :] + jnp.exp2(sink[kv] - m[c])
            scaled.append(o[c] / denom)
        out = jnp.where(row_kv == 0, scaled[0], scaled[1]).T
        outs.append(jnp.concatenate([out[g * ATT_BLOCK:(g + 1) * ATT_BLOCK] for g in range(nslab)],
                                    axis=1).astype(BF16))
    return outs


def _rope(t, cos, sa, sb):
    w = t.shape[1]
    rep = w // cos.shape[1]
    c, a, b = (jnp.concatenate([m] * rep, axis=1) if rep > 1 else m for m in (cos, sa, sb))
    half = HEAD_DIM // 4
    return t * c + pltpu.roll(t, w - half, 1) * a + pltpu.roll(t, half, 1) * b


def _token_kernel(*refs, layer, mod0, n_ctx_tiles, tiles_per_batch, skip, length, two_src, has_mix, has_proj,
                  has_final):
    it = iter(refs)
    if two_src:
        c_ref = next(it)
    x_ref, mods_ref, g_ref, win_ref, wout_ref = (next(it) for _ in range(5))
    if has_mix:
        nloc = TOKEN_TILE // ATT_BLOCK + 2
        sink_ref, q_in_ref = next(it), next(it)
        nctx = TOKEN_TILE // CTX_LEN
        kc_refs, vc_refs = [next(it) for _ in range(nctx)], [next(it) for _ in range(nctx)]
        kl_refs, vl_refs = [next(it) for _ in range(nloc)], [next(it) for _ in range(nloc)]
        yg_ref, yr_ref, wmix_ref = (next(it) for _ in range(3))
    if has_proj:
        gp_ref, wp_ref, cos_ref, sa_ref, sb_ref = (next(it) for _ in range(5))
    if has_final:
        fin_ref = next(it)
    o_ref = next(it)
    if has_proj:
        q_ref, k_ref, v_ref, ph_ref, pr_ref = (next(it) for _ in range(5))

    mods = mods_ref[0]
    shift, scale, gate = mods[mod0:mod0 + 1], mods[mod0 + 1:mod0 + 2], mods[mod0 + 2:mod0 + 3]
    f = wout_ref.shape[0]
    ya_all = None
    if has_mix:
        lt = pl.program_id(0) + skip - n_ctx_tiles
        bpt = TOKEN_TILE // ATT_BLOCK
        n_first = jnp.where(lt >= 0, (lt % tiles_per_batch) * bpt, -bpt)
        ya_all = _attend(
            sink_ref, layer,
            [q_in_ref[g * ATT_BLOCK:(g + 1) * ATT_BLOCK, :] for g in range(bpt)],
            [jnp.concatenate([r[...] for r in kl_refs[g:g + 3]], axis=0) for g in range(bpt)],
            [jnp.concatenate([r[...] for r in vl_refs[g:g + 3]], axis=0) for g in range(bpt)],
            [kc_refs[g * ATT_BLOCK // TILE][...] for g in range(bpt)],
            [vc_refs[g * ATT_BLOCK // TILE][...] for g in range(bpt)],
            [n_first + g for g in range(bpt)], length)
    for r0 in range(0, x_ref.shape[0], TILE):
        rows = slice(r0, r0 + TILE)
        x = x_ref[rows, :]
        if two_src:
            x = jnp.where(pl.program_id(0) < n_ctx_tiles, c_ref[rows, :], x)
        if has_mix:
            g0 = r0 // ATT_BLOCK
            ya = ya_all[g0:g0 + TILE // ATT_BLOCK]
            y = jnp.concatenate([jnp.concatenate(ya, axis=0), yg_ref[rows, :], yr_ref[rows, :]], axis=-1)
            x = x + mods[5:6] * _dot(y, wmix_ref[...])
        h = (_rms(x, g_ref[...]) * (1.0 + scale) + shift).astype(BF16)
        a = _dot(h, win_ref[:, :f])
        b = _dot(h, win_ref[:, f:])
        u = (jax.nn.silu(a) * b).astype(BF16)
        out = x + (0.5 * gate) * _dot(u, wout_ref[...])
        o_ref[rows, :] = _rms(out, fin_ref[...]) if has_final else out
        if has_proj:
            h = (_rms(out, gp_ref[...]) * (1.0 + mods[4:5]) + mods[3:4]).astype(BF16)
            cos, sa, sb = cos_ref[rows, :], sa_ref[rows, :], sb_ref[rows, :]
            o = 0
            for ref, rotate, scl in ((q_ref, True, LOG2E * HEAD_DIM ** -0.5), (k_ref, True, None), (v_ref, False, None),
                                     (ph_ref, False, None), (pr_ref, False, None)):
                p = _dot(h, wp_ref[:, o:o + ref.shape[1]])
                o += ref.shape[1]
                if rotate:
                    p = _rope(p, cos, sa, sb)
                if scl is not None:
                    p = p * scl
                ref[rows, :] = p.astype(ref.dtype)


def _token_block(src, mods_all, g_all, w_in_all, w_out_all, *, layer, mod0, n_ctx_tiles, tiles_per_batch, ctx_row,
                 mix=None, proj=None, final_g=None, latent_only=False):
    two_src = isinstance(src, tuple)
    if two_src:
        ctx2, x2 = src
        n, d = ctx2.shape[0] + x2.shape[0], x2.shape[1]
    else:
        n, d = src.shape
    skip = n_ctx_tiles if latent_only else 0
    nt = n // TOKEN_TILE - skip
    row = lambda i: (i + skip, 0)

    def mod_row(i):
        j = i + skip
        return (layer, jnp.where(j < n_ctx_tiles, ctx_row, (j - n_ctx_tiles) // tiles_per_batch), 0, 0)

    if two_src:
        in_specs = [pl.BlockSpec((TOKEN_TILE, d), lambda i: (jnp.minimum(i, n_ctx_tiles - 1), 0)),
                    pl.BlockSpec((TOKEN_TILE, d), lambda i: (jnp.maximum(i - n_ctx_tiles, 0), 0))]
        args = [ctx2, x2]
    else:
        in_specs, args = [pl.BlockSpec((TOKEN_TILE, d), row)], [src]
    wl = layer if w_in_all.shape[0] > 1 else 0
    in_specs += [pl.BlockSpec((None, 1, N_MODS, d), mod_row), _layer_spec(g_all, layer),
                 _layer_spec(w_in_all, wl), _layer_spec(w_out_all, wl)]
    args += [mods_all, g_all, w_in_all, w_out_all]
    length = None
    if mix is not None:
        q, k, v, sink, yg, yr, w_mix_all, bsz, length = mix
        assert TOKEN_TILE == 2 * CTX_LEN and TILE == ATT_QBLOCKS * ATT_BLOCK and k.shape[1] == 2 * HEAD_DIM
        nblk = length // ATT_BLOCK
        blk0 = bsz * CTX_LEN // ATT_BLOCK
        bpt = TOKEN_TILE // ATT_BLOCK

        def lat(i):
            lt = i + skip - n_ctx_tiles
            return lt >= 0, jnp.where(lt >= 0, lt // tiles_per_batch, 0), jnp.where(lt >= 0, lt % tiles_per_batch, 0)

        def ctx_kv_row(h):
            def f(i):
                is_lat, b, _ = lat(i)
                return (jnp.where(is_lat, b, (TOKEN_TILE // CTX_LEN) * (i + skip) + h), 0)
            return f

        def loc_row(off):
            def f(i):
                _, b, jb = lat(i)
                return (blk0 + nblk * b + jnp.clip(jb * bpt + off, 0, nblk - 1), 0)
            return f

        cspec = [pl.BlockSpec((CTX_LEN, k.shape[1]), ctx_kv_row(h)) for h in range(TOKEN_TILE // CTX_LEN)]
        lspec = [pl.BlockSpec((ATT_BLOCK, k.shape[1]), loc_row(o)) for o in range(-1, bpt + 1)]
        in_specs += [pl.BlockSpec(memory_space=pltpu.SMEM), pl.BlockSpec((TOKEN_TILE, q.shape[1]), row)]
        in_specs += cspec + cspec + lspec + lspec
        in_specs += [pl.BlockSpec((TOKEN_TILE, yg.shape[1]), row), pl.BlockSpec((TOKEN_TILE, yr.shape[1]), row),
                     _layer_spec(w_mix_all, layer)]
        args += [sink, q, k, k, v, v] + [k] * len(lspec) + [v] * len(lspec) + [yg, yr, w_mix_all]
    out_specs = [pl.BlockSpec((TOKEN_TILE, d), lambda i: (i, 0))]
    out_shape = [jax.ShapeDtypeStruct((nt * TOKEN_TILE, d), F32)]
    if proj is not None:
        gp_all, wp_all, tabs, widths = proj

        def tab_row(i):
            return (jnp.where(i < n_ctx_tiles, 0, 1 + (i - n_ctx_tiles) % tiles_per_batch), 0)

        in_specs += [_layer_spec(gp_all, layer), _layer_spec(wp_all, layer)]
        in_specs += [pl.BlockSpec((TOKEN_TILE, tabs[0].shape[1]), tab_row)] * 3
        args += [gp_all, wp_all, *tabs]
        nq, nk, nh, nr = widths
        out_specs += [pl.BlockSpec((TOKEN_TILE, wd), lambda i: (i, 0)) for wd in (nq, nk, nk, nh, nr)]
        out_shape += [jax.ShapeDtypeStruct((n, wd), dt) for wd, dt in
                      ((nq, BF16), (nk, BF16), (nk, BF16), (nh, F32), (nr, F32))]
    if final_g is not None:
        in_specs.append(_const_spec((1, d)))
        args.append(final_g.reshape(1, d))
    return pl.pallas_call(
        functools.partial(_token_kernel, layer=layer, mod0=mod0, n_ctx_tiles=n_ctx_tiles,
                          tiles_per_batch=tiles_per_batch, skip=skip, length=length, two_src=two_src,
                          has_mix=mix is not None, has_proj=proj is not None, has_final=final_g is not None),
        grid=(nt,),
        in_specs=in_specs,
        out_specs=out_specs,
        out_shape=out_shape,
        compiler_params=_cparams(("arbitrary",)),
        name="ffn_out" if mix is not None else "ffn_in",
    )(*args)


def _seq_tile(d, i, tiles_per_batch):
    return jnp.where(d == 0, i, jnp.where(i == 0, 0, tiles_per_batch + 1 - i))


def _seq_row(b, tile, bsz, tiles_per_batch):
    return jnp.where(tile == 0, b, bsz + tiles_per_batch * b + tile - 1)


def _shift_rows(x, s, rev):
    if s == 0:
        return x
    return pltpu.roll(x, (x.shape[0] - s) if rev else s, 0)


def _hgrn_body(lbl_ref, hq_ref, fz_ref, hi_ref, hg_ref, ng_ref, jmat_ref, st_ref, upd_ref, *, layer):
    tt, w = hq_ref.shape

    logits = lbl_ref[:, 0, 0, :]
    e = jnp.exp(logits - logits.max(0, keepdims=True))
    lbp = e / e.sum(0, keepdims=True)
    lb = jnp.zeros((1, w), F32)
    for j in range(1, layer + 1):
        lb = lb + lbp[j:j + 1]

    q = jax.nn.silu(hq_ref[...])
    f = lb + (1.0 - lb) * jax.nn.sigmoid(fz_ref[...])
    kk = 1.0 - f
    v = hi_ref[...]
    jmat = jmat_ref[...]

    def run(rev, factored):
        c = HG_FAST_CHUNK if factored else HG_CHUNK
        pos = lax.broadcasted_iota(jnp.int32, (tt, w), 0) % c

        def in_chunk(s, r):
            return (pos <= c - 1 - s) if r else (pos >= s)

        def cumprod(y, r):
            s = 1
            while s < c:
                y = y * jnp.where(in_chunk(s, r), _shift_rows(y, s, r), 1.0)
                s *= 2
            return y

        p_in = cumprod(f, rev)
        half = c // 2
        nch = tt // c
        p3 = p_in.reshape(nch, c, w)
        dec_end = p3[:, 0:1, :] if rev else p3[:, c - 1:c, :]
        if factored:
            k_div = kk / p_in
            k_out = (k_div.reshape(nch, c, w) * dec_end).reshape(tt, w)
        else:
            k_out = kk * cumprod(jnp.where(in_chunk(1, not rev), _shift_rows(f, 1, not rev), 1.0), not rev)
        nhead = w // HG_DK
        lane_head = lax.broadcasted_iota(jnp.int32, (c, w), 1) // HG_DK

        def pairs_factored():
            x3 = (q * p_in).astype(BF16).reshape(nch, c, w)
            zero = jnp.zeros_like(x3)
            q_hat = jnp.concatenate([jnp.where(lane_head == h, x3, zero) for h in range(nhead)], axis=1)
            k_div16 = k_div.astype(BF16).reshape(nch, c, w)
            v3 = v.astype(BF16).reshape(nch, c, w)
            t_row = lax.broadcasted_iota(jnp.int32, (nhead * c, c), 0) % c
            s_col = lax.broadcasted_iota(jnp.int32, (nhead * c, c), 1)
            causal = (s_col >= t_row) if rev else (s_col <= t_row)
            outs = []
            for ci in range(nch):
                att = jnp.where(causal, _dot_nt(q_hat[ci], k_div16[ci]), 0.0)
                oc = _dot(att.astype(BF16), v3[ci])
                outs.append(sum(jnp.where(lane_head == h, oc[h * c:(h + 1) * c], 0.0) for h in range(nhead)))
            return jnp.concatenate(outs, axis=0)

        def pairs_direct():
            def halves(x):
                x4 = x.reshape(nch, 2, half, w)
                return (x4[:, 1], x4[:, 0]) if rev else (x4[:, 0], x4[:, 1])

            r8 = lax.broadcasted_iota(jnp.int32, (nch, half, w), 1)

            def rot(x, s):
                return x if s == 0 else pltpu.roll(x, (half - s) if rev else s, 1)

            def wrapped(s):
                return (r8 > half - 1 - s) if rev else (r8 < s)

            def pair_sum(wgt, vs):
                return (_dot(wgt.reshape(nch * half, w).astype(BF16), jmat) * vs.reshape(nch * half, w))

            (f_a, f_b), (v_a, v_b), (qd_a, qd_b) = halves(f), halves(v), halves(q)
            o_a = jnp.zeros((nch * half, w), F32)
            o_b = jnp.zeros((nch * half, w), F32)
            for delta in range(c):
                s = delta % half
                fr, vr = rot(f_a, s), rot(v_a, s)
                if delta < half:
                    fs_a = jnp.where(wrapped(s), 1.0, fr)
                    nxt_a = qd_a * fs_a
                    o_a = o_a + pair_sum(qd_a - nxt_a, vr)
                    qd_a = nxt_a
                    fs_b = jnp.where(wrapped(s), fr, rot(f_b, s))
                    vs_b = jnp.where(wrapped(s), vr, rot(v_b, s))
                else:
                    fs_b = jnp.where(wrapped(s), 1.0, fr)
                    vs_b = vr
                nxt_b = qd_b * fs_b
                o_b = o_b + pair_sum(qd_b - nxt_b, vs_b)
                qd_b = nxt_b
            o_a, o_b = o_a.reshape(nch, half, w), o_b.reshape(nch, half, w)
            return jnp.stack([o_b, o_a] if rev else [o_a, o_b], axis=1).reshape(tt, w)

        o = pairs_factored() if factored else pairs_direct()
        q_in = (q * p_in).astype(BF16)
        lane3 = lax.broadcasted_iota(jnp.int32, (nch, c, w), 2) % (2 * HG_DK)

        def per_head(x):
            x3 = x.astype(BF16).reshape(nch, c, w)
            zero = jnp.zeros_like(x3)
            return jnp.concatenate([jnp.where(lane3 < HG_DK, x3, zero), jnp.where(lane3 >= HG_DK, x3, zero)], axis=1)

        v_hat, k_hat = per_head(v), per_head(k_out)
        ntile = st_ref.shape[0]
        lt = w // ntile
        for ci in range(nch):
            for j in range(ntile):
                upd_ref[ci, j] = _dot_tn(v_hat[ci, :, j * lt:(j + 1) * lt], k_hat[ci, :, j * lt:(j + 1) * lt])
        st = [st_ref[j] for j in range(ntile)]
        parts = [None] * nch
        for ci in (range(nch - 1, -1, -1) if rev else range(nch)):
            parts[ci] = jnp.concatenate([_dot_nt(q_in[ci * c:(ci + 1) * c, j * lt:(j + 1) * lt], st[j].astype(BF16))
                                         for j in range(ntile)], axis=1)
            for j in range(ntile):
                st[j] = st[j] * dec_end[ci][:, j * lt:(j + 1) * lt] + upd_ref[ci, j]
        for j in range(ntile):
            st_ref[j] = st[j]
        return o + jnp.concatenate(parts, axis=0)

    def finish(tot):
        sq = tot * tot
        hi = sq.astype(BF16)
        lo = (sq - hi.astype(F32)).astype(BF16)
        ms = (_dot(hi, jmat) + _dot(lo, jmat)) * (1.0 / HG_DK)
        y = tot * lax.rsqrt(ms + EPS)
        return (y * ng_ref[...] * jax.nn.silu(hg_ref[...])).astype(BF16)

    def decay_ok(fz):
        fn = lb + (1.0 - lb) * jax.nn.sigmoid(fz)
        c = HG_FAST_CHUNK
        chunk_log = jnp.log(fn).reshape(tt // c, c, w).sum(axis=1)
        return jnp.min(chunk_log) >= math.log(HG_MIN_DECAY)

    return run, finish, decay_ok


def _rglru_body(x_ref, gate_ref, prev_ref, next_ref, cw_ref, cb_ref, wa_ref, ba_ref, wx_ref, bx_ref, lam_ref, h_ref,
                *, tile, tiles_per_batch):
    tt, w = x_ref.shape

    x = x_ref[...]
    has_prev = (tile >= 2).astype(F32)
    has_next = jnp.logical_and(tile >= 1, tile < tiles_per_batch).astype(F32)
    prev = prev_ref[...] * has_prev
    nxt = next_ref[...] * has_next
    row = lax.broadcasted_iota(jnp.int32, (tt, w), 0)
    xm1 = jnp.where(row == 0, prev[HALO - 1:HALO], _shift_rows(x, 1, False))
    xm2 = jnp.where(row == 0, prev[HALO - 2:HALO - 1],
                    jnp.where(row == 1, prev[HALO - 1:HALO], _shift_rows(x, 2, False)))
    xp1 = jnp.where(row == tt - 1, nxt[0:1], _shift_rows(x, 1, True))
    cw = cw_ref[...]
    u = cb_ref[...] + xm2 * cw[0:1] + xm1 * cw[1:2] + x * cw[2:3] + xp1 * cw[3:4]

    u16 = u.astype(BF16)
    r = jax.nn.sigmoid(_dot(u16, wa_ref[0]) + ba_ref[0])
    ig = jax.nn.sigmoid(_dot(u16, wx_ref[0]) + bx_ref[0])
    log_a = -RG_C * r * jax.nn.softplus(-lam_ref[0])
    a = jnp.exp(log_a)
    z = jnp.sqrt(jnp.tanh(-log_a) * (1.0 + a * a)) * (ig * u)

    def scan(rev):
        ng = tt // SUBLANES
        aa, zz = a.reshape(ng, SUBLANES, w), z.reshape(ng, SUBLANES, w)
        r8 = lax.broadcasted_iota(jnp.int32, (ng, SUBLANES, w), 1)
        s = 1
        while s < SUBLANES:
            ok = (r8 <= SUBLANES - 1 - s) if rev else (r8 >= s)
            sh = (SUBLANES - s) if rev else s
            zz = zz + jnp.where(ok, aa * pltpu.roll(zz, sh, 1), 0.0)
            aa = aa * jnp.where(ok, pltpu.roll(aa, sh, 1), 1.0)
            s *= 2
        carry = h_ref[0:1, :]
        groups = [None] * ng
        for gi in (range(ng - 1, -1, -1) if rev else range(ng)):
            groups[gi] = aa[gi] * carry + zz[gi]
            carry = groups[gi][0:1] if rev else groups[gi][SUBLANES - 1:SUBLANES]
        h_ref[...] = jnp.broadcast_to(carry, h_ref.shape)
        return jnp.concatenate(groups, axis=0)

    def finish(h):
        return (jax.nn.gelu(gate_ref[...]) * h).astype(BF16)

    return scan, finish


def _mixers_kernel(*refs, layer, tiles_per_batch, n_cast):
    it = iter(refs)
    (lbl_ref, hq_ref, fz_ref, hi_ref, hg_ref, ng_ref, jmat_ref, x_ref, gate_ref, prev_ref, next_ref, cw_ref, cb_ref,
     wa_ref, ba_ref, wx_ref, bx_ref, lam_ref, fzn_ref) = (next(it) for _ in range(19))
    cast_in = [next(it) for _ in range(n_cast)]
    yg_ref, yr_ref = next(it), next(it)
    cast_out = [next(it) for _ in range(n_cast)]
    st_ref, ofwd_ref, upd_ref, h_ref, hfwd_ref, ok_ref = (next(it) for _ in range(6))
    for src, dst in zip(cast_in, cast_out):
        dst[...] = src[...].astype(BF16)
    d = pl.program_id(1)
    i = pl.program_id(2)
    tile = _seq_tile(d, i, tiles_per_batch)

    @pl.when(i == 0)
    def _():
        st_ref[...] = jnp.zeros_like(st_ref)
        h_ref[...] = jnp.zeros_like(h_ref)
        ok_ref[0] = 0

    factored = ok_ref[0] == 1
    run, finish_hg, decay_ok = _hgrn_body(lbl_ref, hq_ref, fz_ref, hi_ref, hg_ref, ng_ref, jmat_ref, st_ref, upd_ref,
                                          layer=layer)
    scan, finish_rg = _rglru_body(x_ref, gate_ref, prev_ref, next_ref, cw_ref, cb_ref, wa_ref, ba_ref, wx_ref, bx_ref,
                                  lam_ref, h_ref, tile=tile, tiles_per_batch=tiles_per_batch)

    for fac in (True, False):
        chosen = factored if fac else jnp.logical_not(factored)

        @pl.when(jnp.logical_and(d == 0, chosen))
        def _():
            ofwd_ref[tile] = run(False, fac)
            hfwd_ref[tile] = scan(False)

        @pl.when(jnp.logical_and(d == 1, chosen))
        def _():
            yg_ref[...] = finish_hg(run(True, fac) + ofwd_ref[tile])
            yr_ref[...] = finish_rg(scan(True) + hfwd_ref[tile])

    ok_ref[0] = decay_ok(fzn_ref[...]).astype(jnp.int32)


def _mixers(ph, pr, lb_logits, norm_g_all, conv_w, conv_b, wa, ba, wx, bx, lam, *, layer, bsz, tiles_per_batch,
            casts=()):
    n = ph.shape[0]
    depth = lb_logits.shape[0]
    hw = norm_g_all.shape[-1]
    rw = conv_b.shape[-1]
    head = jnp.arange(hw) // HG_DK
    same = head[:, None] == head[None, :]
    lt = 2 * HG_DK
    ntile = hw // lt
    hpt = TILE // HALO

    def seq_row(b, d, i):
        return _seq_row(b, _seq_tile(d, i, tiles_per_batch), bsz, tiles_per_batch)

    def col(cfn):
        return lambda b, d, i: (seq_row(b, d, i), cfn(d))

    def out_row(b, d, i):
        tile = jnp.where(d == 0, 0, _seq_tile(d, i, tiles_per_batch))
        return (_seq_row(b, tile, bsz, tiles_per_batch), 0)

    per_dir = lambda b, d, i: (layer, d, 0, 0)
    nsteps = tiles_per_batch + 1
    bf16_rows = 2 * SUBLANES
    per_layer = [w2.shape[0] // depth for w2, _ in casts]
    csteps = max([s for s in range(1, bsz * 2 * nsteps + 1) if all(p % (s * bf16_rows) == 0 for p in per_layer)])
    step = lambda b, d, i: jnp.minimum((b * 2 + d) * nsteps + i, csteps - 1)
    cast_specs, cast_shapes = [], []
    for (w2, wl), p in zip(casts, per_layer):
        r = p // csteps
        cast_specs.append((pl.BlockSpec((r, CAST_LANES), lambda b, d, i, wl=wl: (wl * csteps + step(b, d, i), 0)),
                           pl.BlockSpec((r, CAST_LANES), lambda b, d, i: (step(b, d, i), 0))))
        cast_shapes.append(jax.ShapeDtypeStruct((p, CAST_LANES), BF16))
    return pl.pallas_call(
        functools.partial(_mixers_kernel, layer=layer, tiles_per_batch=tiles_per_batch, n_cast=len(casts)),
        grid=(bsz, 2, tiles_per_batch + 1),
        in_specs=[
            pl.BlockSpec((depth, 1, 1, hw), lambda b, d, i: (0, d, 0, 0)),
            pl.BlockSpec((TILE, hw), col(lambda d: 0)),
            pl.BlockSpec((TILE, hw), col(lambda d: 1 + d)),
            pl.BlockSpec((TILE, hw), col(lambda d: 3)),
            pl.BlockSpec((TILE, hw), col(lambda d: 4)),
            _layer_spec(norm_g_all, layer), _const_spec((hw, hw)),
            pl.BlockSpec((TILE, rw), col(lambda d: 0)),
            pl.BlockSpec((TILE, rw), col(lambda d: 1)),
            pl.BlockSpec((HALO, rw), lambda b, d, i: (jnp.maximum(seq_row(b, d, i) * hpt - 1, 0), 0)),
            pl.BlockSpec((HALO, rw), lambda b, d, i: (jnp.minimum((seq_row(b, d, i) + 1) * hpt, n // HALO - 1), 0)),
            _layer_spec(conv_w, layer), _layer_spec(conv_b, layer),
            pl.BlockSpec((None, 1, rw, rw), per_dir), pl.BlockSpec((None, 1, 1, rw), per_dir),
            pl.BlockSpec((None, 1, rw, rw), per_dir), pl.BlockSpec((None, 1, 1, rw), per_dir),
            pl.BlockSpec((None, 1, 1, rw), per_dir),
            pl.BlockSpec((TILE, hw), lambda b, d, i: (seq_row(b, d, jnp.minimum(i + 1, tiles_per_batch)), 1 + d)),
        ] + [cs[0] for cs in cast_specs],
        out_specs=[pl.BlockSpec((TILE, hw), out_row), pl.BlockSpec((TILE, rw), out_row)] + [cs[1] for cs in cast_specs],
        out_shape=[jax.ShapeDtypeStruct((n, hw), BF16), jax.ShapeDtypeStruct((n, rw), BF16)] + cast_shapes,
        scratch_shapes=[pltpu.VMEM((ntile, lt, lt), F32), pltpu.VMEM((tiles_per_batch + 1, TILE, hw), F32),
                        pltpu.VMEM((TILE // HG_CHUNK, ntile, lt, lt), F32),
                        pltpu.VMEM((SUBLANES, rw), F32), pltpu.VMEM((tiles_per_batch + 1, TILE, rw), F32),
                        pltpu.SMEM((1,), jnp.int32)],
        compiler_params=_cparams(("arbitrary", "arbitrary", "arbitrary")),
        name="mixers",
    )(lb_logits.reshape(depth, 2, 1, hw), ph, ph, ph, ph, norm_g_all, same.astype(BF16),
      pr, pr, pr, pr, conv_w, conv_b, wa, ba, wx, bx, lam, ph, *[w2 for w2, _ in casts])


def _rope_tables(length):
    n_freq = HEAD_DIM // 4
    rows = length // GRID_W
    row = np.repeat(np.arange(rows), GRID_W).astype(np.float32)
    col = np.tile(np.arange(GRID_W), rows).astype(np.float32)
    inv_freq = (np.float32(ROPE_BASE) ** (-np.arange(n_freq, dtype=np.float32) / np.float32(n_freq))).astype(np.float32)
    ang = np.stack([row[:, None] * inv_freq, col[:, None] * inv_freq], axis=1)
    cos, sin = np.cos(ang), np.sin(ang)
    zero = np.zeros_like(sin[:, 0])
    cos_h = np.concatenate([cos[:, 0], cos[:, 0], cos[:, 1], cos[:, 1]], axis=-1)
    sa_h = np.concatenate([-sin[:, 0], zero, -sin[:, 1], zero], axis=-1)
    sb_h = np.concatenate([zero, sin[:, 0], zero, sin[:, 1]], axis=-1)
    ctx = (np.ones((TOKEN_TILE, HEAD_DIM), np.float32), np.zeros((TOKEN_TILE, HEAD_DIM), np.float32),
           np.zeros((TOKEN_TILE, HEAD_DIM), np.float32))
    return tuple(jnp.asarray(np.tile(np.concatenate([c, t], axis=0), (1, 2))) for c, t in zip(ctx, (cos_h, sa_h, sb_h)))


def _block_diag(wb):
    nb, bw = wb.shape[-3], wb.shape[-1]
    out = jnp.zeros(wb.shape[:-3] + (nb * bw, nb * bw), wb.dtype)
    for j in range(nb):
        out = out.at[..., j * bw:(j + 1) * bw, j * bw:(j + 1) * bw].set(wb[..., j, :, :])
    return out


def _regroup_heads(t, axis):
    shp = t.shape
    t = t.reshape(shp[:axis] + (ATT_KV_HEADS, ATT_GROUP, HEAD_DIM) + shp[axis + 1:])
    return jnp.swapaxes(t, axis, axis + 1).reshape(shp)


def kernel(x, c, ctx, c_ctx, ada_w, ada_b, norm_ffn1, ffn1_w_in, ffn1_w_out, norm_mix, w_in, w_out, attn_sink, hg_lb_logits, hg_norm, rg_conv_w, rg_conv_b, rg_gate_a_w, rg_gate_a_b, rg_gate_x_w, rg_gate_x_b, rg_lambda, norm_ffn2, ffn2_w_in, ffn2_w_out, final_norm):
    bsz, length, d = x.shape
    depth = ada_w.shape[0]
    assert ctx.shape[1] == CTX_LEN == TILE and length % TOKEN_TILE == 0 and (bsz * CTX_LEN) % TOKEN_TILE == 0
    tpb = length // TILE
    geo = dict(n_ctx_tiles=bsz * CTX_LEN // TOKEN_TILE, tiles_per_batch=length // TOKEN_TILE, ctx_row=bsz)

    ctx2, x2 = ctx.reshape(bsz * CTX_LEN, d), x.reshape(bsz * length, d)
    assert bsz + 1 <= SUBLANES
    cond = jnp.concatenate([c, c_ctx[None], jnp.zeros((SUBLANES - bsz - 1, d), F32)], axis=0)
    mods_all = _adaln(cond, ada_w, ada_b).reshape(depth, SUBLANES, N_MODS, d)
    tabs = _rope_tables(length)

    att_w = w_out.shape[1] // 2
    kv_w = ATT_KV_HEADS * HEAD_DIM
    hg_w = hg_norm.shape[1]
    rg_w = rg_conv_b.shape[1]
    widths = (att_w, kv_w, 5 * hg_w, 2 * rg_w)
    assert sum(widths) + kv_w == w_in.shape[2]

    to16 = lambda t: t.astype(BF16)
    as_rows = lambda t: t.reshape(-1, CAST_LANES)
    f_in, f_out = to16(ffn1_w_in[0])[None], to16(ffn1_w_out[0])[None]
    wi16, wo16 = to16(w_in), to16(w_out)
    wi16 = wi16.at[:, :, :att_w].set(_regroup_heads(wi16[:, :, :att_w], 2))
    wo16 = wo16.at[:, :att_w].set(_regroup_heads(wo16[:, :att_w], 1))
    wa_bd, wx_bd = to16(_block_diag(rg_gate_a_w)), to16(_block_diag(rg_gate_x_w))
    as_row = lambda t: t.reshape(t.shape[:-1] + (1, t.shape[-1]))
    g1, gm, g2, hgn = map(as_row, (norm_ffn1, norm_mix, norm_ffn2, hg_norm))
    conv_b, ba, bx, lam = map(as_row, (rg_conv_b, rg_gate_a_b, rg_gate_x_b, rg_lambda))

    xs = (ctx2, x2)
    for l in range(depth):
        last = l == depth - 1
        xs, q, k, v, ph, pr = _token_block(xs, mods_all, g1, f_in, f_out, layer=l, mod0=0,
                                           proj=(gm, wi16, tabs, widths), **geo)
        casts = [(as_rows(ffn2_w_in), l), (as_rows(ffn2_w_out), l)]
        if not last:
            casts += [(as_rows(ffn1_w_in), l + 1), (as_rows(ffn1_w_out), l + 1)]
        yg, yr, *cast = _mixers(ph, pr, hg_lb_logits, hgn, rg_conv_w, conv_b, wa_bd, ba, wx_bd, bx, lam,
                                layer=l, bsz=bsz, tiles_per_batch=tpb, casts=casts)
        f2_in, f2_out = cast[0].reshape((1,) + ffn2_w_in.shape[1:]), cast[1].reshape((1,) + ffn2_w_out.shape[1:])
        xs, = _token_block(xs, mods_all, g2, f2_in, f2_out, layer=l, mod0=6,
                           mix=(q, k, v, attn_sink, yg, yr, wo16, bsz, length),
                           final_g=final_norm if last else None, latent_only=last, **geo)
        if not last:
            f_in, f_out = cast[2].reshape((1,) + ffn1_w_in.shape[1:]), cast[3].reshape((1,) + ffn1_w_out.shape[1:])
    return xs.reshape(bsz, length, d)
```

```python
import functools
import math

import jax
import jax.numpy as jnp
import numpy as np
from jax import lax
from jax.experimental import pallas as pl
from jax.experimental.pallas import tpu as pltpu

F32 = jnp.float32
BF16 = jnp.bfloat16

GRID_W = 64
CTX_LEN = 256
N_MODS = 9
EPS = 1e-6
HEAD_DIM = 64
ATT_KV_HEADS = 2
ATT_GROUP = 4
WINDOW = 128
ATT_BLOCK = 128
ATT_QBLOCKS = 2
ROPE_BASE = 10000.0
LOG2E = 1.4426950408889634
HG_DK = 64
HG_CHUNK = 16
HG_FAST_CHUNK = 32
HG_MIN_DECAY = 1e-30
RG_C = 8.0
assert ATT_KV_HEADS == 2

TILE = 256
TOKEN_TILE = 512
SUBLANES = 8
HALO = SUBLANES
ADALN_COL_BLOCKS = 8
VMEM_LIMIT = 58 * 1024 * 1024


def _cparams(sem):
    return pltpu.CompilerParams(dimension_semantics=sem, vmem_limit_bytes=VMEM_LIMIT)


def _const_spec(shape):
    nd = len(shape)
    return pl.BlockSpec(shape, lambda *_: (0,) * nd, pipeline_mode=pl.Buffered(1))


def _layer_spec(arr, l):
    nd = arr.ndim
    return pl.BlockSpec((None,) + arr.shape[1:], lambda *_: (l,) + (0,) * (nd - 1), pipeline_mode=pl.Buffered(1))


def _rms(xf, g):
    return xf * lax.rsqrt(jnp.mean(xf * xf, axis=-1, keepdims=True) + EPS) * g


def _dot(a, b):
    return jnp.dot(a, b, preferred_element_type=F32)


def _dot_nt(a, b):
    return lax.dot_general(a, b, (((1,), (1,)), ((), ())), preferred_element_type=F32)


def _dot_tn(a, b):
    return lax.dot_general(a, b, (((0,), (0,)), ((), ())), preferred_element_type=F32)


def _adaln_kernel(c_ref, w_ref, b_ref, o_ref):
    s = jax.nn.silu(c_ref[...]).astype(BF16)
    o_ref[0] = _dot(s, w_ref[0].astype(BF16)) + b_ref[0]


def _adaln(cond, ada_w, ada_b):
    depth, d, n = ada_w.shape
    tn = n // ADALN_COL_BLOCKS
    return pl.pallas_call(
        _adaln_kernel,
        grid=(depth, n // tn),
        in_specs=[
            pl.BlockSpec(cond.shape, lambda l, j: (0, 0)),
            pl.BlockSpec((1, d, tn), lambda l, j: (l, 0, j)),
            pl.BlockSpec((1, 1, tn), lambda l, j: (l, 0, j)),
        ],
        out_specs=pl.BlockSpec((1, cond.shape[0], tn), lambda l, j: (l, 0, j)),
        out_shape=jax.ShapeDtypeStruct((depth, cond.shape[0], n), F32),
        compiler_params=_cparams(("arbitrary", "arbitrary")),
        name="adaln",
    )(cond, ada_w, ada_b.reshape(depth, 1, n))


def _attend(sink_ref, layer, q_blocks, k_loc, v_loc, k_ctx, v_ctx, n_blocks, length):
    sw = k_ctx[0].shape[1]
    nslab = q_blocks[0].shape[1] // sw

    def only(t, kv, fill=0.0):
        lane = lax.broadcasted_iota(jnp.int32, t.shape, 1)
        return jnp.where((lane >= kv * HEAD_DIM) & (lane < (kv + 1) * HEAD_DIM), t, jnp.full_like(t, fill))

    nq = len(q_blocks)
    chains = [(j, kv) for j in range(nq) for kv in range(ATT_KV_HEADS)]
    q, valid = [], []
    for j in range(nq):
        q.append(jnp.concatenate([q_blocks[j][:, g * sw:(g + 1) * sw] for g in range(nslab)], axis=0))
        n = n_blocks[j]
        k_pos = (n - 1) * ATT_BLOCK + lax.broadcasted_iota(jnp.int32, (3 * ATT_BLOCK, ATT_BLOCK), 0)
        q_pos = n * ATT_BLOCK + lax.broadcasted_iota(jnp.int32, (3 * ATT_BLOCK, ATT_BLOCK), 1)
        ok = (jnp.abs(k_pos - q_pos) <= WINDOW) & (k_pos >= 0) & (k_pos < length) & (n >= 0)
        valid.append(jnp.concatenate([ok] * nslab, axis=1))
    sink = {kv: jnp.concatenate([jnp.full((1, ATT_BLOCK), sink_ref[layer, kv * ATT_GROUP + g] * LOG2E, F32)
                                 for g in range(nslab)], axis=1) for kv in range(ATT_KV_HEADS)}
    s_loc = {c: jnp.where(valid[c[0]], _dot_nt(only(k_loc[c[0]], c[1]), q[c[0]]), -jnp.inf) for c in chains}
    s_ctx = {c: _dot_nt(only(k_ctx[c[0]], c[1]), q[c[0]]) for c in chains}
    m = {c: jnp.maximum(jnp.maximum(s_loc[c].max(0, keepdims=True), s_ctx[c].max(0, keepdims=True)), sink[c[1]])
         for c in chains}
    p_loc = {c: jnp.exp2(s_loc[c] - m[c]).astype(BF16) for c in chains}
    p_ctx = {c: jnp.exp2(s_ctx[c] - m[c]).astype(BF16) for c in chains}
    o = {c: (_dot_tn(only(v_loc[c[0]], c[1], 1.0), p_loc[c]) + _dot_tn(only(v_ctx[c[0]], c[1], 1.0), p_ctx[c]))
         for c in chains}
    row_kv = lax.broadcasted_iota(jnp.int32, (sw, nslab * ATT_BLOCK), 0) // HEAD_DIM
    outs = []
    for j in range(nq):
        scaled = []
        for kv in range(ATT_KV_HEADS):
            c = (j, kv)
            other = (1 - kv) * HEAD_DIM
            denom = o[c][other:other + 1, :] + jnp.exp2(sink[kv] - m[c])
            scaled.append(o[c] / denom)
        out = jnp.where(row_kv == 0, scaled[0], scaled[1]).T
        outs.append(jnp.concatenate([out[g * ATT_BLOCK:(g + 1) * ATT_BLOCK] for g in range(nslab)],
                                    axis=1).astype(BF16))
    return outs


def _rope(t, cos, sa, sb):
    w = t.shape[1]
    rep = w // cos.shape[1]
    c, a, b = (jnp.concatenate([m] * rep, axis=1) if rep > 1 else m for m in (cos, sa, sb))
    half = HEAD_DIM // 4
    return t * c + pltpu.roll(t, w - half, 1) * a + pltpu.roll(t, half, 1) * b


def _token_kernel(*refs, layer, mod0, n_ctx_tiles, tiles_per_batch, skip, length, two_src, has_mix, has_proj,
                  has_final):
    it = iter(refs)
    if two_src:
        c_ref = next(it)
    x_ref, mods_ref, g_ref, win_ref, wout_ref = (next(it) for _ in range(5))
    if has_mix:
        nloc = TOKEN_TILE // ATT_BLOCK + 2
        sink_ref, q_in_ref = next(it), next(it)
        nctx = TOKEN_TILE // CTX_LEN
        kc_refs, vc_refs = [next(it) for _ in range(nctx)], [next(it) for _ in range(nctx)]
        kl_refs, vl_refs = [next(it) for _ in range(nloc)], [next(it) for _ in range(nloc)]
        yg_ref, yr_ref, wmix_ref = (next(it) for _ in range(3))
    if has_proj:
        gp_ref, wp_ref, cos_ref, sa_ref, sb_ref = (next(it) for _ in range(5))
    if has_final:
        fin_ref = next(it)
    o_ref = next(it)
    if has_proj:
        q_ref, k_ref, v_ref, ph_ref, pr_ref = (next(it) for _ in range(5))

    mods = mods_ref[0]
    shift, scale, gate = mods[mod0:mod0 + 1], mods[mod0 + 1:mod0 + 2], mods[mod0 + 2:mod0 + 3]
    f = wout_ref.shape[0]
    ya_all = None
    if has_mix:
        lt = pl.program_id(0) + skip - n_ctx_tiles
        bpt = TOKEN_TILE // ATT_BLOCK
        n_first = jnp.where(lt >= 0, (lt % tiles_per_batch) * bpt, -bpt)
        ya_all = _attend(
            sink_ref, layer,
            [q_in_ref[g * ATT_BLOCK:(g + 1) * ATT_BLOCK, :] for g in range(bpt)],
            [jnp.concatenate([r[...] for r in kl_refs[g:g + 3]], axis=0) for g in range(bpt)],
            [jnp.concatenate([r[...] for r in vl_refs[g:g + 3]], axis=0) for g in range(bpt)],
            [kc_refs[g * ATT_BLOCK // TILE][...] for g in range(bpt)],
            [vc_refs[g * ATT_BLOCK // TILE][...] for g in range(bpt)],
            [n_first + g for g in range(bpt)], length)
    halves = [(r0, slice(r0, r0 + TILE), {}) for r0 in range(0, x_ref.shape[0], TILE)]

    def st_in(r0, rows, t):
        x = x_ref[rows, :]
        if two_src:
            x = jnp.where(pl.program_id(0) < n_ctx_tiles, c_ref[rows, :], x)
        if has_mix:
            g0 = r0 // ATT_BLOCK
            ya = ya_all[g0:g0 + TILE // ATT_BLOCK]
            y = jnp.concatenate([jnp.concatenate(ya, axis=0), yg_ref[rows, :], yr_ref[rows, :]], axis=-1)
            x = x + mods[5:6] * _dot(y, wmix_ref[...])
        t["x"] = x
        t["h"] = (_rms(x, g_ref[...]) * (1.0 + scale) + shift).astype(BF16)

    def st_a(r0, rows, t):
        t["a"] = _dot(t["h"], win_ref[:, :f])

    def st_b(r0, rows, t):
        t["u"] = (jax.nn.silu(t["a"]) * _dot(t["h"], win_ref[:, f:])).astype(BF16)

    def st_out(r0, rows, t):
        out = t["x"] + (0.5 * gate) * _dot(t["u"], wout_ref[...])
        o_ref[rows, :] = _rms(out, fin_ref[...]) if has_final else out
        t["out"] = out

    def st_proj(r0, rows, t):
        h = (_rms(t["out"], gp_ref[...]) * (1.0 + mods[4:5]) + mods[3:4]).astype(BF16)
        cos, sa, sb = cos_ref[rows, :], sa_ref[rows, :], sb_ref[rows, :]
        o = 0
        for ref, rotate, scl in ((q_ref, True, LOG2E * HEAD_DIM ** -0.5), (k_ref, True, None), (v_ref, False, None),
                                 (ph_ref, False, None), (pr_ref, False, None)):
            p = _dot(h, wp_ref[:, o:o + ref.shape[1]])
            o += ref.shape[1]
            if rotate:
                p = _rope(p, cos, sa, sb)
            if scl is not None:
                p = p * scl
            ref[rows, :] = p.astype(ref.dtype)

    for stage in (st_in, st_a, st_b, st_out) + ((st_proj,) if has_proj else ()):
        for r0, rows, t in halves:
            stage(r0, rows, t)


def _token_block(src, mods_all, g_all, w_in_all, w_out_all, *, layer, mod0, n_ctx_tiles, tiles_per_batch, ctx_row,
                 mix=None, proj=None, final_g=None, latent_only=False):
    two_src = isinstance(src, tuple)
    if two_src:
        ctx2, x2 = src
        n, d = ctx2.shape[0] + x2.shape[0], x2.shape[1]
    else:
        n, d = src.shape
    skip = n_ctx_tiles if latent_only else 0
    nt = n // TOKEN_TILE - skip
    row = lambda i: (i + skip, 0)

    def mod_row(i):
        j = i + skip
        return (layer, jnp.where(j < n_ctx_tiles, ctx_row, (j - n_ctx_tiles) // tiles_per_batch), 0, 0)

    if two_src:
        in_specs = [pl.BlockSpec((TOKEN_TILE, d), lambda i: (jnp.minimum(i, n_ctx_tiles - 1), 0)),
                    pl.BlockSpec((TOKEN_TILE, d), lambda i: (jnp.maximum(i - n_ctx_tiles, 0), 0))]
        args = [ctx2, x2]
    else:
        in_specs, args = [pl.BlockSpec((TOKEN_TILE, d), row)], [src]
    in_specs += [pl.BlockSpec((None, 1, N_MODS, d), mod_row), _layer_spec(g_all, layer),
                 _layer_spec(w_in_all, layer), _layer_spec(w_out_all, layer)]
    args += [mods_all, g_all, w_in_all, w_out_all]
    length = None
    if mix is not None:
        q, k, v, sink, yg, yr, w_mix_all, bsz, length = mix
        assert TOKEN_TILE == 2 * CTX_LEN and TILE == ATT_QBLOCKS * ATT_BLOCK and k.shape[1] == 2 * HEAD_DIM
        nblk = length // ATT_BLOCK
        blk0 = bsz * CTX_LEN // ATT_BLOCK
        bpt = TOKEN_TILE // ATT_BLOCK

        def lat(i):
            lt = i + skip - n_ctx_tiles
            return lt >= 0, jnp.where(lt >= 0, lt // tiles_per_batch, 0), jnp.where(lt >= 0, lt % tiles_per_batch, 0)

        def ctx_kv_row(h):
            def f(i):
                is_lat, b, _ = lat(i)
                return (jnp.where(is_lat, b, (TOKEN_TILE // CTX_LEN) * (i + skip) + h), 0)
            return f

        def loc_row(off):
            def f(i):
                _, b, jb = lat(i)
                return (blk0 + nblk * b + jnp.clip(jb * bpt + off, 0, nblk - 1), 0)
            return f

        cspec = [pl.BlockSpec((CTX_LEN, k.shape[1]), ctx_kv_row(h)) for h in range(TOKEN_TILE // CTX_LEN)]
        lspec = [pl.BlockSpec((ATT_BLOCK, k.shape[1]), loc_row(o)) for o in range(-1, bpt + 1)]
        in_specs += [pl.BlockSpec(memory_space=pltpu.SMEM), pl.BlockSpec((TOKEN_TILE, q.shape[1]), row)]
        in_specs += cspec + cspec + lspec + lspec
        in_specs += [pl.BlockSpec((TOKEN_TILE, yg.shape[1]), row), pl.BlockSpec((TOKEN_TILE, yr.shape[1]), row),
                     _layer_spec(w_mix_all, layer)]
        args += [sink, q, k, k, v, v] + [k] * len(lspec) + [v] * len(lspec) + [yg, yr, w_mix_all]
    out_specs = [pl.BlockSpec((TOKEN_TILE, d), lambda i: (i, 0))]
    out_shape = [jax.ShapeDtypeStruct((nt * TOKEN_TILE, d), F32)]
    if proj is not None:
        gp_all, wp_all, tabs, widths = proj

        def tab_row(i):
            return (jnp.where(i < n_ctx_tiles, 0, 1 + (i - n_ctx_tiles) % tiles_per_batch), 0)

        in_specs += [_layer_spec(gp_all, layer), _layer_spec(wp_all, layer)]
        in_specs += [pl.BlockSpec((TOKEN_TILE, tabs[0].shape[1]), tab_row)] * 3
        args += [gp_all, wp_all, *tabs]
        nq, nk, nh, nr = widths
        out_specs += [pl.BlockSpec((TOKEN_TILE, wd), lambda i: (i, 0)) for wd in (nq, nk, nk, nh, nr)]
        out_shape += [jax.ShapeDtypeStruct((n, wd), dt) for wd, dt in
                      ((nq, BF16), (nk, BF16), (nk, BF16), (nh, F32), (nr, F32))]
    if final_g is not None:
        in_specs.append(_const_spec((1, d)))
        args.append(final_g.reshape(1, d))
    return pl.pallas_call(
        functools.partial(_token_kernel, layer=layer, mod0=mod0, n_ctx_tiles=n_ctx_tiles,
                          tiles_per_batch=tiles_per_batch, skip=skip, length=length, two_src=two_src,
                          has_mix=mix is not None, has_proj=proj is not None, has_final=final_g is not None),
        grid=(nt,),
        in_specs=in_specs,
        out_specs=out_specs,
        out_shape=out_shape,
        compiler_params=_cparams(("arbitrary",)),
        name="ffn_out" if mix is not None else "ffn_in",
    )(*args)


def _seq_tile(d, i, tiles_per_batch):
    return jnp.where(d == 0, i, jnp.where(i == 0, 0, tiles_per_batch + 1 - i))


def _seq_row(b, tile, bsz, tiles_per_batch):
    return jnp.where(tile == 0, b, bsz + tiles_per_batch * b + tile - 1)


def _shift_rows(x, s, rev):
    if s == 0:
        return x
    return pltpu.roll(x, (x.shape[0] - s) if rev else s, 0)


def _hgrn_body(lbl_ref, hq_ref, fz_ref, hi_ref, hg_ref, ng_ref, jmat_ref, st_ref, upd_ref, *, layer):
    tt, w = hq_ref.shape

    logits = lbl_ref[:, 0, 0, :]
    e = jnp.exp(logits - logits.max(0, keepdims=True))
    lbp = e / e.sum(0, keepdims=True)
    lb = jnp.zeros((1, w), F32)
    for j in range(1, layer + 1):
        lb = lb + lbp[j:j + 1]

    q = jax.nn.silu(hq_ref[...])
    f = lb + (1.0 - lb) * jax.nn.sigmoid(fz_ref[...])
    kk = 1.0 - f
    v = hi_ref[...]
    jmat = jmat_ref[...]

    def run(rev, factored):
        c = HG_FAST_CHUNK if factored else HG_CHUNK
        pos = lax.broadcasted_iota(jnp.int32, (tt, w), 0) % c

        def in_chunk(s, r):
            return (pos <= c - 1 - s) if r else (pos >= s)

        def cumprod(y, r):
            s = 1
            while s < c:
                y = y * jnp.where(in_chunk(s, r), _shift_rows(y, s, r), 1.0)
                s *= 2
            return y

        p_in = cumprod(f, rev)
        half = c // 2
        nch = tt // c
        p3 = p_in.reshape(nch, c, w)
        dec_end = p3[:, 0:1, :] if rev else p3[:, c - 1:c, :]
        if factored:
            k_div = kk / p_in
            k_out = (k_div.reshape(nch, c, w) * dec_end).reshape(tt, w)
        else:
            k_out = kk * cumprod(jnp.where(in_chunk(1, not rev), _shift_rows(f, 1, not rev), 1.0), not rev)
        nhead = w // HG_DK
        lane_head = lax.broadcasted_iota(jnp.int32, (c, w), 1) // HG_DK

        def pairs_factored():
            x3 = (q * p_in).astype(BF16).reshape(nch, c, w)
            zero = jnp.zeros_like(x3)
            q_hat = jnp.concatenate([jnp.where(lane_head == h, x3, zero) for h in range(nhead)], axis=1)
            k_div16 = k_div.astype(BF16).reshape(nch, c, w)
            v3 = v.astype(BF16).reshape(nch, c, w)
            t_row = lax.broadcasted_iota(jnp.int32, (nhead * c, c), 0) % c
            s_col = lax.broadcasted_iota(jnp.int32, (nhead * c, c), 1)
            causal = (s_col >= t_row) if rev else (s_col <= t_row)
            outs = []
            for ci in range(nch):
                att = jnp.where(causal, _dot_nt(q_hat[ci], k_div16[ci]), 0.0)
                oc = _dot(att.astype(BF16), v3[ci])
                outs.append(sum(jnp.where(lane_head == h, oc[h * c:(h + 1) * c], 0.0) for h in range(nhead)))
            return jnp.concatenate(outs, axis=0)

        def pairs_direct():
            def halves(x):
                x4 = x.reshape(nch, 2, half, w)
                return (x4[:, 1], x4[:, 0]) if rev else (x4[:, 0], x4[:, 1])

            r8 = lax.broadcasted_iota(jnp.int32, (nch, half, w), 1)

            def rot(x, s):
                return x if s == 0 else pltpu.roll(x, (half - s) if rev else s, 1)

            def wrapped(s):
                return (r8 > half - 1 - s) if rev else (r8 < s)

            def pair_sum(wgt, vs):
                return (_dot(wgt.reshape(nch * half, w).astype(BF16), jmat) * vs.reshape(nch * half, w))

            (f_a, f_b), (v_a, v_b), (qd_a, qd_b) = halves(f), halves(v), halves(q)
            o_a = jnp.zeros((nch * half, w), F32)
            o_b = jnp.zeros((nch * half, w), F32)
            for delta in range(c):
                s = delta % half
                fr, vr = rot(f_a, s), rot(v_a, s)
                if delta < half:
                    fs_a = jnp.where(wrapped(s), 1.0, fr)
                    nxt_a = qd_a * fs_a
                    o_a = o_a + pair_sum(qd_a - nxt_a, vr)
                    qd_a = nxt_a
                    fs_b = jnp.where(wrapped(s), fr, rot(f_b, s))
                    vs_b = jnp.where(wrapped(s), vr, rot(v_b, s))
                else:
                    fs_b = jnp.where(wrapped(s), 1.0, fr)
                    vs_b = vr
                nxt_b = qd_b * fs_b
                o_b = o_b + pair_sum(qd_b - nxt_b, vs_b)
                qd_b = nxt_b
            o_a, o_b = o_a.reshape(nch, half, w), o_b.reshape(nch, half, w)
            return jnp.stack([o_b, o_a] if rev else [o_a, o_b], axis=1).reshape(tt, w)

        o = pairs_factored() if factored else pairs_direct()
        q_in = (q * p_in).astype(BF16)
        lane3 = lax.broadcasted_iota(jnp.int32, (nch, c, w), 2) % (2 * HG_DK)

        def per_head(x):
            x3 = x.astype(BF16).reshape(nch, c, w)
            zero = jnp.zeros_like(x3)
            return jnp.concatenate([jnp.where(lane3 < HG_DK, x3, zero), jnp.where(lane3 >= HG_DK, x3, zero)], axis=1)

        v_hat, k_hat = per_head(v), per_head(k_out)
        ntile = st_ref.shape[0]
        lt = w // ntile
        for ci in range(nch):
            for j in range(ntile):
                upd_ref[ci, j] = _dot_tn(v_hat[ci, :, j * lt:(j + 1) * lt], k_hat[ci, :, j * lt:(j + 1) * lt])
        st = [st_ref[j] for j in range(ntile)]
        parts = [None] * nch
        for ci in (range(nch - 1, -1, -1) if rev else range(nch)):
            parts[ci] = jnp.concatenate([_dot_nt(q_in[ci * c:(ci + 1) * c, j * lt:(j + 1) * lt], st[j].astype(BF16))
                                         for j in range(ntile)], axis=1)
            for j in range(ntile):
                st[j] = st[j] * dec_end[ci][:, j * lt:(j + 1) * lt] + upd_ref[ci, j]
        for j in range(ntile):
            st_ref[j] = st[j]
        return o + jnp.concatenate(parts, axis=0)

    def finish(tot):
        sq = tot * tot
        hi = sq.astype(BF16)
        lo = (sq - hi.astype(F32)).astype(BF16)
        ms = (_dot(hi, jmat) + _dot(lo, jmat)) * (1.0 / HG_DK)
        y = tot * lax.rsqrt(ms + EPS)
        return (y * ng_ref[...] * jax.nn.silu(hg_ref[...])).astype(BF16)

    def decay_ok(fz):
        fn = lb + (1.0 - lb) * jax.nn.sigmoid(fz)
        c = HG_FAST_CHUNK
        chunk_log = jnp.log(fn).reshape(tt // c, c, w).sum(axis=1)
        return jnp.min(chunk_log) >= math.log(HG_MIN_DECAY)

    return run, finish, decay_ok


def _rglru_body(x_ref, gate_ref, prev_ref, next_ref, cw_ref, cb_ref, wa_ref, ba_ref, wx_ref, bx_ref, lam_ref, h_ref,
                *, tile, tiles_per_batch):
    tt, w = x_ref.shape

    x = x_ref[...]
    has_prev = (tile >= 2).astype(F32)
    has_next = jnp.logical_and(tile >= 1, tile < tiles_per_batch).astype(F32)
    prev = prev_ref[...] * has_prev
    nxt = next_ref[...] * has_next
    row = lax.broadcasted_iota(jnp.int32, (tt, w), 0)
    xm1 = jnp.where(row == 0, prev[HALO - 1:HALO], _shift_rows(x, 1, False))
    xm2 = jnp.where(row == 0, prev[HALO - 2:HALO - 1],
                    jnp.where(row == 1, prev[HALO - 1:HALO], _shift_rows(x, 2, False)))
    xp1 = jnp.where(row == tt - 1, nxt[0:1], _shift_rows(x, 1, True))
    cw = cw_ref[...]
    u = cb_ref[...] + xm2 * cw[0:1] + xm1 * cw[1:2] + x * cw[2:3] + xp1 * cw[3:4]

    u16 = u.astype(BF16)
    r = jax.nn.sigmoid(_dot(u16, wa_ref[0]) + ba_ref[0])
    ig = jax.nn.sigmoid(_dot(u16, wx_ref[0]) + bx_ref[0])
    log_a = -RG_C * r * jax.nn.softplus(-lam_ref[0])
    a = jnp.exp(log_a)
    z = jnp.sqrt(jnp.tanh(-log_a) * (1.0 + a * a)) * (ig * u)

    def scan(rev):
        ng = tt // SUBLANES
        aa, zz = a.reshape(ng, SUBLANES, w), z.reshape(ng, SUBLANES, w)
        r8 = lax.broadcasted_iota(jnp.int32, (ng, SUBLANES, w), 1)
        s = 1
        while s < SUBLANES:
            ok = (r8 <= SUBLANES - 1 - s) if rev else (r8 >= s)
            sh = (SUBLANES - s) if rev else s
            zz = zz + jnp.where(ok, aa * pltpu.roll(zz, sh, 1), 0.0)
            aa = aa * jnp.where(ok, pltpu.roll(aa, sh, 1), 1.0)
            s *= 2
        carry = h_ref[0:1, :]
        groups = [None] * ng
        for gi in (range(ng - 1, -1, -1) if rev else range(ng)):
            groups[gi] = aa[gi] * carry + zz[gi]
            carry = groups[gi][0:1] if rev else groups[gi][SUBLANES - 1:SUBLANES]
        h_ref[...] = jnp.broadcast_to(carry, h_ref.shape)
        return jnp.concatenate(groups, axis=0)

    def finish(h):
        return (jax.nn.gelu(gate_ref[...]) * h).astype(BF16)

    return scan, finish


def _mixers_kernel(lbl_ref, hq_ref, fz_ref, hi_ref, hg_ref, ng_ref, jmat_ref,
                   x_ref, gate_ref, prev_ref, next_ref, cw_ref, cb_ref, wa_ref, ba_ref, wx_ref, bx_ref, lam_ref, fzn_ref,
                   yg_ref, yr_ref, st_ref, ofwd_ref, upd_ref, h_ref, hfwd_ref, ok_ref, *, layer, tiles_per_batch):
    d = pl.program_id(1)
    i = pl.program_id(2)
    tile = _seq_tile(d, i, tiles_per_batch)

    @pl.when(i == 0)
    def _():
        st_ref[...] = jnp.zeros_like(st_ref)
        h_ref[...] = jnp.zeros_like(h_ref)
        ok_ref[0] = 0

    factored = ok_ref[0] == 1
    run, finish_hg, decay_ok = _hgrn_body(lbl_ref, hq_ref, fz_ref, hi_ref, hg_ref, ng_ref, jmat_ref, st_ref, upd_ref,
                                          layer=layer)
    scan, finish_rg = _rglru_body(x_ref, gate_ref, prev_ref, next_ref, cw_ref, cb_ref, wa_ref, ba_ref, wx_ref, bx_ref,
                                  lam_ref, h_ref, tile=tile, tiles_per_batch=tiles_per_batch)

    for fac in (True, False):
        chosen = factored if fac else jnp.logical_not(factored)

        @pl.when(jnp.logical_and(d == 0, chosen))
        def _():
            ofwd_ref[tile] = run(False, fac)
            hfwd_ref[tile] = scan(False)

        @pl.when(jnp.logical_and(d == 1, chosen))
        def _():
            yg_ref[...] = finish_hg(run(True, fac) + ofwd_ref[tile])
            yr_ref[...] = finish_rg(scan(True) + hfwd_ref[tile])

    ok_ref[0] = decay_ok(fzn_ref[...]).astype(jnp.int32)


def _mixers(ph, pr, lb_logits, norm_g_all, conv_w, conv_b, wa, ba, wx, bx, lam, *, layer, bsz, tiles_per_batch):
    n = ph.shape[0]
    depth = lb_logits.shape[0]
    hw = norm_g_all.shape[-1]
    rw = conv_b.shape[-1]
    head = jnp.arange(hw) // HG_DK
    same = head[:, None] == head[None, :]
    lt = 2 * HG_DK
    ntile = hw // lt
    hpt = TILE // HALO

    def seq_row(b, d, i):
        return _seq_row(b, _seq_tile(d, i, tiles_per_batch), bsz, tiles_per_batch)

    def col(cfn):
        return lambda b, d, i: (seq_row(b, d, i), cfn(d))

    def out_row(b, d, i):
        tile = jnp.where(d == 0, 0, _seq_tile(d, i, tiles_per_batch))
        return (_seq_row(b, tile, bsz, tiles_per_batch), 0)

    per_dir = lambda b, d, i: (layer, d, 0, 0)
    return pl.pallas_call(
        functools.partial(_mixers_kernel, layer=layer, tiles_per_batch=tiles_per_batch),
        grid=(bsz, 2, tiles_per_batch + 1),
        in_specs=[
            pl.BlockSpec((depth, 1, 1, hw), lambda b, d, i: (0, d, 0, 0)),
            pl.BlockSpec((TILE, hw), col(lambda d: 0)),
            pl.BlockSpec((TILE, hw), col(lambda d: 1 + d)),
            pl.BlockSpec((TILE, hw), col(lambda d: 3)),
            pl.BlockSpec((TILE, hw), col(lambda d: 4)),
            _layer_spec(norm_g_all, layer), _const_spec((hw, hw)),
            pl.BlockSpec((TILE, rw), col(lambda d: 0)),
            pl.BlockSpec((TILE, rw), col(lambda d: 1)),
            pl.BlockSpec((HALO, rw), lambda b, d, i: (jnp.maximum(seq_row(b, d, i) * hpt - 1, 0), 0)),
            pl.BlockSpec((HALO, rw), lambda b, d, i: (jnp.minimum((seq_row(b, d, i) + 1) * hpt, n // HALO - 1), 0)),
            _layer_spec(conv_w, layer), _layer_spec(conv_b, layer),
            pl.BlockSpec((None, 1, rw, rw), per_dir), pl.BlockSpec((None, 1, 1, rw), per_dir),
            pl.BlockSpec((None, 1, rw, rw), per_dir), pl.BlockSpec((None, 1, 1, rw), per_dir),
            pl.BlockSpec((None, 1, 1, rw), per_dir),
            pl.BlockSpec((TILE, hw), lambda b, d, i: (seq_row(b, d, jnp.minimum(i + 1, tiles_per_batch)), 1 + d)),
        ],
        out_specs=[pl.BlockSpec((TILE, hw), out_row), pl.BlockSpec((TILE, rw), out_row)],
        out_shape=[jax.ShapeDtypeStruct((n, hw), BF16), jax.ShapeDtypeStruct((n, rw), BF16)],
        scratch_shapes=[pltpu.VMEM((ntile, lt, lt), F32), pltpu.VMEM((tiles_per_batch + 1, TILE, hw), F32),
                        pltpu.VMEM((TILE // HG_CHUNK, ntile, lt, lt), F32),
                        pltpu.VMEM((SUBLANES, rw), F32), pltpu.VMEM((tiles_per_batch + 1, TILE, rw), F32),
                        pltpu.SMEM((1,), jnp.int32)],
        compiler_params=_cparams(("arbitrary", "arbitrary", "arbitrary")),
        name="mixers",
    )(lb_logits.reshape(depth, 2, 1, hw), ph, ph, ph, ph, norm_g_all, same.astype(BF16),
      pr, pr, pr, pr, conv_w, conv_b, wa, ba, wx, bx, lam, ph)


def _rope_tables(length):
    n_freq = HEAD_DIM // 4
    rows = length // GRID_W
    row = np.repeat(np.arange(rows), GRID_W).astype(np.float32)
    col = np.tile(np.arange(GRID_W), rows).astype(np.float32)
    inv_freq = (np.float32(ROPE_BASE) ** (-np.arange(n_freq, dtype=np.float32) / np.float32(n_freq))).astype(np.float32)
    ang = np.stack([row[:, None] * inv_freq, col[:, None] * inv_freq], axis=1)
    cos, sin = np.cos(ang), np.sin(ang)
    zero = np.zeros_like(sin[:, 0])
    cos_h = np.concatenate([cos[:, 0], cos[:, 0], cos[:, 1], cos[:, 1]], axis=-1)
    sa_h = np.concatenate([-sin[:, 0], zero, -sin[:, 1], zero], axis=-1)
    sb_h = np.concatenate([zero, sin[:, 0], zero, sin[:, 1]], axis=-1)
    ctx = (np.ones((TOKEN_TILE, HEAD_DIM), np.float32), np.zeros((TOKEN_TILE, HEAD_DIM), np.float32),
           np.zeros((TOKEN_TILE, HEAD_DIM), np.float32))
    return tuple(jnp.asarray(np.tile(np.concatenate([c, t], axis=0), (1, 2))) for c, t in zip(ctx, (cos_h, sa_h, sb_h)))


def _block_diag(wb):
    nb, bw = wb.shape[-3], wb.shape[-1]
    out = jnp.zeros(wb.shape[:-3] + (nb * bw, nb * bw), wb.dtype)
    for j in range(nb):
        out = out.at[..., j * bw:(j + 1) * bw, j * bw:(j + 1) * bw].set(wb[..., j, :, :])
    return out


def _regroup_heads(t, axis):
    shp = t.shape
    t = t.reshape(shp[:axis] + (ATT_KV_HEADS, ATT_GROUP, HEAD_DIM) + shp[axis + 1:])
    return jnp.swapaxes(t, axis, axis + 1).reshape(shp)


def kernel(x, c, ctx, c_ctx, ada_w, ada_b, norm_ffn1, ffn1_w_in, ffn1_w_out, norm_mix, w_in, w_out, attn_sink, hg_lb_logits, hg_norm, rg_conv_w, rg_conv_b, rg_gate_a_w, rg_gate_a_b, rg_gate_x_w, rg_gate_x_b, rg_lambda, norm_ffn2, ffn2_w_in, ffn2_w_out, final_norm):
    bsz, length, d = x.shape
    depth = ada_w.shape[0]
    assert ctx.shape[1] == CTX_LEN == TILE and length % TOKEN_TILE == 0 and (bsz * CTX_LEN) % TOKEN_TILE == 0
    tpb = length // TILE
    geo = dict(n_ctx_tiles=bsz * CTX_LEN // TOKEN_TILE, tiles_per_batch=length // TOKEN_TILE, ctx_row=bsz)

    ctx2, x2 = ctx.reshape(bsz * CTX_LEN, d), x.reshape(bsz * length, d)
    assert bsz + 1 <= SUBLANES
    cond = jnp.concatenate([c, c_ctx[None], jnp.zeros((SUBLANES - bsz - 1, d), F32)], axis=0)
    mods_all = _adaln(cond, ada_w, ada_b).reshape(depth, SUBLANES, N_MODS, d)
    tabs = _rope_tables(length)

    att_w = w_out.shape[1] // 2
    kv_w = ATT_KV_HEADS * HEAD_DIM
    hg_w = hg_norm.shape[1]
    rg_w = rg_conv_b.shape[1]
    widths = (att_w, kv_w, 5 * hg_w, 2 * rg_w)
    assert sum(widths) + kv_w == w_in.shape[2]

    to16 = lambda t: t.astype(BF16)
    f1_in, f1_out, f2_in, f2_out = map(to16, (ffn1_w_in, ffn1_w_out, ffn2_w_in, ffn2_w_out))
    wi16, wo16 = to16(w_in), to16(w_out)
    wi16 = wi16.at[:, :, :att_w].set(_regroup_heads(wi16[:, :, :att_w], 2))
    wo16 = wo16.at[:, :att_w].set(_regroup_heads(wo16[:, :att_w], 1))
    wa_bd, wx_bd = to16(_block_diag(rg_gate_a_w)), to16(_block_diag(rg_gate_x_w))
    as_row = lambda t: t.reshape(t.shape[:-1] + (1, t.shape[-1]))
    g1, gm, g2, hgn = map(as_row, (norm_ffn1, norm_mix, norm_ffn2, hg_norm))
    conv_b, ba, bx, lam = map(as_row, (rg_conv_b, rg_gate_a_b, rg_gate_x_b, rg_lambda))

    xs = (ctx2, x2)
    for l in range(depth):
        last = l == depth - 1
        xs, q, k, v, ph, pr = _token_block(xs, mods_all, g1, f1_in, f1_out, layer=l, mod0=0,
                                           proj=(gm, wi16, tabs, widths), **geo)
        yg, yr = _mixers(ph, pr, hg_lb_logits, hgn, rg_conv_w, conv_b, wa_bd, ba, wx_bd, bx, lam,
                         layer=l, bsz=bsz, tiles_per_batch=tpb)
        xs, = _token_block(xs, mods_all, g2, f2_in, f2_out, layer=l, mod0=6,
                           mix=(q, k, v, attn_sink, yg, yr, wo16, bsz, length),
                           final_g=final_norm if last else None, latent_only=last, **geo)
    return xs.reshape(bsz, length, d)
```
